```python
import jax, jax.numpy as jnp
from jax import lax
import numpy as np

D_MODEL = 1024
BATCH = 16
SEQ = 2048
DEPTH = 1

D_FF = 2816
PLE_DIM = 256
HG_HEADS = 4
HG_KDIM = 128
HG_VDIM = 128
HG_FDIM = HG_HEADS * HG_KDIM
HG_WIDTH = HG_HEADS * HG_VDIM
CHUNK = 64
MLA_HEADS = 4
Q_LORA = 256
KV_LORA = 128
NOPE_DIM = 128
ROPE_DIM = 64
V_DIM = 128
QK_DIM = NOPE_DIM + ROPE_DIM
MLA_WIDTH = MLA_HEADS * V_DIM
QBLOCK = 128
ROPE_THETA = 10000.0
MIX_WIDTH = HG_WIDTH + MLA_WIDTH
IN_SIZES = (HG_FDIM, HG_FDIM, HG_WIDTH, HG_WIDTH, Q_LORA, KV_LORA, ROPE_DIM)
IN_WIDTH = sum(IN_SIZES)
EPS = 1e-6

kernel_name = "hymba_hgrn2_mla_macaron_ple"


def rms_norm(x, g):
    xf = x.astype(jnp.float32)
    y = xf * lax.rsqrt(jnp.mean(xf * xf, axis=-1, keepdims=True) + EPS)
    return (y * g.astype(jnp.float32)).astype(x.dtype)


def swiglu(x, w_gate, w_up, w_down):
    return (jax.nn.silu(x @ w_gate) * (x @ w_up)) @ w_down


def apply_rope(x, cos, sin):
    x1, x2 = jnp.split(x.astype(jnp.float32), 2, axis=-1)
    return jnp.concatenate([x1 * cos - x2 * sin, x2 * cos + x1 * sin], axis=-1).astype(x.dtype)


def hgrn2_chunkwise(q, log_f, k, v):
    B, S, H, DK = q.shape
    DV = v.shape[-1]
    n_chunks = S // CHUNK

    def to_chunks(t):
        return t.reshape(B, n_chunks, CHUNK, H, t.shape[-1]).transpose(1, 0, 3, 2, 4)

    causal = jnp.tril(jnp.ones((CHUNK, CHUNK), dtype=bool))[:, :, None]

    def step(state, inp):
        q_c, g_c, k_c, v_c = inp
        b = jnp.cumsum(g_c, axis=2)
        diff = b[:, :, :, None, :] - b[:, :, None, :, :]
        decay = jnp.exp(jnp.where(causal, diff, -jnp.inf))
        scores = jnp.einsum('bhtd,bhsd,bhtsd->bhts', q_c, k_c, decay)
        o = (jnp.einsum('bhts,bhsv->bhtv', scores, v_c)
             + jnp.einsum('bhtd,bhdv->bhtv', q_c * jnp.exp(b), state))
        b_last = b[:, :, -1:, :]
        state = (state * jnp.exp(b_last)[:, :, 0, :, None]
                 + jnp.einsum('bhsd,bhsv->bhdv', k_c * jnp.exp(b_last - b), v_c))
        return state, o

    s0 = jnp.zeros((B, H, DK, DV), jnp.float32)
    _, o = lax.scan(step, s0, tuple(map(to_chunks, (q, log_f, k, v))))
    return o.transpose(1, 0, 3, 2, 4).reshape(B, S, H, DV)


def causal_block_attention(q, k, v):
    S = q.shape[2]
    scale = QK_DIM ** -0.5
    outs = []
    for j in range(S // QBLOCK):
        lo, hi = j * QBLOCK, (j + 1) * QBLOCK
        s = jnp.einsum('bhqd,bhkd->bhqk', q[:, :, lo:hi], k[:, :, :hi]).astype(jnp.float32) * scale
        mask = (lo + jnp.arange(QBLOCK))[:, None] >= jnp.arange(hi)[None, :]
        s = jnp.where(mask, s, -jnp.inf)
        pr = jax.nn.softmax(s, axis=-1).astype(v.dtype)
        outs.append(jnp.einsum('bhqk,bhkv->bhqv', pr, v[:, :, :hi]))
    return jnp.concatenate(outs, axis=2)


def setup_inputs(seed: int = 0) -> dict:
    key = jax.random.key(seed)
    ks = jax.random.split(key, 24)

    def w(k, shape, fan_in):
        return jax.random.normal(k, shape, jnp.float32) * fan_in ** -0.5

    def gain(k, shape):
        return 1.0 + 0.02 * jax.random.normal(k, shape, jnp.float32)

    offsets = jax.random.randint(ks[2], (BATCH, 1), 0, 1024, dtype=jnp.int32)
    positions = offsets + jnp.arange(SEQ, dtype=jnp.int32)[None, :]
    return {
        "x": jax.random.normal(ks[0], (BATCH, SEQ, D_MODEL), jnp.float32),
        "p": jax.random.normal(ks[1], (DEPTH, BATCH, SEQ, PLE_DIM), jnp.float32),
        "positions": positions,
        "ln_ffn1": gain(ks[3], (DEPTH, D_MODEL)),
        "w1_gate": w(ks[4], (DEPTH, D_MODEL, D_FF), D_MODEL),
        "w1_up": w(ks[5], (DEPTH, D_MODEL, D_FF), D_MODEL),
        "w1_down": w(ks[6], (DEPTH, D_FF, D_MODEL), D_FF),
        "ln_mix": gain(ks[7], (DEPTH, D_MODEL)),
        "w_in": w(ks[8], (DEPTH, D_MODEL, IN_WIDTH), D_MODEL),
        "hg_lb_logits": 0.5 * jax.random.normal(ks[9], (DEPTH + 1, HG_FDIM), jnp.float32),
        "hg_out_norm": gain(ks[10], (DEPTH, HG_HEADS, HG_VDIM)),
        "q_a_norm": gain(ks[11], (DEPTH, Q_LORA)),
        "w_q_up": w(ks[12], (DEPTH, Q_LORA, MLA_HEADS * QK_DIM), Q_LORA),
        "kv_a_norm": gain(ks[13], (DEPTH, KV_LORA)),
        "w_kv_up": w(ks[14], (DEPTH, KV_LORA, MLA_HEADS * (NOPE_DIM + V_DIM)), KV_LORA),
        "w_out": w(ks[15], (DEPTH, MIX_WIDTH, D_MODEL), MIX_WIDTH),
        "ln_ffn2": gain(ks[16], (DEPTH, D_MODEL)),
        "w2_gate": w(ks[17], (DEPTH, D_MODEL, D_FF), D_MODEL),
        "w2_up": w(ks[18], (DEPTH, D_MODEL, D_FF), D_MODEL),
        "w2_down": w(ks[19], (DEPTH, D_FF, D_MODEL), D_FF),
        "ln_ple": gain(ks[20], (DEPTH, D_MODEL)),
        "w_ple_gate": w(ks[21], (DEPTH, D_MODEL, D_MODEL), D_MODEL),
        "w_ple_proj": w(ks[22], (DEPTH, PLE_DIM, D_MODEL), PLE_DIM),
        "ln_final": gain(ks[23], (D_MODEL,)),
    }


def reference(x, p, positions, ln_ffn1, w1_gate, w1_up, w1_down, ln_mix, w_in,
              hg_lb_logits, hg_out_norm, q_a_norm, w_q_up, kv_a_norm, w_kv_up, w_out,
              ln_ffn2, w2_gate, w2_up, w2_down, ln_ple, w_ple_gate, w_ple_proj, ln_final):
    B, S, _ = x.shape
    split_at = [int(v) for v in np.cumsum(IN_SIZES)[:-1]]

    lower_bounds = jnp.cumsum(jax.nn.softmax(hg_lb_logits.astype(jnp.float32), axis=0), axis=0)

    half = ROPE_DIM // 2
    inv_freq = ROPE_THETA ** (-jnp.arange(half, dtype=jnp.float32) / half)
    ang = positions.astype(jnp.float32)[..., None] * inv_freq
    cos, sin = jnp.cos(ang), jnp.sin(ang)

    h = x
    for i in range(DEPTH):
        h = h + 0.5 * swiglu(rms_norm(h, ln_ffn1[i]), w1_gate[i], w1_up[i], w1_down[i])

        u = rms_norm(h, ln_mix[i]) @ w_in[i]
        hq, hf, hi_, hg, cq, ckv, kr = jnp.split(u, split_at, axis=-1)

        lb = lower_bounds[i]
        f_raw = hf.astype(jnp.float32)
        log_f = jnp.log(lb + (1.0 - lb) * jax.nn.sigmoid(f_raw))
        k_in = (1.0 - lb) * jax.nn.sigmoid(-f_raw)
        o_hg = hgrn2_chunkwise(
            hq.astype(jnp.float32).reshape(B, S, HG_HEADS, HG_KDIM),
            log_f.reshape(B, S, HG_HEADS, HG_KDIM),
            k_in.reshape(B, S, HG_HEADS, HG_KDIM),
            hi_.astype(jnp.float32).reshape(B, S, HG_HEADS, HG_VDIM)).astype(h.dtype)
        o_hg = rms_norm(o_hg, hg_out_norm[i]).reshape(B, S, HG_WIDTH) * jax.nn.silu(hg)

        q = (rms_norm(cq, q_a_norm[i]) @ w_q_up[i]).reshape(B, S, MLA_HEADS, QK_DIM)
        q_nope, q_rope = jnp.split(q, [NOPE_DIM], axis=-1)
        q_rope = apply_rope(q_rope, cos[:, :, None, :], sin[:, :, None, :])
        kv = (rms_norm(ckv, kv_a_norm[i]) @ w_kv_up[i]).reshape(B, S, MLA_HEADS, NOPE_DIM + V_DIM)
        k_nope, v = jnp.split(kv, [NOPE_DIM], axis=-1)
        k_rope = apply_rope(kr, cos, sin)
        k_rope = jnp.broadcast_to(k_rope[:, :, None, :], (B, S, MLA_HEADS, ROPE_DIM))
        qf = jnp.concatenate([q_nope, q_rope], axis=-1).transpose(0, 2, 1, 3)
        kf = jnp.concatenate([k_nope, k_rope], axis=-1).transpose(0, 2, 1, 3)
        o_mla = causal_block_attention(qf, kf, v.transpose(0, 2, 1, 3))
        o_mla = o_mla.transpose(0, 2, 1, 3).reshape(B, S, MLA_WIDTH)

        h = h + jnp.concatenate([o_hg, o_mla], axis=-1) @ w_out[i]

        h = h + 0.5 * swiglu(rms_norm(h, ln_ffn2[i]), w2_gate[i], w2_up[i], w2_down[i])

        gate = jax.nn.sigmoid(rms_norm(h, ln_ple[i]) @ w_ple_gate[i])
        h = h + gate * (p[i].astype(h.dtype) @ w_ple_proj[i])

    return rms_norm(h, ln_final)
```

```python
import functools

import jax
import jax.numpy as jnp
from jax import lax
from jax.experimental import pallas as pl
from jax.experimental.pallas import tpu as pltpu

F32 = jnp.float32
BF16 = jnp.bfloat16

D_MODEL = 1024
D_FF = 2816
PLE_DIM = 256
HG_HEADS = 4
HG_DIM = 128
HG_WIDTH = HG_HEADS * HG_DIM
MLA_HEADS = 4
Q_LORA = 256
KV_LORA = 128
NOPE_DIM = 128
ROPE_DIM = 64
V_DIM = 128
QK_DIM = NOPE_DIM + ROPE_DIM
ROPE_THETA = 10000.0
EPS = 1e-6

LANES = 128
SUBLANES = 8
HEAD_PAD = 2 * LANES
IN_PAD = 4 * HG_WIDTH + Q_LORA + KV_LORA + LANES

FFN_CHUNK = 256
TOKEN_TILE = 512
HGRN_CHUNK = 256
ATTN_BLOCK = 256
VMEM_LIMIT = 56 * 1024 * 1024

NT_DIMS = (((1,), (1,)), ((), ()))
TN_DIMS = (((0,), (0,)), ((), ()))


def _dot(a, b):
    return jnp.dot(a, b, preferred_element_type=F32)


def _rms(x, g):
    return x * lax.rsqrt(jnp.mean(x * x, axis=-1, keepdims=True) + EPS) * g


def _swiglu(xn, wg_ref, wu_ref, wd_ref, acc_ref):
    acc_ref[...] = jnp.zeros_like(acc_ref)

    def body(c, carry):
        g = _dot(xn, wg_ref[c])
        u = _dot(xn, wu_ref[c])
        a = (g * jax.nn.sigmoid(g) * u).astype(BF16)
        acc_ref[...] += _dot(a, wd_ref[c])
        return carry

    lax.fori_loop(0, wg_ref.shape[0], body, 0)
    return acc_ref[...]


def _rope(x, cos_t, sin_lo, sin_hi):
    return (x * cos_t + pltpu.roll(x, LANES - ROPE_DIM // 2, 1) * sin_lo
            + pltpu.roll(x, ROPE_DIM // 2, 1) * sin_hi)


def _premix_kernel(x_ref, pos_ref, invf_ref, g1_ref, wg_ref, wu_ref, wd_ref, gmix_ref, win_ref,
                   qan_ref, wq_ref, kvan_ref, wk_ref, wv_ref,
                   h1_ref, hq_ref, hf_ref, hi_ref, hg_ref, qp_ref, kp_ref, v_ref, acc_ref):
    x = x_ref[...]
    xn = _rms(x, g1_ref[...]).astype(BF16)
    h1 = x + 0.5 * _swiglu(xn, wg_ref, wu_ref, wd_ref, acc_ref)
    h1_ref[...] = h1

    hn = _rms(h1, gmix_ref[...]).astype(BF16)
    w = HG_WIDTH
    hq_ref[...] = _dot(hn, win_ref[:, 0:w])
    hf_ref[...] = _dot(hn, win_ref[:, w:2 * w])
    hi_ref[...] = _dot(hn, win_ref[:, 2 * w:3 * w])
    hg_ref[...] = _dot(hn, win_ref[:, 3 * w:4 * w])
    c0 = 4 * w
    cq = _dot(hn, win_ref[:, c0:c0 + Q_LORA])
    ckv = _dot(hn, win_ref[:, c0 + Q_LORA:c0 + Q_LORA + KV_LORA])
    kr = _dot(hn, win_ref[:, c0 + Q_LORA + KV_LORA:IN_PAD])

    ang = pos_ref[...].astype(F32) * invf_ref[...]
    lane = lax.broadcasted_iota(jnp.int32, ang.shape, 1)
    cos_a, sin_a = jnp.cos(ang), jnp.sin(ang)
    half = ROPE_DIM // 2
    cos_t = jnp.where(lane < ROPE_DIM, cos_a, 0.0)
    sin_lo = jnp.where(lane < half, -sin_a, 0.0)
    sin_hi = jnp.where((lane >= half) & (lane < ROPE_DIM), sin_a, 0.0)

    q = _dot(_rms(cq, qan_ref[...]).astype(BF16), wq_ref[...]) * (QK_DIM ** -0.5)
    ckvn = _rms(ckv, kvan_ref[...]).astype(BF16)
    k_nope = _dot(ckvn, wk_ref[...])
    v_ref[...] = _dot(ckvn, wv_ref[...]).astype(v_ref.dtype)
    k_rope = _rope(kr, cos_t, sin_lo, sin_hi).astype(kp_ref.dtype)
    for h in range(MLA_HEADS):
        a = h * HEAD_PAD
        qp_ref[:, a:a + NOPE_DIM] = q[:, a:a + NOPE_DIM].astype(qp_ref.dtype)
        qp_ref[:, a + NOPE_DIM:a + HEAD_PAD] = _rope(
            q[:, a + NOPE_DIM:a + HEAD_PAD], cos_t, sin_lo, sin_hi).astype(qp_ref.dtype)
        kp_ref[:, a:a + NOPE_DIM] = k_nope[:, h * NOPE_DIM:(h + 1) * NOPE_DIM].astype(kp_ref.dtype)
        kp_ref[:, a + NOPE_DIM:a + HEAD_PAD] = k_rope


def _level_mid(b, m):
    t = b.shape[0]
    if m >= SUBLANES:
        pieces = [jnp.broadcast_to(b[s + m - 1:s + m, :], (2 * m, LANES)) for s in range(0, t, 2 * m)]
        return pieces[0] if len(pieces) == 1 else jnp.concatenate(pieces, axis=0)
    b3 = b.reshape(t // SUBLANES, SUBLANES, LANES)
    sub = lax.broadcasted_iota(jnp.int32, b3.shape, 1)

    def row(i):
        return jnp.broadcast_to(b3[:, i:i + 1, :], b3.shape)

    out = row(m - 1)
    for s in range(2 * m, SUBLANES, 2 * m):
        out = jnp.where(sub >= s, row(s + m - 1), out)
    return out.reshape(t, LANES)


def _hgrn_kernel(lbl_ref, hq_ref, hf_ref, hi_ref, hg_ref, gn_ref, o_ref, st_ref):
    t = hq_ref.shape[0]

    @pl.when(pl.program_id(1) == 0)
    def _():
        st_ref[...] = jnp.zeros_like(st_ref)

    lg = lbl_ref[...]
    e = jnp.exp(lg - jnp.max(lg, axis=0, keepdims=True))
    lb = e[0:1, :] / jnp.sum(e, axis=0, keepdims=True)

    f_raw = hf_ref[...]
    g = jnp.log(lb + (1.0 - lb) * jax.nn.sigmoid(f_raw))
    kk = (1.0 - lb) * jax.nn.sigmoid(-f_raw)

    row = lax.broadcasted_iota(jnp.int32, (t, t), 0)
    col = lax.broadcasted_iota(jnp.int32, (t, t), 1)
    tri = (row >= col).astype(BF16)
    g1 = g.astype(BF16)
    r1 = g - g1.astype(F32)
    g2 = r1.astype(BF16)
    g3 = (r1 - g2.astype(F32)).astype(BF16)
    b_all = _dot(tri, g1) + _dot(tri, g2) + _dot(tri, g3)

    xm = jnp.where(row > col, row ^ col, 0)

    for h in range(HG_HEADS):
        sl = slice(h * HG_DIM, (h + 1) * HG_DIM)
        q = hq_ref[:, sl]
        k = kk[:, sl]
        v = hi_ref[:, sl]
        b = b_all[:, sl]
        vb = v.astype(BF16)

        scores = jnp.zeros((t, t), F32)
        m = 1
        while m < t:
            decay = jnp.exp(-jnp.abs(b - _level_mid(b, m)))
            s_l = lax.dot_general((q * decay).astype(BF16), (k * decay).astype(BF16), NT_DIMS,
                                  preferred_element_type=F32)
            scores = jnp.where((xm >= m) & (xm < 2 * m), s_l, scores)
            m *= 2

        st = st_ref[h]
        b_last = b[t - 1:t, :]
        o = (_dot(scores.astype(BF16), vb)
             + lax.dot_general((q * jnp.exp(b)).astype(BF16), st.astype(BF16), NT_DIMS,
                               preferred_element_type=F32)
             + jnp.sum(q * k, axis=-1, keepdims=True) * v)
        k_end = (k * jnp.exp(b_last - b)).astype(BF16)
        st_ref[h] = st * jnp.exp(b_last) + lax.dot_general(vb, k_end, TN_DIMS, preferred_element_type=F32)

        gate = hg_ref[:, sl]
        o_ref[:, sl] = (_rms(o, gn_ref[h:h + 1, :]) * (gate * jax.nn.sigmoid(gate))).astype(o_ref.dtype)


def _attn_kernel(q_ref, k_ref, v_ref, o_ref):
    tq = q_ref.shape[0]
    qi = pl.program_id(2)
    q = q_ref[...]

    def block(j, carry, diagonal):
        m, l, acc = carry
        start = pl.multiple_of(j * tq, tq)
        s = lax.dot_general(q, k_ref[pl.ds(start, tq), :], NT_DIMS, preferred_element_type=F32)
        if diagonal:
            row = lax.broadcasted_iota(jnp.int32, s.shape, 0)
            col = lax.broadcasted_iota(jnp.int32, s.shape, 1)
            s = jnp.where(row >= col, s, -jnp.inf)
        m_new = jnp.maximum(m, jnp.max(s, axis=-1, keepdims=True))
        alpha = jnp.exp(m - m_new)
        p = jnp.exp(s - m_new)
        l = alpha * l + jnp.sum(p, axis=-1, keepdims=True)
        acc = alpha * acc + _dot(p.astype(BF16), v_ref[pl.ds(start, tq), :])
        return m_new, l, acc

    carry = (jnp.full((tq, 1), -jnp.inf, F32), jnp.zeros((tq, 1), F32), jnp.zeros((tq, V_DIM), F32))
    carry = lax.fori_loop(0, qi, functools.partial(block, diagonal=False), carry)
    _, l, acc = block(qi, carry, diagonal=True)
    o_ref[...] = (acc / l).astype(o_ref.dtype)


def _postmix_kernel(h1_ref, ohg_ref, omla_ref, p_ref, wo_hg_ref, wo_mla_ref, g2_ref, wg_ref, wu_ref, wd_ref,
                    gple_ref, wpg_ref, wpp_ref, gfin_ref, y_ref, acc_ref):
    h2 = h1_ref[...] + _dot(ohg_ref[...], wo_hg_ref[...]) + _dot(omla_ref[...], wo_mla_ref[...])
    h3 = h2 + 0.5 * _swiglu(_rms(h2, g2_ref[...]).astype(BF16), wg_ref, wu_ref, wd_ref, acc_ref)
    gate = jax.nn.sigmoid(_dot(_rms(h3, gple_ref[...]).astype(BF16), wpg_ref[...]))
    h4 = h3 + gate * _dot(p_ref[...].astype(BF16), wpp_ref[...])
    y_ref[...] = _rms(h4, gfin_ref[...])


def _resident(shape):
    return pl.BlockSpec(shape, lambda *_: (0,) * len(shape), pipeline_mode=pl.Buffered(1))


def _rows(tile, width):
    return pl.BlockSpec((tile, width), lambda i: (i, 0))


def _ffn_weights(w_gate, w_up, w_down):
    nc = D_FF // FFN_CHUNK
    wg = w_gate.astype(BF16).reshape(D_MODEL, nc, FFN_CHUNK).transpose(1, 0, 2)
    wu = w_up.astype(BF16).reshape(D_MODEL, nc, FFN_CHUNK).transpose(1, 0, 2)
    wd = w_down.astype(BF16).reshape(nc, FFN_CHUNK, D_MODEL)
    return wg, wu, wd


def kernel(x, p, positions, ln_ffn1, w1_gate, w1_up, w1_down, ln_mix, w_in, hg_lb_logits, hg_out_norm,
           q_a_norm, w_q_up, kv_a_norm, w_kv_up, w_out, ln_ffn2, w2_gate, w2_up, w2_down, ln_ple,
           w_ple_gate, w_ple_proj, ln_final):
    bsz, seq, _ = x.shape
    assert p.shape[0] == 1 and hg_lb_logits.shape[0] == 2, "single-layer trunk"
    n = bsz * seq
    tm = min(TOKEN_TILE, n)
    tc = min(HGRN_CHUNK, seq)
    tq = min(ATTN_BLOCK, seq)
    assert n % tm == 0 and seq % tc == 0 and seq % tq == 0
    nc = D_FF // FFN_CHUNK

    x2 = x.reshape(n, D_MODEL)
    pos2 = positions.reshape(n, 1)
    half = ROPE_DIM // 2
    inv_freq = ROPE_THETA ** (-jnp.arange(half, dtype=F32) / half)
    invf = jnp.concatenate([inv_freq, inv_freq, jnp.zeros((LANES - ROPE_DIM,), F32)]).reshape(1, LANES)
    wg1, wu1, wd1 = _ffn_weights(w1_gate[0], w1_up[0], w1_down[0])
    wg2, wu2, wd2 = _ffn_weights(w2_gate[0], w2_up[0], w2_down[0])
    win = jnp.pad(w_in[0].astype(BF16), ((0, 0), (0, IN_PAD - w_in.shape[-1])))
    wq = jnp.pad(w_q_up[0].astype(BF16).reshape(Q_LORA, MLA_HEADS, QK_DIM),
                 ((0, 0), (0, 0), (0, HEAD_PAD - QK_DIM))).reshape(Q_LORA, MLA_HEADS * HEAD_PAD)
    wkv = w_kv_up[0].astype(BF16).reshape(KV_LORA, MLA_HEADS, NOPE_DIM + V_DIM)
    wk = wkv[:, :, :NOPE_DIM].reshape(KV_LORA, MLA_HEADS * NOPE_DIM)
    wv = wkv[:, :, NOPE_DIM:].reshape(KV_LORA, MLA_HEADS * V_DIM)
    wo = w_out[0].astype(BF16)
    row = lambda a: a.reshape(1, -1)

    h1, hq, hf, hi, hg, qp, kp, vv = pl.pallas_call(
        _premix_kernel,
        name="premix",
        grid=(n // tm,),
        in_specs=[
            _rows(tm, D_MODEL), _rows(tm, 1), _resident((1, LANES)), _resident((1, D_MODEL)),
            _resident((nc, D_MODEL, FFN_CHUNK)), _resident((nc, D_MODEL, FFN_CHUNK)),
            _resident((nc, FFN_CHUNK, D_MODEL)), _resident((1, D_MODEL)), _resident((D_MODEL, IN_PAD)),
            _resident((1, Q_LORA)), _resident((Q_LORA, MLA_HEADS * HEAD_PAD)), _resident((1, KV_LORA)),
            _resident((KV_LORA, MLA_HEADS * NOPE_DIM)), _resident((KV_LORA, MLA_HEADS * V_DIM)),
        ],
        out_specs=[
            _rows(tm, D_MODEL), _rows(tm, HG_WIDTH), _rows(tm, HG_WIDTH), _rows(tm, HG_WIDTH),
            _rows(tm, HG_WIDTH), _rows(tm, MLA_HEADS * HEAD_PAD), _rows(tm, MLA_HEADS * HEAD_PAD),
            _rows(tm, MLA_HEADS * V_DIM),
        ],
        out_shape=[
            jax.ShapeDtypeStruct((n, D_MODEL), F32),
            jax.ShapeDtypeStruct((n, HG_WIDTH), F32), jax.ShapeDtypeStruct((n, HG_WIDTH), F32),
            jax.ShapeDtypeStruct((n, HG_WIDTH), F32), jax.ShapeDtypeStruct((n, HG_WIDTH), F32),
            jax.ShapeDtypeStruct((n, MLA_HEADS * HEAD_PAD), BF16),
            jax.ShapeDtypeStruct((n, MLA_HEADS * HEAD_PAD), BF16),
            jax.ShapeDtypeStruct((n, MLA_HEADS * V_DIM), BF16),
        ],
        scratch_shapes=[pltpu.VMEM((tm, D_MODEL), F32)],
        compiler_params=pltpu.CompilerParams(dimension_semantics=("parallel",), vmem_limit_bytes=VMEM_LIMIT),
    )(x2, pos2, invf, row(ln_ffn1[0]), wg1, wu1, wd1, row(ln_mix[0]), win,
      row(q_a_norm[0]), wq, row(kv_a_norm[0]), wk, wv)

    steps = seq // tc
    chunk_spec = pl.BlockSpec((tc, HG_WIDTH), lambda b, c: (b * steps + c, 0))
    o_hg = pl.pallas_call(
        _hgrn_kernel,
        name="hgrn2",
        grid=(bsz, steps),
        in_specs=[pl.BlockSpec((2, HG_WIDTH), lambda b, c: (0, 0)), chunk_spec, chunk_spec, chunk_spec,
                  chunk_spec, pl.BlockSpec((HG_HEADS, HG_DIM), lambda b, c: (0, 0))],
        out_specs=chunk_spec,
        out_shape=jax.ShapeDtypeStruct((n, HG_WIDTH), BF16),
        scratch_shapes=[pltpu.VMEM((HG_HEADS, HG_DIM, HG_DIM), F32)],
        compiler_params=pltpu.CompilerParams(dimension_semantics=("parallel", "arbitrary"),
                                             vmem_limit_bytes=VMEM_LIMIT),
    )(hg_lb_logits, hq, hf, hi, hg, hg_out_norm[0])

    qsteps = seq // tq
    o_mla = pl.pallas_call(
        _attn_kernel,
        name="mla_attn",
        grid=(bsz, MLA_HEADS, qsteps),
        in_specs=[pl.BlockSpec((tq, HEAD_PAD), lambda b, h, i: (b * qsteps + i, h)),
                  pl.BlockSpec((seq, HEAD_PAD), lambda b, h, i: (b, h)),
                  pl.BlockSpec((seq, V_DIM), lambda b, h, i: (b, h))],
        out_specs=pl.BlockSpec((tq, V_DIM), lambda b, h, i: (b * qsteps + i, h)),
        out_shape=jax.ShapeDtypeStruct((n, MLA_HEADS * V_DIM), BF16),
        compiler_params=pltpu.CompilerParams(dimension_semantics=("parallel", "parallel", "arbitrary"),
                                             vmem_limit_bytes=VMEM_LIMIT),
    )(qp, kp, vv)

    y = pl.pallas_call(
        _postmix_kernel,
        name="postmix",
        grid=(n // tm,),
        in_specs=[
            _rows(tm, D_MODEL), _rows(tm, HG_WIDTH), _rows(tm, MLA_HEADS * V_DIM), _rows(tm, PLE_DIM),
            _resident((HG_WIDTH, D_MODEL)), _resident((MLA_HEADS * V_DIM, D_MODEL)), _resident((1, D_MODEL)),
            _resident((nc, D_MODEL, FFN_CHUNK)), _resident((nc, D_MODEL, FFN_CHUNK)),
            _resident((nc, FFN_CHUNK, D_MODEL)), _resident((1, D_MODEL)), _resident((D_MODEL, D_MODEL)),
            _resident((PLE_DIM, D_MODEL)), _resident((1, D_MODEL)),
        ],
        out_specs=_rows(tm, D_MODEL),
        out_shape=jax.ShapeDtypeStruct((n, D_MODEL), F32),
        scratch_shapes=[pltpu.VMEM((tm, D_MODEL), F32)],
        compiler_params=pltpu.CompilerParams(dimension_semantics=("parallel",), vmem_limit_bytes=VMEM_LIMIT),
    )(h1, o_hg, o_mla, p[0].reshape(n, PLE_DIM), wo[:HG_WIDTH], wo[HG_WIDTH:], row(ln_ffn2[0]),
      wg2, wu2, wd2, row(ln_ple[0]), w_ple_gate[0].astype(BF16), w_ple_proj[0].astype(BF16), row(ln_final))

    return y.reshape(bsz, seq, D_MODEL)
```

```python
import functools

import jax
import jax.numpy as jnp
from jax import lax
from jax.experimental import pallas as pl
from jax.experimental.pallas import tpu as pltpu

F32 = jnp.float32
BF16 = jnp.bfloat16

D_MODEL = 1024
D_FF = 2816
PLE_DIM = 256
HG_HEADS = 4
HG_DIM = 128
HG_WIDTH = HG_HEADS * HG_DIM
MLA_HEADS = 4
Q_LORA = 256
KV_LORA = 128
NOPE_DIM = 128
ROPE_DIM = 64
V_DIM = 128
QK_DIM = NOPE_DIM + ROPE_DIM
ROPE_THETA = 10000.0
EPS = 1e-6
LOG2_E = 1.4426950408889634

LANES = 128
SUBLANES = 8
HEAD_PAD = 2 * LANES
IN_PAD = 4 * HG_WIDTH + Q_LORA + KV_LORA + LANES

FFN_CHUNK = 256
TOKEN_TILE = 512
HGRN_CHUNK = 256
ATTN_BLOCK = 256
VMEM_LIMIT = 56 * 1024 * 1024

NT_DIMS = (((1,), (1,)), ((), ()))
TN_DIMS = (((0,), (0,)), ((), ()))


def _dot(a, b):
    return jnp.dot(a, b, preferred_element_type=F32)


def _rms(x, g):
    return x * lax.rsqrt(jnp.mean(x * x, axis=-1, keepdims=True) + EPS) * g


def _swiglu(xn, wg_ref, wu_ref, wd_ref, acc_ref):
    acc_ref[...] = jnp.zeros_like(acc_ref)

    def body(c, carry):
        g = _dot(xn, wg_ref[c])
        u = _dot(xn, wu_ref[c])
        a = (g * jax.nn.sigmoid(g) * u).astype(BF16)
        acc_ref[...] += _dot(a, wd_ref[c])
        return carry

    lax.fori_loop(0, wg_ref.shape[0], body, 0)
    return acc_ref[...]


def _rope(x, cos_t, sin_lo, sin_hi):
    return (x * cos_t + pltpu.roll(x, LANES - ROPE_DIM // 2, 1) * sin_lo
            + pltpu.roll(x, ROPE_DIM // 2, 1) * sin_hi)


def _premix_kernel(x_ref, pos_ref, invf_ref, g1_ref, wg_ref, wu_ref, wd_ref, gmix_ref, win_ref,
                   qan_ref, wq_ref, kvan_ref, wk_ref, wv_ref,
                   h1_ref, hq_ref, hf_ref, hi_ref, hg_ref, qp_ref, kp_ref, v_ref, acc_ref):
    x = x_ref[...]
    xn = _rms(x, g1_ref[...]).astype(BF16)
    h1 = x + 0.5 * _swiglu(xn, wg_ref, wu_ref, wd_ref, acc_ref)
    h1_ref[...] = h1

    hn = _rms(h1, gmix_ref[...]).astype(BF16)
    w = HG_WIDTH
    hq_ref[...] = _dot(hn, win_ref[:, 0:w])
    hf_ref[...] = _dot(hn, win_ref[:, w:2 * w])
    hi_ref[...] = _dot(hn, win_ref[:, 2 * w:3 * w])
    hg_ref[...] = _dot(hn, win_ref[:, 3 * w:4 * w])
    c0 = 4 * w
    cq = _dot(hn, win_ref[:, c0:c0 + Q_LORA])
    ckv = _dot(hn, win_ref[:, c0 + Q_LORA:c0 + Q_LORA + KV_LORA])
    kr = _dot(hn, win_ref[:, c0 + Q_LORA + KV_LORA:IN_PAD])

    ang = pos_ref[...].astype(F32) * invf_ref[...]
    lane = lax.broadcasted_iota(jnp.int32, ang.shape, 1)
    cos_a, sin_a = jnp.cos(ang), jnp.sin(ang)
    half = ROPE_DIM // 2
    cos_t = jnp.where(lane < ROPE_DIM, cos_a, 0.0)
    sin_lo = jnp.where(lane < half, -sin_a, 0.0)
    sin_hi = jnp.where((lane >= half) & (lane < ROPE_DIM), sin_a, 0.0)

    q = _dot(_rms(cq, qan_ref[...]).astype(BF16), wq_ref[...]) * (QK_DIM ** -0.5 * LOG2_E)
    ckvn = _rms(ckv, kvan_ref[...]).astype(BF16)
    k_nope = _dot(ckvn, wk_ref[...])
    v_ref[...] = _dot(ckvn, wv_ref[...]).astype(v_ref.dtype)
    k_rope = _rope(kr, cos_t, sin_lo, sin_hi).astype(kp_ref.dtype)
    for h in range(MLA_HEADS):
        a = h * HEAD_PAD
        qp_ref[:, a:a + NOPE_DIM] = q[:, a:a + NOPE_DIM].astype(qp_ref.dtype)
        qp_ref[:, a + NOPE_DIM:a + HEAD_PAD] = _rope(
            q[:, a + NOPE_DIM:a + HEAD_PAD], cos_t, sin_lo, sin_hi).astype(qp_ref.dtype)
        kp_ref[:, a:a + NOPE_DIM] = k_nope[:, h * NOPE_DIM:(h + 1) * NOPE_DIM].astype(kp_ref.dtype)
        kp_ref[:, a + NOPE_DIM:a + HEAD_PAD] = k_rope


def _level_mid(b, m):
    t = b.shape[0]
    if m >= SUBLANES:
        pieces = [jnp.broadcast_to(b[s + m - 1:s + m, :], (2 * m, LANES)) for s in range(0, t, 2 * m)]
        return pieces[0] if len(pieces) == 1 else jnp.concatenate(pieces, axis=0)
    b3 = b.reshape(t // SUBLANES, SUBLANES, LANES)
    sub = lax.broadcasted_iota(jnp.int32, b3.shape, 1)

    def row(i):
        return jnp.broadcast_to(b3[:, i:i + 1, :], b3.shape)

    out = row(m - 1)
    for s in range(2 * m, SUBLANES, 2 * m):
        out = jnp.where(sub >= s, row(s + m - 1), out)
    return out.reshape(t, LANES)


def _hgrn_kernel(lbl_ref, hq_ref, hf_ref, hi_ref, hg_ref, gn_ref, o_ref, st_ref):
    t = hq_ref.shape[0]

    @pl.when(pl.program_id(1) == 0)
    def _():
        st_ref[...] = jnp.zeros_like(st_ref)

    lg = lbl_ref[...]
    e = jnp.exp(lg - jnp.max(lg, axis=0, keepdims=True))
    lb = e[0:1, :] / jnp.sum(e, axis=0, keepdims=True)

    f_raw = hf_ref[...]
    g = jnp.log(lb + (1.0 - lb) * jax.nn.sigmoid(f_raw))
    kk = (1.0 - lb) * jax.nn.sigmoid(-f_raw)

    row = lax.broadcasted_iota(jnp.int32, (t, t), 0)
    col = lax.broadcasted_iota(jnp.int32, (t, t), 1)
    tri = (row >= col).astype(BF16)
    g1 = g.astype(BF16)
    r1 = g - g1.astype(F32)
    g2 = r1.astype(BF16)
    g3 = (r1 - g2.astype(F32)).astype(BF16)
    b_all = _dot(tri, g1) + _dot(tri, g2) + _dot(tri, g3)

    xm = jnp.where(row > col, row ^ col, 0)

    for h in range(HG_HEADS):
        sl = slice(h * HG_DIM, (h + 1) * HG_DIM)
        q = hq_ref[:, sl]
        k = kk[:, sl]
        v = hi_ref[:, sl]
        b = b_all[:, sl]
        vb = v.astype(BF16)

        scores = jnp.zeros((t, t), F32)
        m = 1
        while m < t:
            decay = jnp.exp(-jnp.abs(b - _level_mid(b, m)))
            s_l = lax.dot_general((q * decay).astype(BF16), (k * decay).astype(BF16), NT_DIMS,
                                  preferred_element_type=F32)
            scores = jnp.where((xm >= m) & (xm < 2 * m), s_l, scores)
            m *= 2

        st = st_ref[h]
        b_last = b[t - 1:t, :]
        o = (_dot(scores.astype(BF16), vb)
             + lax.dot_general((q * jnp.exp(b)).astype(BF16), st.astype(BF16), NT_DIMS,
                               preferred_element_type=F32)
             + jnp.sum(q * k, axis=-1, keepdims=True) * v)
        k_end = (k * jnp.exp(b_last - b)).astype(BF16)
        st_ref[h] = st * jnp.exp(b_last) + lax.dot_general(vb, k_end, TN_DIMS, preferred_element_type=F32)

        gate = hg_ref[:, sl]
        o_ref[:, sl] = (_rms(o, gn_ref[h:h + 1, :]) * (gate * jax.nn.sigmoid(gate))).astype(o_ref.dtype)


def _attn_kernel(q_ref, k_ref, v_ref, o_ref, vaug_ref, *, tq):
    seq = q_ref.shape[0]
    vaug_ref[:, 0:V_DIM] = v_ref[...]
    vaug_ref[:, V_DIM:2 * V_DIM] = jnp.ones((seq, V_DIM), vaug_ref.dtype)
    row = lax.broadcasted_iota(jnp.int32, (tq, tq), 0)
    col = lax.broadcasted_iota(jnp.int32, (tq, tq), 1)
    for i in range(seq // tq):
        lo = i * tq
        q = q_ref[lo:lo + tq, :]
        s_d = lax.dot_general(q, k_ref[lo:lo + tq, :], NT_DIMS, preferred_element_type=F32)
        s_d = jnp.where(row >= col, s_d, -jnp.inf)
        m = jnp.max(s_d, axis=-1, keepdims=True)
        if i > 0:
            s_p = lax.dot_general(q, k_ref[0:lo, :], NT_DIMS, preferred_element_type=F32)
            m = jnp.maximum(m, jnp.max(s_p, axis=-1, keepdims=True))
        acc = _dot(jnp.exp2(s_d - m).astype(BF16), vaug_ref[lo:lo + tq, :])
        if i > 0:
            acc += _dot(jnp.exp2(s_p - m).astype(BF16), vaug_ref[0:lo, :])
        o_ref[lo:lo + tq, :] = (acc[:, 0:V_DIM] / acc[:, V_DIM:2 * V_DIM]).astype(o_ref.dtype)


def _postmix_kernel(h1_ref, ohg_ref, omla_ref, p_ref, wo_hg_ref, wo_mla_ref, g2_ref, wg_ref, wu_ref, wd_ref,
                    gple_ref, wpg_ref, wpp_ref, gfin_ref, y_ref, acc_ref):
    h2 = h1_ref[...] + _dot(ohg_ref[...], wo_hg_ref[...]) + _dot(omla_ref[...], wo_mla_ref[...])
    h3 = h2 + 0.5 * _swiglu(_rms(h2, g2_ref[...]).astype(BF16), wg_ref, wu_ref, wd_ref, acc_ref)
    gate = jax.nn.sigmoid(_dot(_rms(h3, gple_ref[...]).astype(BF16), wpg_ref[...]))
    h4 = h3 + gate * _dot(p_ref[...].astype(BF16), wpp_ref[...])
    y_ref[...] = _rms(h4, gfin_ref[...])


def _resident(shape):
    return pl.BlockSpec(shape, lambda *_: (0,) * len(shape), pipeline_mode=pl.Buffered(1))


def _rows(tile, width):
    return pl.BlockSpec((tile, width), lambda i: (i, 0))


def _ffn_weights(w_gate, w_up, w_down):
    nc = D_FF // FFN_CHUNK
    wg = w_gate.astype(BF16).reshape(D_MODEL, nc, FFN_CHUNK).transpose(1, 0, 2)
    wu = w_up.astype(BF16).reshape(D_MODEL, nc, FFN_CHUNK).transpose(1, 0, 2)
    wd = w_down.astype(BF16).reshape(nc, FFN_CHUNK, D_MODEL)
    return wg, wu, wd


def kernel(x, p, positions, ln_ffn1, w1_gate, w1_up, w1_down, ln_mix, w_in, hg_lb_logits, hg_out_norm,
           q_a_norm, w_q_up, kv_a_norm, w_kv_up, w_out, ln_ffn2, w2_gate, w2_up, w2_down, ln_ple,
           w_ple_gate, w_ple_proj, ln_final):
    bsz, seq, _ = x.shape
    assert p.shape[0] == 1 and hg_lb_logits.shape[0] == 2, "single-layer trunk"
    n = bsz * seq
    tm = min(TOKEN_TILE, n)
    tc = min(HGRN_CHUNK, seq)
    tq = min(ATTN_BLOCK, seq)
    assert n % tm == 0 and seq % tc == 0 and seq % tq == 0
    nc = D_FF // FFN_CHUNK

    x2 = x.reshape(n, D_MODEL)
    pos2 = positions.reshape(n, 1)
    half = ROPE_DIM // 2
    inv_freq = ROPE_THETA ** (-jnp.arange(half, dtype=F32) / half)
    invf = jnp.concatenate([inv_freq, inv_freq, jnp.zeros((LANES - ROPE_DIM,), F32)]).reshape(1, LANES)
    wg1, wu1, wd1 = _ffn_weights(w1_gate[0], w1_up[0], w1_down[0])
    wg2, wu2, wd2 = _ffn_weights(w2_gate[0], w2_up[0], w2_down[0])
    win = jnp.pad(w_in[0].astype(BF16), ((0, 0), (0, IN_PAD - w_in.shape[-1])))
    wq = jnp.pad(w_q_up[0].astype(BF16).reshape(Q_LORA, MLA_HEADS, QK_DIM),
                 ((0, 0), (0, 0), (0, HEAD_PAD - QK_DIM))).reshape(Q_LORA, MLA_HEADS * HEAD_PAD)
    wkv = w_kv_up[0].astype(BF16).reshape(KV_LORA, MLA_HEADS, NOPE_DIM + V_DIM)
    wk = wkv[:, :, :NOPE_DIM].reshape(KV_LORA, MLA_HEADS * NOPE_DIM)
    wv = wkv[:, :, NOPE_DIM:].reshape(KV_LORA, MLA_HEADS * V_DIM)
    wo = w_out[0].astype(BF16)
    row = lambda a: a.reshape(1, -1)

    h1, hq, hf, hi, hg, qp, kp, vv = pl.pallas_call(
        _premix_kernel,
        name="premix",
        grid=(n // tm,),
        in_specs=[
            _rows(tm, D_MODEL), _rows(tm, 1), _resident((1, LANES)), _resident((1, D_MODEL)),
            _resident((nc, D_MODEL, FFN_CHUNK)), _resident((nc, D_MODEL, FFN_CHUNK)),
            _resident((nc, FFN_CHUNK, D_MODEL)), _resident((1, D_MODEL)), _resident((D_MODEL, IN_PAD)),
            _resident((1, Q_LORA)), _resident((Q_LORA, MLA_HEADS * HEAD_PAD)), _resident((1, KV_LORA)),
            _resident((KV_LORA, MLA_HEADS * NOPE_DIM)), _resident((KV_LORA, MLA_HEADS * V_DIM)),
        ],
        out_specs=[
            _rows(tm, D_MODEL), _rows(tm, HG_WIDTH), _rows(tm, HG_WIDTH), _rows(tm, HG_WIDTH),
            _rows(tm, HG_WIDTH), _rows(tm, MLA_HEADS * HEAD_PAD), _rows(tm, MLA_HEADS * HEAD_PAD),
            _rows(tm, MLA_HEADS * V_DIM),
        ],
        out_shape=[
            jax.ShapeDtypeStruct((n, D_MODEL), F32),
            jax.ShapeDtypeStruct((n, HG_WIDTH), F32), jax.ShapeDtypeStruct((n, HG_WIDTH), F32),
            jax.ShapeDtypeStruct((n, HG_WIDTH), F32), jax.ShapeDtypeStruct((n, HG_WIDTH), F32),
            jax.ShapeDtypeStruct((n, MLA_HEADS * HEAD_PAD), BF16),
            jax.ShapeDtypeStruct((n, MLA_HEADS * HEAD_PAD), BF16),
            jax.ShapeDtypeStruct((n, MLA_HEADS * V_DIM), BF16),
        ],
        scratch_shapes=[pltpu.VMEM((tm, D_MODEL), F32)],
        compiler_params=pltpu.CompilerParams(dimension_semantics=("parallel",), vmem_limit_bytes=VMEM_LIMIT),
    )(x2, pos2, invf, row(ln_ffn1[0]), wg1, wu1, wd1, row(ln_mix[0]), win,
      row(q_a_norm[0]), wq, row(kv_a_norm[0]), wk, wv)

    steps = seq // tc
    chunk_spec = pl.BlockSpec((tc, HG_WIDTH), lambda b, c: (b * steps + c, 0))
    o_hg = pl.pallas_call(
        _hgrn_kernel,
        name="hgrn2",
        grid=(bsz, steps),
        in_specs=[pl.BlockSpec((2, HG_WIDTH), lambda b, c: (0, 0)), chunk_spec, chunk_spec, chunk_spec,
                  chunk_spec, pl.BlockSpec((HG_HEADS, HG_DIM), lambda b, c: (0, 0))],
        out_specs=chunk_spec,
        out_shape=jax.ShapeDtypeStruct((n, HG_WIDTH), BF16),
        scratch_shapes=[pltpu.VMEM((HG_HEADS, HG_DIM, HG_DIM), F32)],
        compiler_params=pltpu.CompilerParams(dimension_semantics=("parallel", "arbitrary"),
                                             vmem_limit_bytes=VMEM_LIMIT),
    )(hg_lb_logits, hq, hf, hi, hg, hg_out_norm[0])

    head_spec = lambda width: pl.BlockSpec((seq, width), lambda b, h: (b, h))
    o_mla = pl.pallas_call(
        functools.partial(_attn_kernel, tq=tq),
        name="mla_attn",
        grid=(bsz, MLA_HEADS),
        in_specs=[head_spec(HEAD_PAD), head_spec(HEAD_PAD), head_spec(V_DIM)],
        out_specs=head_spec(V_DIM),
        out_shape=jax.ShapeDtypeStruct((n, MLA_HEADS * V_DIM), BF16),
        scratch_shapes=[pltpu.VMEM((seq, 2 * V_DIM), BF16)],
        compiler_params=pltpu.CompilerParams(dimension_semantics=("parallel", "parallel"),
                                             vmem_limit_bytes=VMEM_LIMIT),
    )(qp, kp, vv)

    y = pl.pallas_call(
        _postmix_kernel,
        name="postmix",
        grid=(n // tm,),
        in_specs=[
            _rows(tm, D_MODEL), _rows(tm, HG_WIDTH), _rows(tm, MLA_HEADS * V_DIM), _rows(tm, PLE_DIM),
            _resident((HG_WIDTH, D_MODEL)), _resident((MLA_HEADS * V_DIM, D_MODEL)), _resident((1, D_MODEL)),
            _resident((nc, D_MODEL, FFN_CHUNK)), _resident((nc, D_MODEL, FFN_CHUNK)),
            _resident((nc, FFN_CHUNK, D_MODEL)), _resident((1, D_MODEL)), _resident((D_MODEL, D_MODEL)),
            _resident((PLE_DIM, D_MODEL)), _resident((1, D_MODEL)),
        ],
        out_specs=_rows(tm, D_MODEL),
        out_shape=jax.ShapeDtypeStruct((n, D_MODEL), F32),
        scratch_shapes=[pltpu.VMEM((tm, D_MODEL), F32)],
        compiler_params=pltpu.CompilerParams(dimension_semantics=("parallel",), vmem_limit_bytes=VMEM_LIMIT),
    )(h1, o_hg, o_mla, p[0].reshape(n, PLE_DIM), wo[:HG_WIDTH], wo[HG_WIDTH:], row(ln_ffn2[0]),
      wg2, wu2, wd2, row(ln_ple[0]), w_ple_gate[0].astype(BF16), w_ple_proj[0].astype(BF16), row(ln_final))

    return y.reshape(bsz, seq, D_MODEL)
```

```python
import functools

import jax
import jax.numpy as jnp
from jax import lax
from jax.experimental import pallas as pl
from jax.experimental.pallas import tpu as pltpu

F32 = jnp.float32
BF16 = jnp.bfloat16

D_MODEL = 1024
D_FF = 2816
PLE_DIM = 256
HG_HEADS = 4
HG_DIM = 128
HG_WIDTH = HG_HEADS * HG_DIM
MLA_HEADS = 4
Q_LORA = 256
KV_LORA = 128
NOPE_DIM = 128
ROPE_DIM = 64
V_DIM = 128
QK_DIM = NOPE_DIM + ROPE_DIM
ROPE_THETA = 10000.0
EPS = 1e-6
LOG2_E = 1.4426950408889634

LANES = 128
SUBLANES = 8
HEAD_PAD = 2 * LANES
IN_PAD = 4 * HG_WIDTH + Q_LORA + KV_LORA + LANES

FFN_CHUNK = 256
TOKEN_TILE = 512
HGRN_CHUNK = 256
ATTN_BLOCK = 256
VMEM_LIMIT = 56 * 1024 * 1024

NT_DIMS = (((1,), (1,)), ((), ()))
TN_DIMS = (((0,), (0,)), ((), ()))


def _dot(a, b):
    return jnp.dot(a, b, preferred_element_type=F32)


def _rms(x, g):
    return x * lax.rsqrt(jnp.mean(x * x, axis=-1, keepdims=True) + EPS) * g


def _swiglu(xn, wg_ref, wu_ref, wd_ref):
    acc = None
    for c in range(wg_ref.shape[0]):
        g = _dot(xn, wg_ref[c])
        u = _dot(xn, wu_ref[c])
        a = (g * jax.nn.sigmoid(g) * u).astype(BF16)
        d = _dot(a, wd_ref[c])
        acc = d if acc is None else acc + d
    return acc


def _rope(x, cos_t, sin_lo, sin_hi):
    return (x * cos_t + pltpu.roll(x, LANES - ROPE_DIM // 2, 1) * sin_lo
            + pltpu.roll(x, ROPE_DIM // 2, 1) * sin_hi)


def _premix_kernel(x_ref, pos_ref, invf_ref, g1_ref, wg_ref, wu_ref, wd_ref, gmix_ref, win_ref,
                   qan_ref, wq_ref, kvan_ref, wk_ref, wv_ref,
                   h1_ref, hq_ref, hf_ref, hi_ref, hg_ref, qp_ref, kp_ref, v_ref):
    x = x_ref[...]
    xn = _rms(x, g1_ref[...]).astype(BF16)
    h1 = x + 0.5 * _swiglu(xn, wg_ref, wu_ref, wd_ref)
    h1_ref[...] = h1

    hn = _rms(h1, gmix_ref[...]).astype(BF16)
    w = HG_WIDTH
    hq_ref[...] = _dot(hn, win_ref[:, 0:w])
    hf_ref[...] = _dot(hn, win_ref[:, w:2 * w])
    hi_ref[...] = _dot(hn, win_ref[:, 2 * w:3 * w])
    hg_ref[...] = _dot(hn, win_ref[:, 3 * w:4 * w])
    c0 = 4 * w
    cq = _dot(hn, win_ref[:, c0:c0 + Q_LORA])
    ckv_kr = _dot(hn, win_ref[:, c0 + Q_LORA:IN_PAD])
    ckv = ckv_kr[:, 0:KV_LORA]
    kr = ckv_kr[:, KV_LORA:KV_LORA + LANES]

    ang = pos_ref[...].astype(F32) * invf_ref[...]
    lane = lax.broadcasted_iota(jnp.int32, ang.shape, 1)
    cos_a, sin_a = jnp.cos(ang), jnp.sin(ang)
    half = ROPE_DIM // 2
    cos_t = jnp.where(lane < ROPE_DIM, cos_a, 0.0)
    sin_lo = jnp.where(lane < half, -sin_a, 0.0)
    sin_hi = jnp.where((lane >= half) & (lane < ROPE_DIM), sin_a, 0.0)

    q = _dot(_rms(cq, qan_ref[...]).astype(BF16), wq_ref[...]) * (QK_DIM ** -0.5 * LOG2_E)
    ckvn = _rms(ckv, kvan_ref[...]).astype(BF16)
    k_nope = _dot(ckvn, wk_ref[...])
    v_ref[...] = _dot(ckvn, wv_ref[...]).astype(v_ref.dtype)
    k_rope = _rope(kr, cos_t, sin_lo, sin_hi).astype(kp_ref.dtype)
    for h in range(MLA_HEADS):
        a = h * HEAD_PAD
        qp_ref[:, a:a + NOPE_DIM] = q[:, a:a + NOPE_DIM].astype(qp_ref.dtype)
        qp_ref[:, a + NOPE_DIM:a + HEAD_PAD] = _rope(
            q[:, a + NOPE_DIM:a + HEAD_PAD], cos_t, sin_lo, sin_hi).astype(qp_ref.dtype)
        kp_ref[:, a:a + NOPE_DIM] = k_nope[:, h * NOPE_DIM:(h + 1) * NOPE_DIM].astype(kp_ref.dtype)
        kp_ref[:, a + NOPE_DIM:a + HEAD_PAD] = k_rope


def _level_mid(b, m):
    t = b.shape[0]
    if m >= SUBLANES:
        pieces = [jnp.broadcast_to(b[s + m - 1:s + m, :], (2 * m, LANES)) for s in range(0, t, 2 * m)]
        return pieces[0] if len(pieces) == 1 else jnp.concatenate(pieces, axis=0)
    b3 = b.reshape(t // SUBLANES, SUBLANES, LANES)
    sub = lax.broadcasted_iota(jnp.int32, b3.shape, 1)

    def row(i):
        return jnp.broadcast_to(b3[:, i:i + 1, :], b3.shape)

    out = row(m - 1)
    for s in range(2 * m, SUBLANES, 2 * m):
        out = jnp.where(sub >= s, row(s + m - 1), out)
    return out.reshape(t, LANES)


def _hgrn_kernel(lbl_ref, hq_ref, hf_ref, hi_ref, hg_ref, gn_ref, o_ref, st_ref):
    t = hq_ref.shape[0]

    @pl.when(pl.program_id(1) == 0)
    def _():
        st_ref[...] = jnp.zeros_like(st_ref)

    lg = lbl_ref[...]
    e = jnp.exp(lg - jnp.max(lg, axis=0, keepdims=True))
    lb = e[0:1, :] / jnp.sum(e, axis=0, keepdims=True)

    f_raw = hf_ref[...]
    g = jnp.log(lb + (1.0 - lb) * jax.nn.sigmoid(f_raw))
    kk = (1.0 - lb) * jax.nn.sigmoid(-f_raw)

    row = lax.broadcasted_iota(jnp.int32, (t, t), 0)
    col = lax.broadcasted_iota(jnp.int32, (t, t), 1)
    tri = (row >= col).astype(BF16)
    g1 = g.astype(BF16)
    r1 = g - g1.astype(F32)
    g2 = r1.astype(BF16)
    g3 = (r1 - g2.astype(F32)).astype(BF16)
    b_all = _dot(tri, g1) + _dot(tri, g2) + _dot(tri, g3)

    xm = jnp.where(row > col, row ^ col, 0)

    for h in range(HG_HEADS):
        sl = slice(h * HG_DIM, (h + 1) * HG_DIM)
        q = hq_ref[:, sl]
        k = kk[:, sl]
        v = hi_ref[:, sl]
        b = b_all[:, sl]
        vb = v.astype(BF16)

        scores = jnp.zeros((t, t), F32)
        m = 1
        while m < t:
            decay = jnp.exp(-jnp.abs(b - _level_mid(b, m)))
            s_l = lax.dot_general((q * decay).astype(BF16), (k * decay).astype(BF16), NT_DIMS,
                                  preferred_element_type=F32)
            scores = jnp.where((xm >= m) & (xm < 2 * m), s_l, scores)
            m *= 2

        st = st_ref[h]
        b_last = b[t - 1:t, :]
        o = (_dot(scores.astype(BF16), vb)
             + lax.dot_general((q * jnp.exp(b)).astype(BF16), st.astype(BF16), NT_DIMS,
                               preferred_element_type=F32)
             + jnp.sum(q * k, axis=-1, keepdims=True) * v)
        k_end = (k * jnp.exp(b_last - b)).astype(BF16)
        st_ref[h] = st * jnp.exp(b_last) + lax.dot_general(vb, k_end, TN_DIMS, preferred_element_type=F32)

        gate = hg_ref[:, sl]
        o_ref[:, sl] = (_rms(o, gn_ref[h:h + 1, :]) * (gate * jax.nn.sigmoid(gate))).astype(o_ref.dtype)


def _attn_kernel(q_ref, k_ref, v_ref, o_ref, vaug_ref, *, tq):
    seq = q_ref.shape[0]
    vaug_ref[:, 0:V_DIM] = v_ref[...]
    vaug_ref[:, V_DIM:2 * V_DIM] = jnp.ones((seq, V_DIM), vaug_ref.dtype)
    row = lax.broadcasted_iota(jnp.int32, (tq, tq), 0)
    col = lax.broadcasted_iota(jnp.int32, (tq, tq), 1)
    for i in range(seq // tq):
        lo = i * tq
        q = q_ref[lo:lo + tq, :]
        s_d = lax.dot_general(q, k_ref[lo:lo + tq, :], NT_DIMS, preferred_element_type=F32)
        s_d = jnp.where(row >= col, s_d, -jnp.inf)
        m = jnp.max(s_d, axis=-1, keepdims=True)
        if i > 0:
            s_p = lax.dot_general(q, k_ref[0:lo, :], NT_DIMS, preferred_element_type=F32)
            m = jnp.maximum(m, jnp.max(s_p, axis=-1, keepdims=True))
        acc = _dot(jnp.exp2(s_d - m).astype(BF16), vaug_ref[lo:lo + tq, :])
        if i > 0:
            acc += _dot(jnp.exp2(s_p - m).astype(BF16), vaug_ref[0:lo, :])
        o_ref[lo:lo + tq, :] = (acc[:, 0:V_DIM] / acc[:, V_DIM:2 * V_DIM]).astype(o_ref.dtype)


def _postmix_kernel(h1_ref, ohg_ref, omla_ref, p_ref, wo_hg_ref, wo_mla_ref, g2_ref, wg_ref, wu_ref, wd_ref,
                    gple_ref, wpg_ref, wpp_ref, gfin_ref, y_ref):
    h2 = h1_ref[...] + _dot(ohg_ref[...], wo_hg_ref[...]) + _dot(omla_ref[...], wo_mla_ref[...])
    h3 = h2 + 0.5 * _swiglu(_rms(h2, g2_ref[...]).astype(BF16), wg_ref, wu_ref, wd_ref)
    gate = jax.nn.sigmoid(_dot(_rms(h3, gple_ref[...]).astype(BF16), wpg_ref[...]))
    h4 = h3 + gate * _dot(p_ref[...].astype(BF16), wpp_ref[...])
    y_ref[...] = _rms(h4, gfin_ref[...])


def _resident(shape):
    return pl.BlockSpec(shape, lambda *_: (0,) * len(shape), pipeline_mode=pl.Buffered(1))


def _rows(tile, width):
    return pl.BlockSpec((tile, width), lambda i: (i, 0))


def _ffn_weights(w_gate, w_up, w_down):
    nc = D_FF // FFN_CHUNK
    wg = w_gate.astype(BF16).reshape(D_MODEL, nc, FFN_CHUNK).transpose(1, 0, 2)
    wu = w_up.astype(BF16).reshape(D_MODEL, nc, FFN_CHUNK).transpose(1, 0, 2)
    wd = w_down.astype(BF16).reshape(nc, FFN_CHUNK, D_MODEL)
    return wg, wu, wd


def kernel(x, p, positions, ln_ffn1, w1_gate, w1_up, w1_down, ln_mix, w_in, hg_lb_logits, hg_out_norm,
           q_a_norm, w_q_up, kv_a_norm, w_kv_up, w_out, ln_ffn2, w2_gate, w2_up, w2_down, ln_ple,
           w_ple_gate, w_ple_proj, ln_final):
    bsz, seq, _ = x.shape
    assert p.shape[0] == 1 and hg_lb_logits.shape[0] == 2, "single-layer trunk"
    n = bsz * seq
    tm = min(TOKEN_TILE, n)
    tc = min(HGRN_CHUNK, seq)
    tq = min(ATTN_BLOCK, seq)
    assert n % tm == 0 and seq % tc == 0 and seq % tq == 0
    nc = D_FF // FFN_CHUNK

    x2 = x.reshape(n, D_MODEL)
    pos2 = positions.reshape(n, 1)
    half = ROPE_DIM // 2
    inv_freq = ROPE_THETA ** (-jnp.arange(half, dtype=F32) / half)
    invf = jnp.concatenate([inv_freq, inv_freq, jnp.zeros((LANES - ROPE_DIM,), F32)]).reshape(1, LANES)
    wg1, wu1, wd1 = _ffn_weights(w1_gate[0], w1_up[0], w1_down[0])
    wg2, wu2, wd2 = _ffn_weights(w2_gate[0], w2_up[0], w2_down[0])
    win = jnp.pad(w_in[0].astype(BF16), ((0, 0), (0, IN_PAD - w_in.shape[-1])))
    wq = jnp.pad(w_q_up[0].astype(BF16).reshape(Q_LORA, MLA_HEADS, QK_DIM),
                 ((0, 0), (0, 0), (0, HEAD_PAD - QK_DIM))).reshape(Q_LORA, MLA_HEADS * HEAD_PAD)
    wkv = w_kv_up[0].astype(BF16).reshape(KV_LORA, MLA_HEADS, NOPE_DIM + V_DIM)
    wk = wkv[:, :, :NOPE_DIM].reshape(KV_LORA, MLA_HEADS * NOPE_DIM)
    wv = wkv[:, :, NOPE_DIM:].reshape(KV_LORA, MLA_HEADS * V_DIM)
    wo = w_out[0].astype(BF16)
    row = lambda a: a.reshape(1, -1)

    h1, hq, hf, hi, hg, qp, kp, vv = pl.pallas_call(
        _premix_kernel,
        name="premix",
        grid=(n // tm,),
        in_specs=[
            _rows(tm, D_MODEL), _rows(tm, 1), _resident((1, LANES)), _resident((1, D_MODEL)),
            _resident((nc, D_MODEL, FFN_CHUNK)), _resident((nc, D_MODEL, FFN_CHUNK)),
            _resident((nc, FFN_CHUNK, D_MODEL)), _resident((1, D_MODEL)), _resident((D_MODEL, IN_PAD)),
            _resident((1, Q_LORA)), _resident((Q_LORA, MLA_HEADS * HEAD_PAD)), _resident((1, KV_LORA)),
            _resident((KV_LORA, MLA_HEADS * NOPE_DIM)), _resident((KV_LORA, MLA_HEADS * V_DIM)),
        ],
        out_specs=[
            _rows(tm, D_MODEL), _rows(tm, HG_WIDTH), _rows(tm, HG_WIDTH), _rows(tm, HG_WIDTH),
            _rows(tm, HG_WIDTH), _rows(tm, MLA_HEADS * HEAD_PAD), _rows(tm, MLA_HEADS * HEAD_PAD),
            _rows(tm, MLA_HEADS * V_DIM),
        ],
        out_shape=[
            jax.ShapeDtypeStruct((n, D_MODEL), F32),
            jax.ShapeDtypeStruct((n, HG_WIDTH), F32), jax.ShapeDtypeStruct((n, HG_WIDTH), F32),
            jax.ShapeDtypeStruct((n, HG_WIDTH), F32), jax.ShapeDtypeStruct((n, HG_WIDTH), F32),
            jax.ShapeDtypeStruct((n, MLA_HEADS * HEAD_PAD), BF16),
            jax.ShapeDtypeStruct((n, MLA_HEADS * HEAD_PAD), BF16),
            jax.ShapeDtypeStruct((n, MLA_HEADS * V_DIM), BF16),
        ],
        compiler_params=pltpu.CompilerParams(dimension_semantics=("parallel",), vmem_limit_bytes=VMEM_LIMIT),
    )(x2, pos2, invf, row(ln_ffn1[0]), wg1, wu1, wd1, row(ln_mix[0]), win,
      row(q_a_norm[0]), wq, row(kv_a_norm[0]), wk, wv)

    steps = seq // tc
    chunk_spec = pl.BlockSpec((tc, HG_WIDTH), lambda b, c: (b * steps + c, 0))
    o_hg = pl.pallas_call(
        _hgrn_kernel,
        name="hgrn2",
        grid=(bsz, steps),
        in_specs=[pl.BlockSpec((2, HG_WIDTH), lambda b, c: (0, 0)), chunk_spec, chunk_spec, chunk_spec,
                  chunk_spec, pl.BlockSpec((HG_HEADS, HG_DIM), lambda b, c: (0, 0))],
        out_specs=chunk_spec,
        out_shape=jax.ShapeDtypeStruct((n, HG_WIDTH), BF16),
        scratch_shapes=[pltpu.VMEM((HG_HEADS, HG_DIM, HG_DIM), F32)],
        compiler_params=pltpu.CompilerParams(dimension_semantics=("parallel", "arbitrary"),
                                             vmem_limit_bytes=VMEM_LIMIT),
    )(hg_lb_logits, hq, hf, hi, hg, hg_out_norm[0])

    head_spec = lambda width: pl.BlockSpec((seq, width), lambda b, h: (b, h))
    o_mla = pl.pallas_call(
        functools.partial(_attn_kernel, tq=tq),
        name="mla_attn",
        grid=(bsz, MLA_HEADS),
        in_specs=[head_spec(HEAD_PAD), head_spec(HEAD_PAD), head_spec(V_DIM)],
        out_specs=head_spec(V_DIM),
        out_shape=jax.ShapeDtypeStruct((n, MLA_HEADS * V_DIM), BF16),
        scratch_shapes=[pltpu.VMEM((seq, 2 * V_DIM), BF16)],
        compiler_params=pltpu.CompilerParams(dimension_semantics=("parallel", "parallel"),
                                             vmem_limit_bytes=VMEM_LIMIT),
    )(qp, kp, vv)

    y = pl.pallas_call(
        _postmix_kernel,
        name="postmix",
        grid=(n // tm,),
        in_specs=[
            _rows(tm, D_MODEL), _rows(tm, HG_WIDTH), _rows(tm, MLA_HEADS * V_DIM), _rows(tm, PLE_DIM),
            _resident((HG_WIDTH, D_MODEL)), _resident((MLA_HEADS * V_DIM, D_MODEL)), _resident((1, D_MODEL)),
            _resident((nc, D_MODEL, FFN_CHUNK)), _resident((nc, D_MODEL, FFN_CHUNK)),
            _resident((nc, FFN_CHUNK, D_MODEL)), _resident((1, D_MODEL)), _resident((D_MODEL, D_MODEL)),
            _resident((PLE_DIM, D_MODEL)), _resident((1, D_MODEL)),
        ],
        out_specs=_rows(tm, D_MODEL),
        out_shape=jax.ShapeDtypeStruct((n, D_MODEL), F32),
        compiler_params=pltpu.CompilerParams(dimension_semantics=("parallel",), vmem_limit_bytes=VMEM_LIMIT),
    )(h1, o_hg, o_mla, p[0].reshape(n, PLE_DIM), wo[:HG_WIDTH], wo[HG_WIDTH:], row(ln_ffn2[0]),
      wg2, wu2, wd2, row(ln_ple[0]), w_ple_gate[0].astype(BF16), w_ple_proj[0].astype(BF16), row(ln_final))

    return y.reshape(bsz, seq, D_MODEL)
```

```python
import functools

import jax
import jax.numpy as jnp
from jax import lax
from jax.experimental import pallas as pl
from jax.experimental.pallas import tpu as pltpu

F32 = jnp.float32
BF16 = jnp.bfloat16

D_MODEL = 1024
D_FF = 2816
PLE_DIM = 256
HG_HEADS = 4
HG_DIM = 128
HG_WIDTH = HG_HEADS * HG_DIM
MLA_HEADS = 4
Q_LORA = 256
KV_LORA = 128
NOPE_DIM = 128
ROPE_DIM = 64
V_DIM = 128
QK_DIM = NOPE_DIM + ROPE_DIM
ROPE_THETA = 10000.0
EPS = 1e-6
LOG2_E = 1.4426950408889634

LANES = 128
SUBLANES = 8
HEAD_PAD = 2 * LANES
IN_PAD = 4 * HG_WIDTH + Q_LORA + KV_LORA + LANES

FFN_CHUNK = 256
TOKEN_TILE = 512
HGRN_CHUNK = 256
ATTN_BLOCK = 256
VMEM_LIMIT = 56 * 1024 * 1024

NT_DIMS = (((1,), (1,)), ((), ()))
TN_DIMS = (((0,), (0,)), ((), ()))


def _dot(a, b):
    return jnp.dot(a, b, preferred_element_type=F32)


def _rms(x, g):
    return x * lax.rsqrt(jnp.mean(x * x, axis=-1, keepdims=True) + EPS) * g


def _swiglu(xn, wg_ref, wu_ref, wd_ref):
    acc = None
    for c in range(wg_ref.shape[0]):
        g = _dot(xn, wg_ref[c])
        u = _dot(xn, wu_ref[c])
        a = (g * jax.nn.sigmoid(g) * u).astype(BF16)
        d = _dot(a, wd_ref[c])
        acc = d if acc is None else acc + d
    return acc


def _rope(x, cos_t, sin_lo, sin_hi):
    return (x * cos_t + pltpu.roll(x, LANES - ROPE_DIM // 2, 1) * sin_lo
            + pltpu.roll(x, ROPE_DIM // 2, 1) * sin_hi)


def _premix_kernel(x_ref, pos_ref, invf_ref, g1_ref, wg_ref, wu_ref, wd_ref, gmix_ref, win_ref,
                   qan_ref, wq_ref, kvan_ref, wk_ref, wv_ref,
                   h1_ref, hq_ref, hf_ref, hi_ref, hg_ref, qp_ref, kp_ref, v_ref):
    x = x_ref[...]
    xn = _rms(x, g1_ref[...]).astype(BF16)
    h1 = x + 0.5 * _swiglu(xn, wg_ref, wu_ref, wd_ref)
    h1_ref[...] = h1

    hn = _rms(h1, gmix_ref[...]).astype(BF16)
    w = HG_WIDTH
    hq_ref[...] = _dot(hn, win_ref[:, 0:w]).astype(hq_ref.dtype)
    hf_ref[...] = _dot(hn, win_ref[:, w:2 * w])
    hi_ref[...] = _dot(hn, win_ref[:, 2 * w:3 * w]).astype(hi_ref.dtype)
    hg_ref[...] = _dot(hn, win_ref[:, 3 * w:4 * w]).astype(hg_ref.dtype)
    c0 = 4 * w
    cq = _dot(hn, win_ref[:, c0:c0 + Q_LORA])
    ckv_kr = _dot(hn, win_ref[:, c0 + Q_LORA:IN_PAD])
    ckv = ckv_kr[:, 0:KV_LORA]
    kr = ckv_kr[:, KV_LORA:KV_LORA + LANES]

    ang = pos_ref[...].astype(F32) * invf_ref[...]
    lane = lax.broadcasted_iota(jnp.int32, ang.shape, 1)
    cos_a, sin_a = jnp.cos(ang), jnp.sin(ang)
    half = ROPE_DIM // 2
    cos_t = jnp.where(lane < ROPE_DIM, cos_a, 0.0)
    sin_lo = jnp.where(lane < half, -sin_a, 0.0)
    sin_hi = jnp.where((lane >= half) & (lane < ROPE_DIM), sin_a, 0.0)

    q = _dot(_rms(cq, qan_ref[...]).astype(BF16), wq_ref[...]) * (QK_DIM ** -0.5 * LOG2_E)
    ckvn = _rms(ckv, kvan_ref[...]).astype(BF16)
    k_nope = _dot(ckvn, wk_ref[...])
    v_ref[...] = _dot(ckvn, wv_ref[...]).astype(v_ref.dtype)
    k_rope = _rope(kr, cos_t, sin_lo, sin_hi).astype(kp_ref.dtype)
    for h in range(MLA_HEADS):
        a = h * HEAD_PAD
        qp_ref[:, a:a + NOPE_DIM] = q[:, a:a + NOPE_DIM].astype(qp_ref.dtype)
        qp_ref[:, a + NOPE_DIM:a + HEAD_PAD] = _rope(
            q[:, a + NOPE_DIM:a + HEAD_PAD], cos_t, sin_lo, sin_hi).astype(qp_ref.dtype)
        kp_ref[:, a:a + NOPE_DIM] = k_nope[:, h * NOPE_DIM:(h + 1) * NOPE_DIM].astype(kp_ref.dtype)
        kp_ref[:, a + NOPE_DIM:a + HEAD_PAD] = k_rope


def _neg_gap(b, m):
    t = b.shape[0]
    if m >= SUBLANES:
        pieces = []
        for s in range(0, t, 2 * m):
            mid = b[s + m - 1:s + m, :]
            pieces += [mid - b[s:s + m], b[s + m:s + 2 * m] - mid]
        return jnp.concatenate(pieces, axis=0)
    b3 = b.reshape(t // SUBLANES, SUBLANES, LANES)
    sub = lax.broadcasted_iota(jnp.int32, b3.shape, 1)

    def row(i):
        return jnp.broadcast_to(b3[:, i:i + 1, :], b3.shape)

    mid = row(m - 1)
    for s in range(2 * m, SUBLANES, 2 * m):
        mid = jnp.where(sub >= s, row(s + m - 1), mid)
    return -jnp.abs(b3 - mid).reshape(t, LANES)


def _hgrn_kernel(lbl_ref, hq_ref, hf_ref, hi_ref, hg_ref, gn_ref, o_ref, st_ref):
    t = hq_ref.shape[0]

    @pl.when(pl.program_id(1) == 0)
    def _():
        st_ref[...] = jnp.zeros_like(st_ref)

    lg = lbl_ref[...]
    e = jnp.exp(lg - jnp.max(lg, axis=0, keepdims=True))
    lb = e[0:1, :] / jnp.sum(e, axis=0, keepdims=True)

    f_raw = hf_ref[...]
    f = lb + (1.0 - lb) * jax.nn.sigmoid(f_raw)
    g = jnp.log(f)
    kk = (1.0 - lb) * jax.nn.sigmoid(-f_raw)

    row = lax.broadcasted_iota(jnp.int32, (t, t), 0)
    col = lax.broadcasted_iota(jnp.int32, (t, t), 1)
    tri = (row >= col).astype(BF16)
    g1 = g.astype(BF16)
    r1 = g - g1.astype(F32)
    g2 = r1.astype(BF16)
    g3 = (r1 - g2.astype(F32)).astype(BF16)
    b_all = (_dot(tri, g1) + _dot(tri, g2) + _dot(tri, g3)) * LOG2_E

    half = t // 2
    ri = lax.broadcasted_iota(jnp.int32, (half, half), 0)
    ci = lax.broadcasted_iota(jnp.int32, (half, half), 1)
    xm = jnp.where(ri > ci, ri ^ ci, 0)
    levels = [1 << i for i in range(half.bit_length() - 1)]
    masks = [ri == ci] + [(xm >= m) & (xm < 2 * m) for m in levels]
    odd_row = (lax.broadcasted_iota(jnp.int32, (t, HG_DIM), 0) & 1) == 1

    for h in range(HG_HEADS):
        sl = slice(h * HG_DIM, (h + 1) * HG_DIM)
        q = hq_ref[:, sl]
        v = hi_ref[:, sl]
        kb = kk[:, sl].astype(BF16)
        b = b_all[:, sl]

        qk = [(q, kb)]
        for m in levels:
            if m == 1:
                decay = jnp.where(odd_row, f[:, sl], 1.0).astype(BF16)
            else:
                decay = jnp.exp2(_neg_gap(b, m)).astype(BF16)
            qk.append((q * decay, kb * decay))
        diag = []
        for r0 in (0, half):
            p = None
            for (q_l, k_l), mask in zip(qk, masks):
                s_l = lax.dot_general(q_l[r0:r0 + half], k_l[r0:r0 + half], NT_DIMS, preferred_element_type=F32)
                p = jnp.where(mask, s_l, 0.0 if p is None else p)
            diag.append(p.astype(BF16))
        mid = b[half - 1:half, :]
        cross = lax.dot_general(q[half:] * jnp.exp2(b[half:] - mid).astype(BF16),
                                kb[:half] * jnp.exp2(mid - b[:half]).astype(BF16), NT_DIMS,
                                preferred_element_type=F32).astype(BF16)
        o = jnp.concatenate([_dot(diag[0], v[:half]),
                             _dot(jnp.concatenate([cross, diag[1]], axis=1), v)], axis=0)

        st = st_ref[h]
        b_last = b[t - 1:t, :]
        o = o + lax.dot_general(q * jnp.exp2(b).astype(BF16), st.astype(BF16), NT_DIMS,
                                preferred_element_type=F32)
        k_end = kb * jnp.exp2(b_last - b).astype(BF16)
        st_ref[h] = st * jnp.exp2(b_last) + lax.dot_general(v, k_end, TN_DIMS, preferred_element_type=F32)

        gate = hg_ref[:, sl].astype(F32)
        o_ref[:, sl] = (_rms(o, gn_ref[h:h + 1, :]) * (gate * jax.nn.sigmoid(gate))).astype(o_ref.dtype)


def _attn_kernel(q_ref, k_ref, v_ref, o_ref, vaug_ref, *, tq):
    seq = q_ref.shape[0]
    vaug_ref[:, 0:V_DIM] = v_ref[...]
    vaug_ref[:, V_DIM:2 * V_DIM] = jnp.ones((seq, V_DIM), vaug_ref.dtype)
    row = lax.broadcasted_iota(jnp.int32, (tq, tq), 0)
    col = lax.broadcasted_iota(jnp.int32, (tq, tq), 1)
    for i in range(seq // tq):
        lo = i * tq
        q = q_ref[lo:lo + tq, :]
        s_d = lax.dot_general(q, k_ref[lo:lo + tq, :], NT_DIMS, preferred_element_type=F32)
        s_d = jnp.where(row >= col, s_d, -jnp.inf)
        m = jnp.max(s_d, axis=-1, keepdims=True)
        if i > 0:
            s_p = lax.dot_general(q, k_ref[0:lo, :], NT_DIMS, preferred_element_type=F32)
            m = jnp.maximum(m, jnp.max(s_p, axis=-1, keepdims=True))
        acc = _dot(jnp.exp2(s_d - m).astype(BF16), vaug_ref[lo:lo + tq, :])
        if i > 0:
            acc += _dot(jnp.exp2(s_p - m).astype(BF16), vaug_ref[0:lo, :])
        o_ref[lo:lo + tq, :] = (acc[:, 0:V_DIM] / acc[:, V_DIM:2 * V_DIM]).astype(o_ref.dtype)


def _postmix_kernel(h1_ref, ohg_ref, omla_ref, p_ref, wo_hg_ref, wo_mla_ref, g2_ref, wg_ref, wu_ref, wd_ref,
                    gple_ref, wpg_ref, wpp_ref, gfin_ref, y_ref):
    h2 = h1_ref[...] + _dot(ohg_ref[...], wo_hg_ref[...]) + _dot(omla_ref[...], wo_mla_ref[...])
    h3 = h2 + 0.5 * _swiglu(_rms(h2, g2_ref[...]).astype(BF16), wg_ref, wu_ref, wd_ref)
    gate = jax.nn.sigmoid(_dot(_rms(h3, gple_ref[...]).astype(BF16), wpg_ref[...]))
    h4 = h3 + gate * _dot(p_ref[...].astype(BF16), wpp_ref[...])
    y_ref[...] = _rms(h4, gfin_ref[...])


def _resident(shape):
    return pl.BlockSpec(shape, lambda *_: (0,) * len(shape), pipeline_mode=pl.Buffered(1))


def _rows(tile, width):
    return pl.BlockSpec((tile, width), lambda i: (i, 0))


def _ffn_weights(w_gate, w_up, w_down):
    nc = D_FF // FFN_CHUNK
    wg = w_gate.astype(BF16).reshape(D_MODEL, nc, FFN_CHUNK).transpose(1, 0, 2)
    wu = w_up.astype(BF16).reshape(D_MODEL, nc, FFN_CHUNK).transpose(1, 0, 2)
    wd = w_down.astype(BF16).reshape(nc, FFN_CHUNK, D_MODEL)
    return wg, wu, wd


def kernel(x, p, positions, ln_ffn1, w1_gate, w1_up, w1_down, ln_mix, w_in, hg_lb_logits, hg_out_norm,
           q_a_norm, w_q_up, kv_a_norm, w_kv_up, w_out, ln_ffn2, w2_gate, w2_up, w2_down, ln_ple,
           w_ple_gate, w_ple_proj, ln_final):
    bsz, seq, _ = x.shape
    assert p.shape[0] == 1 and hg_lb_logits.shape[0] == 2, "single-layer trunk"
    n = bsz * seq
    tm = min(TOKEN_TILE, n)
    tc = min(HGRN_CHUNK, seq)
    tq = min(ATTN_BLOCK, seq)
    assert n % tm == 0 and seq % tc == 0 and seq % tq == 0
    nc = D_FF // FFN_CHUNK

    x2 = x.reshape(n, D_MODEL)
    pos2 = positions.reshape(n, 1)
    half = ROPE_DIM // 2
    inv_freq = ROPE_THETA ** (-jnp.arange(half, dtype=F32) / half)
    invf = jnp.concatenate([inv_freq, inv_freq, jnp.zeros((LANES - ROPE_DIM,), F32)]).reshape(1, LANES)
    wg1, wu1, wd1 = _ffn_weights(w1_gate[0], w1_up[0], w1_down[0])
    wg2, wu2, wd2 = _ffn_weights(w2_gate[0], w2_up[0], w2_down[0])
    win = jnp.pad(w_in[0].astype(BF16), ((0, 0), (0, IN_PAD - w_in.shape[-1])))
    wq = jnp.pad(w_q_up[0].astype(BF16).reshape(Q_LORA, MLA_HEADS, QK_DIM),
                 ((0, 0), (0, 0), (0, HEAD_PAD - QK_DIM))).reshape(Q_LORA, MLA_HEADS * HEAD_PAD)
    wkv = w_kv_up[0].astype(BF16).reshape(KV_LORA, MLA_HEADS, NOPE_DIM + V_DIM)
    wk = wkv[:, :, :NOPE_DIM].reshape(KV_LORA, MLA_HEADS * NOPE_DIM)
    wv = wkv[:, :, NOPE_DIM:].reshape(KV_LORA, MLA_HEADS * V_DIM)
    wo = w_out[0].astype(BF16)
    row = lambda a: a.reshape(1, -1)

    h1, hq, hf, hi, hg, qp, kp, vv = pl.pallas_call(
        _premix_kernel,
        name="premix",
        grid=(n // tm,),
        in_specs=[
            _rows(tm, D_MODEL), _rows(tm, 1), _resident((1, LANES)), _resident((1, D_MODEL)),
            _resident((nc, D_MODEL, FFN_CHUNK)), _resident((nc, D_MODEL, FFN_CHUNK)),
            _resident((nc, FFN_CHUNK, D_MODEL)), _resident((1, D_MODEL)), _resident((D_MODEL, IN_PAD)),
            _resident((1, Q_LORA)), _resident((Q_LORA, MLA_HEADS * HEAD_PAD)), _resident((1, KV_LORA)),
            _resident((KV_LORA, MLA_HEADS * NOPE_DIM)), _resident((KV_LORA, MLA_HEADS * V_DIM)),
        ],
        out_specs=[
            _rows(tm, D_MODEL), _rows(tm, HG_WIDTH), _rows(tm, HG_WIDTH), _rows(tm, HG_WIDTH),
            _rows(tm, HG_WIDTH), _rows(tm, MLA_HEADS * HEAD_PAD), _rows(tm, MLA_HEADS * HEAD_PAD),
            _rows(tm, MLA_HEADS * V_DIM),
        ],
        out_shape=[
            jax.ShapeDtypeStruct((n, D_MODEL), F32),
            jax.ShapeDtypeStruct((n, HG_WIDTH), BF16), jax.ShapeDtypeStruct((n, HG_WIDTH), F32),
            jax.ShapeDtypeStruct((n, HG_WIDTH), BF16), jax.ShapeDtypeStruct((n, HG_WIDTH), BF16),
            jax.ShapeDtypeStruct((n, MLA_HEADS * HEAD_PAD), BF16),
            jax.ShapeDtypeStruct((n, MLA_HEADS * HEAD_PAD), BF16),
            jax.ShapeDtypeStruct((n, MLA_HEADS * V_DIM), BF16),
        ],
        compiler_params=pltpu.CompilerParams(dimension_semantics=("parallel",), vmem_limit_bytes=VMEM_LIMIT),
    )(x2, pos2, invf, row(ln_ffn1[0]), wg1, wu1, wd1, row(ln_mix[0]), win,
      row(q_a_norm[0]), wq, row(kv_a_norm[0]), wk, wv)

    steps = seq // tc
    chunk_spec = pl.BlockSpec((tc, HG_WIDTH), lambda b, c: (b * steps + c, 0))
    o_hg = pl.pallas_call(
        _hgrn_kernel,
        name="hgrn2",
        grid=(bsz, steps),
        in_specs=[pl.BlockSpec((2, HG_WIDTH), lambda b, c: (0, 0)), chunk_spec, chunk_spec, chunk_spec,
                  chunk_spec, pl.BlockSpec((HG_HEADS, HG_DIM), lambda b, c: (0, 0))],
        out_specs=chunk_spec,
        out_shape=jax.ShapeDtypeStruct((n, HG_WIDTH), BF16),
        scratch_shapes=[pltpu.VMEM((HG_HEADS, HG_DIM, HG_DIM), F32)],
        compiler_params=pltpu.CompilerParams(dimension_semantics=("parallel", "arbitrary"),
                                             vmem_limit_bytes=VMEM_LIMIT),
    )(hg_lb_logits, hq, hf, hi, hg, hg_out_norm[0])

    head_spec = lambda width: pl.BlockSpec((seq, width), lambda b, h: (b, h))
    o_mla = pl.pallas_call(
        functools.partial(_attn_kernel, tq=tq),
        name="mla_attn",
        grid=(bsz, MLA_HEADS),
        in_specs=[head_spec(HEAD_PAD), head_spec(HEAD_PAD), head_spec(V_DIM)],
        out_specs=head_spec(V_DIM),
        out_shape=jax.ShapeDtypeStruct((n, MLA_HEADS * V_DIM), BF16),
        scratch_shapes=[pltpu.VMEM((seq, 2 * V_DIM), BF16)],
        compiler_params=pltpu.CompilerParams(dimension_semantics=("parallel", "parallel"),
                                             vmem_limit_bytes=VMEM_LIMIT),
    )(qp, kp, vv)

    y = pl.pallas_call(
        _postmix_kernel,
        name="postmix",
        grid=(n // tm,),
        in_specs=[
            _rows(tm, D_MODEL), _rows(tm, HG_WIDTH), _rows(tm, MLA_HEADS * V_DIM), _rows(tm, PLE_DIM),
            _resident((HG_WIDTH, D_MODEL)), _resident((MLA_HEADS * V_DIM, D_MODEL)), _resident((1, D_MODEL)),
            _resident((nc, D_MODEL, FFN_CHUNK)), _resident((nc, D_MODEL, FFN_CHUNK)),
            _resident((nc, FFN_CHUNK, D_MODEL)), _resident((1, D_MODEL)), _resident((D_MODEL, D_MODEL)),
            _resident((PLE_DIM, D_MODEL)), _resident((1, D_MODEL)),
        ],
        out_specs=_rows(tm, D_MODEL),
        out_shape=jax.ShapeDtypeStruct((n, D_MODEL), F32),
        compiler_params=pltpu.CompilerParams(dimension_semantics=("parallel",), vmem_limit_bytes=VMEM_LIMIT),
    )(h1, o_hg, o_mla, p[0].reshape(n, PLE_DIM), wo[:HG_WIDTH], wo[HG_WIDTH:], row(ln_ffn2[0]),
      wg2, wu2, wd2, row(ln_ple[0]), w_ple_gate[0].astype(BF16), w_ple_proj[0].astype(BF16), row(ln_final))

    return y.reshape(bsz, seq, D_MODEL)
```

```python
import functools

import jax
import jax.numpy as jnp
from jax import lax
from jax.experimental import pallas as pl
from jax.experimental.pallas import tpu as pltpu

F32 = jnp.float32
BF16 = jnp.bfloat16

D_MODEL = 1024
D_FF = 2816
PLE_DIM = 256
HG_HEADS = 4
HG_DIM = 128
HG_WIDTH = HG_HEADS * HG_DIM
MLA_HEADS = 4
Q_LORA = 256
KV_LORA = 128
NOPE_DIM = 128
ROPE_DIM = 64
V_DIM = 128
QK_DIM = NOPE_DIM + ROPE_DIM
ROPE_THETA = 10000.0
EPS = 1e-6
LOG2_E = 1.4426950408889634

LANES = 128
SUBLANES = 8
HEAD_PAD = 2 * LANES
IN_PAD = 4 * HG_WIDTH + Q_LORA + KV_LORA + LANES

FFN_CHUNK = 256
TOKEN_TILE = 512
HGRN_CHUNK = 256
ATTN_BLOCK = 256
VMEM_LIMIT = 56 * 1024 * 1024

NT_DIMS = (((1,), (1,)), ((), ()))
TN_DIMS = (((0,), (0,)), ((), ()))


def _dot(a, b):
    return jnp.dot(a, b, preferred_element_type=F32)


def _rms(x, g):
    return x * lax.rsqrt(jnp.mean(x * x, axis=-1, keepdims=True) + EPS) * g


def _swiglu(xn, wg_ref, wu_ref, wd_ref):
    acc = None
    for c in range(0, D_FF, FFN_CHUNK):
        g = _dot(xn, wg_ref[:, c:c + FFN_CHUNK])
        u = _dot(xn, wu_ref[:, c:c + FFN_CHUNK])
        a = (g * jax.nn.sigmoid(g) * u).astype(BF16)
        d = _dot(a, wd_ref[c:c + FFN_CHUNK, :])
        acc = d if acc is None else acc + d
    return acc


def _rope(x, cos_t, sin_lo, sin_hi):
    return (x * cos_t + pltpu.roll(x, LANES - ROPE_DIM // 2, 1) * sin_lo
            + pltpu.roll(x, ROPE_DIM // 2, 1) * sin_hi)


def _premix_kernel(x_ref, pos_ref, invf_ref, g1_ref, wg_ref, wu_ref, wd_ref, gmix_ref, win_ref,
                   qan_ref, wq_ref, kvan_ref, wk_ref, wv_ref,
                   h1_ref, hq_ref, hf_ref, hi_ref, hg_ref, qp_ref, kp_ref, v_ref):
    x = x_ref[...]
    xn = _rms(x, g1_ref[...]).astype(BF16)
    h1 = x + 0.5 * _swiglu(xn, wg_ref, wu_ref, wd_ref)
    h1_ref[...] = h1

    hn = _rms(h1, gmix_ref[...]).astype(BF16)
    w = HG_WIDTH
    hq_ref[...] = _dot(hn, win_ref[:, 0:w]).astype(hq_ref.dtype)
    hf_ref[...] = _dot(hn, win_ref[:, w:2 * w])
    hi_ref[...] = _dot(hn, win_ref[:, 2 * w:3 * w]).astype(hi_ref.dtype)
    hg_ref[...] = _dot(hn, win_ref[:, 3 * w:4 * w]).astype(hg_ref.dtype)
    c0 = 4 * w
    cq = _dot(hn, win_ref[:, c0:c0 + Q_LORA])
    ckv_kr = _dot(hn, win_ref[:, c0 + Q_LORA:IN_PAD])
    ckv = ckv_kr[:, 0:KV_LORA]
    kr = ckv_kr[:, KV_LORA:KV_LORA + LANES]

    ang = pos_ref[...].astype(F32) * invf_ref[...]
    lane = lax.broadcasted_iota(jnp.int32, ang.shape, 1)
    cos_a, sin_a = jnp.cos(ang), jnp.sin(ang)
    half = ROPE_DIM // 2
    cos_t = jnp.where(lane < ROPE_DIM, cos_a, 0.0)
    sin_lo = jnp.where(lane < half, -sin_a, 0.0)
    sin_hi = jnp.where((lane >= half) & (lane < ROPE_DIM), sin_a, 0.0)

    q = _dot(_rms(cq, qan_ref[...]).astype(BF16), wq_ref[...]) * (QK_DIM ** -0.5 * LOG2_E)
    ckvn = _rms(ckv, kvan_ref[...]).astype(BF16)
    k_nope = _dot(ckvn, wk_ref[...])
    v_ref[...] = _dot(ckvn, wv_ref[...]).astype(v_ref.dtype)
    k_rope = _rope(kr, cos_t, sin_lo, sin_hi).astype(kp_ref.dtype)
    for h in range(MLA_HEADS):
        a = h * HEAD_PAD
        qp_ref[:, a:a + NOPE_DIM] = q[:, a:a + NOPE_DIM].astype(qp_ref.dtype)
        qp_ref[:, a + NOPE_DIM:a + HEAD_PAD] = _rope(
            q[:, a + NOPE_DIM:a + HEAD_PAD], cos_t, sin_lo, sin_hi).astype(qp_ref.dtype)
        kp_ref[:, a:a + NOPE_DIM] = k_nope[:, h * NOPE_DIM:(h + 1) * NOPE_DIM].astype(kp_ref.dtype)
        kp_ref[:, a + NOPE_DIM:a + HEAD_PAD] = k_rope


def _neg_gap(b, m):
    t = b.shape[0]
    if m >= SUBLANES:
        pieces = []
        for s in range(0, t, 2 * m):
            mid = b[s + m - 1:s + m, :]
            pieces += [mid - b[s:s + m], b[s + m:s + 2 * m] - mid]
        return jnp.concatenate(pieces, axis=0)
    b3 = b.reshape(t // SUBLANES, SUBLANES, LANES)
    sub = lax.broadcasted_iota(jnp.int32, b3.shape, 1)

    def row(i):
        return jnp.broadcast_to(b3[:, i:i + 1, :], b3.shape)

    mid = row(m - 1)
    for s in range(2 * m, SUBLANES, 2 * m):
        mid = jnp.where(sub >= s, row(s + m - 1), mid)
    return -jnp.abs(b3 - mid).reshape(t, LANES)


def _hgrn_kernel(lbl_ref, hq_ref, hf_ref, hi_ref, hg_ref, gn_ref, o_ref, st_ref):
    t = hq_ref.shape[0]

    @pl.when(pl.program_id(1) == 0)
    def _():
        st_ref[...] = jnp.zeros_like(st_ref)

    lg = lbl_ref[...]
    e = jnp.exp(lg - jnp.max(lg, axis=0, keepdims=True))
    lb = e[0:1, :] / jnp.sum(e, axis=0, keepdims=True)

    f_raw = hf_ref[...]
    f = lb + (1.0 - lb) * jax.nn.sigmoid(f_raw)
    g = jnp.log(f)
    kk = (1.0 - lb) * jax.nn.sigmoid(-f_raw)

    row = lax.broadcasted_iota(jnp.int32, (t, t), 0)
    col = lax.broadcasted_iota(jnp.int32, (t, t), 1)
    tri = (row >= col).astype(BF16)
    g1 = g.astype(BF16)
    r1 = g - g1.astype(F32)
    g2 = r1.astype(BF16)
    g3 = (r1 - g2.astype(F32)).astype(BF16)
    b_all = (_dot(tri, g1) + _dot(tri, g2) + _dot(tri, g3)) * LOG2_E

    half = t // 2
    ri = lax.broadcasted_iota(jnp.int32, (half, half), 0)
    ci = lax.broadcasted_iota(jnp.int32, (half, half), 1)
    xm = jnp.where(ri > ci, ri ^ ci, 0)
    levels = [1 << i for i in range(half.bit_length() - 1)]
    masks = [ri == ci] + [(xm >= m) & (xm < 2 * m) for m in levels]
    odd_row = (lax.broadcasted_iota(jnp.int32, (t, HG_DIM), 0) & 1) == 1

    for h in range(HG_HEADS):
        sl = slice(h * HG_DIM, (h + 1) * HG_DIM)
        q = hq_ref[:, sl]
        v = hi_ref[:, sl]
        kb = kk[:, sl].astype(BF16)
        b = b_all[:, sl]

        qk = [(q, kb)]
        for m in levels:
            if m == 1:
                decay = jnp.where(odd_row, f[:, sl], 1.0).astype(BF16)
            else:
                decay = jnp.exp2(_neg_gap(b, m)).astype(BF16)
            qk.append((q * decay, kb * decay))
        diag = []
        for r0 in (0, half):
            p = None
            for (q_l, k_l), mask in zip(qk, masks):
                s_l = lax.dot_general(q_l[r0:r0 + half], k_l[r0:r0 + half], NT_DIMS, preferred_element_type=F32)
                p = jnp.where(mask, s_l, 0.0 if p is None else p)
            diag.append(p.astype(BF16))
        mid = b[half - 1:half, :]
        cross = lax.dot_general(q[half:] * jnp.exp2(b[half:] - mid).astype(BF16),
                                kb[:half] * jnp.exp2(mid - b[:half]).astype(BF16), NT_DIMS,
                                preferred_element_type=F32).astype(BF16)
        o = jnp.concatenate([_dot(diag[0], v[:half]),
                             _dot(jnp.concatenate([cross, diag[1]], axis=1), v)], axis=0)

        st = st_ref[h]
        b_last = b[t - 1:t, :]
        o = o + lax.dot_general(q * jnp.exp2(b).astype(BF16), st.astype(BF16), NT_DIMS,
                                preferred_element_type=F32)
        k_end = kb * jnp.exp2(b_last - b).astype(BF16)
        st_ref[h] = st * jnp.exp2(b_last) + lax.dot_general(v, k_end, TN_DIMS, preferred_element_type=F32)

        gate = hg_ref[:, sl].astype(F32)
        o_ref[:, sl] = (_rms(o, gn_ref[h:h + 1, :]) * (gate * jax.nn.sigmoid(gate))).astype(o_ref.dtype)


def _attn_kernel(q_ref, k_ref, v_ref, o_ref, vaug_ref, *, tq):
    seq = q_ref.shape[0]
    vaug_ref[:, 0:V_DIM] = v_ref[...]
    vaug_ref[:, V_DIM:2 * V_DIM] = jnp.ones((seq, V_DIM), vaug_ref.dtype)
    row = lax.broadcasted_iota(jnp.int32, (tq, tq), 0)
    col = lax.broadcasted_iota(jnp.int32, (tq, tq), 1)
    nq = seq // tq

    def scores(i):
        lo = i * tq
        q = q_ref[lo:lo + tq, :]
        s_d = lax.dot_general(q, k_ref[lo:lo + tq, :], NT_DIMS, preferred_element_type=F32)
        s_d = jnp.where(row >= col, s_d, -jnp.inf)
        s_p = lax.dot_general(q, k_ref[0:lo, :], NT_DIMS, preferred_element_type=F32) if i > 0 else None
        return s_d, s_p

    nxt = scores(0)
    for i in range(nq):
        lo = i * tq
        s_d, s_p = nxt
        if i + 1 < nq:
            nxt = scores(i + 1)
        m = jnp.max(s_d, axis=-1, keepdims=True)
        if i > 0:
            m = jnp.maximum(m, jnp.max(s_p, axis=-1, keepdims=True))
        acc = _dot(jnp.exp2(s_d - m).astype(BF16), vaug_ref[lo:lo + tq, :])
        if i > 0:
            acc += _dot(jnp.exp2(s_p - m).astype(BF16), vaug_ref[0:lo, :])
        o_ref[lo:lo + tq, :] = (acc[:, 0:V_DIM] / acc[:, V_DIM:2 * V_DIM]).astype(o_ref.dtype)


def _postmix_kernel(h1_ref, ohg_ref, omla_ref, p_ref, wo_hg_ref, wo_mla_ref, g2_ref, wg_ref, wu_ref, wd_ref,
                    gple_ref, wpg_ref, wpp_ref, gfin_ref, y_ref):
    h2 = h1_ref[...] + _dot(ohg_ref[...], wo_hg_ref[...]) + _dot(omla_ref[...], wo_mla_ref[...])
    h3 = h2 + 0.5 * _swiglu(_rms(h2, g2_ref[...]).astype(BF16), wg_ref, wu_ref, wd_ref)
    gate = jax.nn.sigmoid(_dot(_rms(h3, gple_ref[...]).astype(BF16), wpg_ref[...]))
    h4 = h3 + gate * _dot(p_ref[...].astype(BF16), wpp_ref[...])
    y_ref[...] = _rms(h4, gfin_ref[...])


def _resident(shape):
    return pl.BlockSpec(shape, lambda *_: (0,) * len(shape), pipeline_mode=pl.Buffered(1))


def _rows(tile, width):
    return pl.BlockSpec((tile, width), lambda i: (i, 0))


def kernel(x, p, positions, ln_ffn1, w1_gate, w1_up, w1_down, ln_mix, w_in, hg_lb_logits, hg_out_norm,
           q_a_norm, w_q_up, kv_a_norm, w_kv_up, w_out, ln_ffn2, w2_gate, w2_up, w2_down, ln_ple,
           w_ple_gate, w_ple_proj, ln_final):
    bsz, seq, _ = x.shape
    assert p.shape[0] == 1 and hg_lb_logits.shape[0] == 2, "single-layer trunk"
    n = bsz * seq
    tm = min(TOKEN_TILE, n)
    tc = min(HGRN_CHUNK, seq)
    tq = min(ATTN_BLOCK, seq)
    assert n % tm == 0 and seq % tc == 0 and seq % tq == 0

    x2 = x.reshape(n, D_MODEL)
    pos2 = positions.reshape(n, 1)
    half = ROPE_DIM // 2
    inv_freq = ROPE_THETA ** (-jnp.arange(half, dtype=F32) / half)
    invf = jnp.concatenate([inv_freq, inv_freq, jnp.zeros((LANES - ROPE_DIM,), F32)]).reshape(1, LANES)
    wg1, wu1, wd1 = w1_gate[0].astype(BF16), w1_up[0].astype(BF16), w1_down[0].astype(BF16)
    wg2, wu2, wd2 = w2_gate[0].astype(BF16), w2_up[0].astype(BF16), w2_down[0].astype(BF16)
    win = jnp.pad(w_in[0].astype(BF16), ((0, 0), (0, IN_PAD - w_in.shape[-1])))
    wq = jnp.pad(w_q_up[0].astype(BF16).reshape(Q_LORA, MLA_HEADS, QK_DIM),
                 ((0, 0), (0, 0), (0, HEAD_PAD - QK_DIM))).reshape(Q_LORA, MLA_HEADS * HEAD_PAD)
    wkv = w_kv_up[0].astype(BF16).reshape(KV_LORA, MLA_HEADS, NOPE_DIM + V_DIM)
    wk = wkv[:, :, :NOPE_DIM].reshape(KV_LORA, MLA_HEADS * NOPE_DIM)
    wv = wkv[:, :, NOPE_DIM:].reshape(KV_LORA, MLA_HEADS * V_DIM)
    wo = w_out[0].astype(BF16)
    row = lambda a: a.reshape(1, -1)

    h1, hq, hf, hi, hg, qp, kp, vv = pl.pallas_call(
        _premix_kernel,
        name="premix",
        grid=(n // tm,),
        in_specs=[
            _rows(tm, D_MODEL), _rows(tm, 1), _resident((1, LANES)), _resident((1, D_MODEL)),
            _resident((D_MODEL, D_FF)), _resident((D_MODEL, D_FF)), _resident((D_FF, D_MODEL)), _resident((1, D_MODEL)), _resident((D_MODEL, IN_PAD)),
            _resident((1, Q_LORA)), _resident((Q_LORA, MLA_HEADS * HEAD_PAD)), _resident((1, KV_LORA)),
            _resident((KV_LORA, MLA_HEADS * NOPE_DIM)), _resident((KV_LORA, MLA_HEADS * V_DIM)),
        ],
        out_specs=[
            _rows(tm, D_MODEL), _rows(tm, HG_WIDTH), _rows(tm, HG_WIDTH), _rows(tm, HG_WIDTH),
            _rows(tm, HG_WIDTH), _rows(tm, MLA_HEADS * HEAD_PAD), _rows(tm, MLA_HEADS * HEAD_PAD),
            _rows(tm, MLA_HEADS * V_DIM),
        ],
        out_shape=[
            jax.ShapeDtypeStruct((n, D_MODEL), F32),
            jax.ShapeDtypeStruct((n, HG_WIDTH), BF16), jax.ShapeDtypeStruct((n, HG_WIDTH), F32),
            jax.ShapeDtypeStruct((n, HG_WIDTH), BF16), jax.ShapeDtypeStruct((n, HG_WIDTH), BF16),
            jax.ShapeDtypeStruct((n, MLA_HEADS * HEAD_PAD), BF16),
            jax.ShapeDtypeStruct((n, MLA_HEADS * HEAD_PAD), BF16),
            jax.ShapeDtypeStruct((n, MLA_HEADS * V_DIM), BF16),
        ],
        compiler_params=pltpu.CompilerParams(dimension_semantics=("parallel",), vmem_limit_bytes=VMEM_LIMIT),
    )(x2, pos2, invf, row(ln_ffn1[0]), wg1, wu1, wd1, row(ln_mix[0]), win,
      row(q_a_norm[0]), wq, row(kv_a_norm[0]), wk, wv)

    steps = seq // tc
    chunk_spec = pl.BlockSpec((tc, HG_WIDTH), lambda b, c: (b * steps + c, 0))
    o_hg = pl.pallas_call(
        _hgrn_kernel,
        name="hgrn2",
        grid=(bsz, steps),
        in_specs=[pl.BlockSpec((2, HG_WIDTH), lambda b, c: (0, 0)), chunk_spec, chunk_spec, chunk_spec,
                  chunk_spec, pl.BlockSpec((HG_HEADS, HG_DIM), lambda b, c: (0, 0))],
        out_specs=chunk_spec,
        out_shape=jax.ShapeDtypeStruct((n, HG_WIDTH), BF16),
        scratch_shapes=[pltpu.VMEM((HG_HEADS, HG_DIM, HG_DIM), F32)],
        compiler_params=pltpu.CompilerParams(dimension_semantics=("parallel", "arbitrary"),
                                             vmem_limit_bytes=VMEM_LIMIT),
    )(hg_lb_logits, hq, hf, hi, hg, hg_out_norm[0])

    head_spec = lambda width: pl.BlockSpec((seq, width), lambda b, h: (b, h))
    o_mla = pl.pallas_call(
        functools.partial(_attn_kernel, tq=tq),
        name="mla_attn",
        grid=(bsz, MLA_HEADS),
        in_specs=[head_spec(HEAD_PAD), head_spec(HEAD_PAD), head_spec(V_DIM)],
        out_specs=head_spec(V_DIM),
        out_shape=jax.ShapeDtypeStruct((n, MLA_HEADS * V_DIM), BF16),
        scratch_shapes=[pltpu.VMEM((seq, 2 * V_DIM), BF16)],
        compiler_params=pltpu.CompilerParams(dimension_semantics=("parallel", "parallel"),
                                             vmem_limit_bytes=VMEM_LIMIT),
    )(qp, kp, vv)

    y = pl.pallas_call(
        _postmix_kernel,
        name="postmix",
        grid=(n // tm,),
        in_specs=[
            _rows(tm, D_MODEL), _rows(tm, HG_WIDTH), _rows(tm, MLA_HEADS * V_DIM), _rows(tm, PLE_DIM),
            _resident((HG_WIDTH, D_MODEL)), _resident((MLA_HEADS * V_DIM, D_MODEL)), _resident((1, D_MODEL)),
            _resident((D_MODEL, D_FF)), _resident((D_MODEL, D_FF)), _resident((D_FF, D_MODEL)), _resident((1, D_MODEL)), _resident((D_MODEL, D_MODEL)),
            _resident((PLE_DIM, D_MODEL)), _resident((1, D_MODEL)),
        ],
        out_specs=_rows(tm, D_MODEL),
        out_shape=jax.ShapeDtypeStruct((n, D_MODEL), F32),
        compiler_params=pltpu.CompilerParams(dimension_semantics=("parallel",), vmem_limit_bytes=VMEM_LIMIT),
    )(h1, o_hg, o_mla, p[0].reshape(n, PLE_DIM), wo[:HG_WIDTH], wo[HG_WIDTH:], row(ln_ffn2[0]),
      wg2, wu2, wd2, row(ln_ple[0]), w_ple_gate[0].astype(BF16), w_ple_proj[0].astype(BF16), row(ln_final))

    return y.reshape(bsz, seq, D_MODEL)
```

```python
import functools

import jax
import jax.numpy as jnp
from jax import lax
from jax.experimental import pallas as pl
from jax.experimental.pallas import tpu as pltpu

F32 = jnp.float32
BF16 = jnp.bfloat16

D_MODEL = 1024
D_FF = 2816
PLE_DIM = 256
HG_HEADS = 4
HG_DIM = 128
HG_WIDTH = HG_HEADS * HG_DIM
MLA_HEADS = 4
Q_LORA = 256
KV_LORA = 128
NOPE_DIM = 128
ROPE_DIM = 64
V_DIM = 128
QK_DIM = NOPE_DIM + ROPE_DIM
ROPE_THETA = 10000.0
EPS = 1e-6
LOG2_E = 1.4426950408889634

LANES = 128
SUBLANES = 8
HEAD_PAD = 2 * LANES
IN_PAD = 4 * HG_WIDTH + Q_LORA + KV_LORA + LANES

FFN_CHUNK = 256
TOKEN_TILE = 512
HGRN_CHUNK = 256
ATTN_BLOCK = 256
HGRN_AFTER_FFN_CHUNKS = (1, 3, 5, 7, 9, 10)
VMEM_LIMIT = 56 * 1024 * 1024

NT_DIMS = (((1,), (1,)), ((), ()))
TN_DIMS = (((0,), (0,)), ((), ()))


def _dot(a, b):
    return jnp.dot(a, b, preferred_element_type=F32)


def _rms(x, g):
    return x * lax.rsqrt(jnp.mean(x * x, axis=-1, keepdims=True) + EPS) * g


def _swiglu(xn, wg_ref, wu_ref, wd_ref, between=None):
    acc = None
    for idx, c in enumerate(range(0, D_FF, FFN_CHUNK)):
        g = _dot(xn, wg_ref[:, c:c + FFN_CHUNK])
        u = _dot(xn, wu_ref[:, c:c + FFN_CHUNK])
        a = (g * jax.nn.sigmoid(g) * u).astype(BF16)
        d = _dot(a, wd_ref[c:c + FFN_CHUNK, :])
        acc = d if acc is None else acc + d
        for piece in (between or {}).get(idx, ()):
            piece()
    return acc


def _rope(x, cos_t, sin_lo, sin_hi):
    return (x * cos_t + pltpu.roll(x, LANES - ROPE_DIM // 2, 1) * sin_lo
            + pltpu.roll(x, ROPE_DIM // 2, 1) * sin_hi)


def _premix_kernel(x_ref, pos_ref, invf_ref, g1_ref, wg_ref, wu_ref, wd_ref, gmix_ref, win_ref,
                   qan_ref, wq_ref, kvan_ref, wk_ref, wv_ref, lbl_ref, gn_ref,
                   h1_ref, qp_ref, kp_ref, v_ref, ohg_ref,
                   hq_s, hf_s, hi_s, hg_s, st_ref, *, tiles_per_row, chunk):
    i = pl.program_id(0)
    tm = x_ref.shape[0]

    @pl.when(i == 0)
    def _():
        hq_s[...] = jnp.zeros_like(hq_s)
        hf_s[...] = jnp.zeros_like(hf_s)
        hi_s[...] = jnp.zeros_like(hi_s)
        hg_s[...] = jnp.zeros_like(hg_s)
        st_ref[...] = jnp.zeros_like(st_ref)

    @pl.when(lax.rem(i + tiles_per_row - 1, tiles_per_row) == 0)
    def _():
        st_ref[...] = jnp.zeros_like(st_ref)

    pieces = []
    for r0 in range(0, tm, chunk):
        pieces += _hgrn_pieces(lbl_ref, hq_s, hf_s, hi_s, hg_s, gn_ref, ohg_ref, st_ref, r0, chunk)
    early = {c: [piece] for c, piece in zip(HGRN_AFTER_FFN_CHUNKS, pieces)}
    late = iter(pieces[len(HGRN_AFTER_FFN_CHUNKS):])

    ang = pos_ref[...].astype(F32) * invf_ref[...]
    lane = lax.broadcasted_iota(jnp.int32, ang.shape, 1)
    cos_a, sin_a = jnp.cos(ang), jnp.sin(ang)
    half = ROPE_DIM // 2
    cos_t = jnp.where(lane < ROPE_DIM, cos_a, 0.0)
    sin_lo = jnp.where(lane < half, -sin_a, 0.0)
    sin_hi = jnp.where((lane >= half) & (lane < ROPE_DIM), sin_a, 0.0)

    x = x_ref[...]
    xn = _rms(x, g1_ref[...]).astype(BF16)
    h1 = x + 0.5 * _swiglu(xn, wg_ref, wu_ref, wd_ref, early)
    h1_ref[...] = h1

    hn = _rms(h1, gmix_ref[...]).astype(BF16)
    w = HG_WIDTH
    new = []
    for j in range(4):
        new.append(_dot(hn, win_ref[:, j * w:(j + 1) * w]))
        piece = next(late, None)
        if piece is not None:
            piece()
    for piece in late:
        piece()
    c0 = 4 * w
    cq = _dot(hn, win_ref[:, c0:c0 + Q_LORA])
    ckv_kr = _dot(hn, win_ref[:, c0 + Q_LORA:IN_PAD])
    ckv = ckv_kr[:, 0:KV_LORA]
    kr = ckv_kr[:, KV_LORA:KV_LORA + LANES]

    q = _dot(_rms(cq, qan_ref[...]).astype(BF16), wq_ref[...]) * (QK_DIM ** -0.5 * LOG2_E)
    ckvn = _rms(ckv, kvan_ref[...]).astype(BF16)
    k_nope = _dot(ckvn, wk_ref[...])
    v_ref[...] = _dot(ckvn, wv_ref[...]).astype(v_ref.dtype)
    k_rope = _rope(kr, cos_t, sin_lo, sin_hi).astype(kp_ref.dtype)
    for h in range(MLA_HEADS):
        a = h * HEAD_PAD
        qp_ref[:, a:a + NOPE_DIM] = q[:, a:a + NOPE_DIM].astype(qp_ref.dtype)
        qp_ref[:, a + NOPE_DIM:a + HEAD_PAD] = _rope(
            q[:, a + NOPE_DIM:a + HEAD_PAD], cos_t, sin_lo, sin_hi).astype(qp_ref.dtype)
        kp_ref[:, a:a + NOPE_DIM] = k_nope[:, h * NOPE_DIM:(h + 1) * NOPE_DIM].astype(kp_ref.dtype)
        kp_ref[:, a + NOPE_DIM:a + HEAD_PAD] = k_rope

    hq_s[...] = new[0].astype(hq_s.dtype)
    hf_s[...] = new[1]
    hi_s[...] = new[2].astype(hi_s.dtype)
    hg_s[...] = new[3].astype(hg_s.dtype)


def _neg_gap(b, m):
    t = b.shape[0]
    if m >= SUBLANES:
        pieces = []
        for s in range(0, t, 2 * m):
            mid = b[s + m - 1:s + m, :]
            pieces += [mid - b[s:s + m], b[s + m:s + 2 * m] - mid]
        return jnp.concatenate(pieces, axis=0)
    b3 = b.reshape(t // SUBLANES, SUBLANES, LANES)
    sub = lax.broadcasted_iota(jnp.int32, b3.shape, 1)

    def row(i):
        return jnp.broadcast_to(b3[:, i:i + 1, :], b3.shape)

    mid = row(m - 1)
    for s in range(2 * m, SUBLANES, 2 * m):
        mid = jnp.where(sub >= s, row(s + m - 1), mid)
    return -jnp.abs(b3 - mid).reshape(t, LANES)


def _hgrn_pieces(lbl_ref, hq_ref, hf_ref, hi_ref, hg_ref, gn_ref, o_ref, st_ref, r0, t):
    rows = slice(r0, r0 + t)
    half = t // 2
    ctx = {}

    def gates():
        lg = lbl_ref[...]
        e = jnp.exp(lg - jnp.max(lg, axis=0, keepdims=True))
        lb = e[0:1, :] / jnp.sum(e, axis=0, keepdims=True)
        f_raw = hf_ref[rows, :]
        f = lb + (1.0 - lb) * jax.nn.sigmoid(f_raw)
        g = jnp.log(f)
        ctx["f"] = f
        ctx["kk"] = (1.0 - lb) * jax.nn.sigmoid(-f_raw)

        row = lax.broadcasted_iota(jnp.int32, (t, t), 0)
        col = lax.broadcasted_iota(jnp.int32, (t, t), 1)
        tri = (row >= col).astype(BF16)
        g1 = g.astype(BF16)
        r1 = g - g1.astype(F32)
        g2 = r1.astype(BF16)
        g3 = (r1 - g2.astype(F32)).astype(BF16)
        ctx["b"] = (_dot(tri, g1) + _dot(tri, g2) + _dot(tri, g3)) * LOG2_E

    def head(h):
        ri = lax.broadcasted_iota(jnp.int32, (half, half), 0)
        ci = lax.broadcasted_iota(jnp.int32, (half, half), 1)
        xm = jnp.where(ri > ci, ri ^ ci, 0)
        levels = [1 << i for i in range(half.bit_length() - 1)]
        masks = [ri == ci] + [(xm >= m) & (xm < 2 * m) for m in levels]
        odd_row = (lax.broadcasted_iota(jnp.int32, (t, HG_DIM), 0) & 1) == 1

        sl = slice(h * HG_DIM, (h + 1) * HG_DIM)
        q = hq_ref[rows, sl]
        v = hi_ref[rows, sl]
        kb = ctx["kk"][:, sl].astype(BF16)
        b = ctx["b"][:, sl]

        qk = [(q, kb)]
        for m in levels:
            if m == 1:
                decay = jnp.where(odd_row, ctx["f"][:, sl], 1.0).astype(BF16)
            else:
                decay = jnp.exp2(_neg_gap(b, m)).astype(BF16)
            qk.append((q * decay, kb * decay))
        diag = []
        for q0 in (0, half):
            p = None
            for (q_l, k_l), mask in zip(qk, masks):
                s_l = lax.dot_general(q_l[q0:q0 + half], k_l[q0:q0 + half], NT_DIMS, preferred_element_type=F32)
                p = jnp.where(mask, s_l, 0.0 if p is None else p)
            diag.append(p.astype(BF16))
        mid = b[half - 1:half, :]
        cross = lax.dot_general(q[half:] * jnp.exp2(b[half:] - mid).astype(BF16),
                                kb[:half] * jnp.exp2(mid - b[:half]).astype(BF16), NT_DIMS,
                                preferred_element_type=F32).astype(BF16)
        o = jnp.concatenate([_dot(diag[0], v[:half]),
                             _dot(jnp.concatenate([cross, diag[1]], axis=1), v)], axis=0)

        st = st_ref[h]
        b_last = b[t - 1:t, :]
        o = o + lax.dot_general(q * jnp.exp2(b).astype(BF16), st.astype(BF16), NT_DIMS,
                                preferred_element_type=F32)
        k_end = kb * jnp.exp2(b_last - b).astype(BF16)
        st_ref[h] = st * jnp.exp2(b_last) + lax.dot_general(v, k_end, TN_DIMS, preferred_element_type=F32)

        gate = hg_ref[rows, sl].astype(F32)
        o_ref[rows, sl] = (_rms(o, gn_ref[h:h + 1, :]) * (gate * jax.nn.sigmoid(gate))).astype(o_ref.dtype)

    return [gates] + [functools.partial(head, h) for h in range(HG_HEADS)]


def _attn_kernel(q_ref, k_ref, v_ref, o_ref, vaug_ref, *, tq):
    seq = q_ref.shape[0]
    vaug_ref[:, 0:V_DIM] = v_ref[...]
    vaug_ref[:, V_DIM:2 * V_DIM] = jnp.ones((seq, V_DIM), vaug_ref.dtype)
    row = lax.broadcasted_iota(jnp.int32, (tq, tq), 0)
    col = lax.broadcasted_iota(jnp.int32, (tq, tq), 1)
    nq = seq // tq

    def scores(i):
        lo = i * tq
        q = q_ref[lo:lo + tq, :]
        s_d = lax.dot_general(q, k_ref[lo:lo + tq, :], NT_DIMS, preferred_element_type=F32)
        s_d = jnp.where(row >= col, s_d, -jnp.inf)
        s_p = lax.dot_general(q, k_ref[0:lo, :], NT_DIMS, preferred_element_type=F32) if i > 0 else None
        return s_d, s_p

    nxt = scores(0)
    for i in range(nq):
        lo = i * tq
        s_d, s_p = nxt
        if i + 1 < nq:
            nxt = scores(i + 1)
        m = jnp.max(s_d, axis=-1, keepdims=True)
        if i > 0:
            m = jnp.maximum(m, jnp.max(s_p, axis=-1, keepdims=True))
        acc = _dot(jnp.exp2(s_d - m).astype(BF16), vaug_ref[lo:lo + tq, :])
        if i > 0:
            acc += _dot(jnp.exp2(s_p - m).astype(BF16), vaug_ref[0:lo, :])
        o_ref[lo:lo + tq, :] = (acc[:, 0:V_DIM] / acc[:, V_DIM:2 * V_DIM]).astype(o_ref.dtype)


def _postmix_kernel(h1_ref, ohg_ref, omla_ref, p_ref, wo_hg_ref, wo_mla_ref, g2_ref, wg_ref, wu_ref, wd_ref,
                    gple_ref, wpg_ref, wpp_ref, gfin_ref, y_ref):
    h2 = h1_ref[...] + _dot(ohg_ref[...], wo_hg_ref[...]) + _dot(omla_ref[...], wo_mla_ref[...])
    h3 = h2 + 0.5 * _swiglu(_rms(h2, g2_ref[...]).astype(BF16), wg_ref, wu_ref, wd_ref)
    gate = jax.nn.sigmoid(_dot(_rms(h3, gple_ref[...]).astype(BF16), wpg_ref[...]))
    h4 = h3 + gate * _dot(p_ref[...].astype(BF16), wpp_ref[...])
    y_ref[...] = _rms(h4, gfin_ref[...])


def _resident(shape):
    return pl.BlockSpec(shape, lambda *_: (0,) * len(shape), pipeline_mode=pl.Buffered(1))


def _rows(tile, width):
    return pl.BlockSpec((tile, width), lambda i: (i, 0))


def kernel(x, p, positions, ln_ffn1, w1_gate, w1_up, w1_down, ln_mix, w_in, hg_lb_logits, hg_out_norm,
           q_a_norm, w_q_up, kv_a_norm, w_kv_up, w_out, ln_ffn2, w2_gate, w2_up, w2_down, ln_ple,
           w_ple_gate, w_ple_proj, ln_final):
    bsz, seq, _ = x.shape
    assert p.shape[0] == 1 and hg_lb_logits.shape[0] == 2, "single-layer trunk"
    n = bsz * seq
    tm = min(TOKEN_TILE, seq)
    tc = min(HGRN_CHUNK, seq)
    tq = min(ATTN_BLOCK, seq)
    assert seq % tm == 0 and tm % tc == 0 and seq % tq == 0

    x2 = x.reshape(n, D_MODEL)
    pos2 = positions.reshape(n, 1)
    half = ROPE_DIM // 2
    inv_freq = ROPE_THETA ** (-jnp.arange(half, dtype=F32) / half)
    invf = jnp.concatenate([inv_freq, inv_freq, jnp.zeros((LANES - ROPE_DIM,), F32)]).reshape(1, LANES)
    wg1, wu1, wd1 = w1_gate[0].astype(BF16), w1_up[0].astype(BF16), w1_down[0].astype(BF16)
    wg2, wu2, wd2 = w2_gate[0].astype(BF16), w2_up[0].astype(BF16), w2_down[0].astype(BF16)
    win = jnp.pad(w_in[0].astype(BF16), ((0, 0), (0, IN_PAD - w_in.shape[-1])))
    wq = jnp.pad(w_q_up[0].astype(BF16).reshape(Q_LORA, MLA_HEADS, QK_DIM),
                 ((0, 0), (0, 0), (0, HEAD_PAD - QK_DIM))).reshape(Q_LORA, MLA_HEADS * HEAD_PAD)
    wkv = w_kv_up[0].astype(BF16).reshape(KV_LORA, MLA_HEADS, NOPE_DIM + V_DIM)
    wk = wkv[:, :, :NOPE_DIM].reshape(KV_LORA, MLA_HEADS * NOPE_DIM)
    wv = wkv[:, :, NOPE_DIM:].reshape(KV_LORA, MLA_HEADS * V_DIM)
    wo = w_out[0].astype(BF16)
    row = lambda a: a.reshape(1, -1)

    steps = n // tm
    tile = lambda width: pl.BlockSpec((tm, width), lambda i: (jnp.minimum(i, steps - 1), 0))
    prev_tile = pl.BlockSpec((tm, HG_WIDTH), lambda i: (jnp.maximum(i - 1, 0), 0))
    h1, qp, kp, vv, o_hg = pl.pallas_call(
        functools.partial(_premix_kernel, tiles_per_row=seq // tm, chunk=tc),
        name="premix",
        grid=(steps + 1,),
        in_specs=[
            tile(D_MODEL), tile(1), _resident((1, LANES)), _resident((1, D_MODEL)),
            _resident((D_MODEL, D_FF)), _resident((D_MODEL, D_FF)), _resident((D_FF, D_MODEL)),
            _resident((1, D_MODEL)), _resident((D_MODEL, IN_PAD)),
            _resident((1, Q_LORA)), _resident((Q_LORA, MLA_HEADS * HEAD_PAD)), _resident((1, KV_LORA)),
            _resident((KV_LORA, MLA_HEADS * NOPE_DIM)), _resident((KV_LORA, MLA_HEADS * V_DIM)),
            _resident((2, HG_WIDTH)), _resident((HG_HEADS, HG_DIM)),
        ],
        out_specs=[tile(D_MODEL), tile(MLA_HEADS * HEAD_PAD), tile(MLA_HEADS * HEAD_PAD),
                   tile(MLA_HEADS * V_DIM), prev_tile],
        out_shape=[
            jax.ShapeDtypeStruct((n, D_MODEL), F32),
            jax.ShapeDtypeStruct((n, MLA_HEADS * HEAD_PAD), BF16),
            jax.ShapeDtypeStruct((n, MLA_HEADS * HEAD_PAD), BF16),
            jax.ShapeDtypeStruct((n, MLA_HEADS * V_DIM), BF16),
            jax.ShapeDtypeStruct((n, HG_WIDTH), BF16),
        ],
        scratch_shapes=[pltpu.VMEM((tm, HG_WIDTH), BF16), pltpu.VMEM((tm, HG_WIDTH), F32),
                        pltpu.VMEM((tm, HG_WIDTH), BF16), pltpu.VMEM((tm, HG_WIDTH), BF16),
                        pltpu.VMEM((HG_HEADS, HG_DIM, HG_DIM), F32)],
        compiler_params=pltpu.CompilerParams(dimension_semantics=("arbitrary",), vmem_limit_bytes=VMEM_LIMIT),
    )(x2, pos2, invf, row(ln_ffn1[0]), wg1, wu1, wd1, row(ln_mix[0]), win,
      row(q_a_norm[0]), wq, row(kv_a_norm[0]), wk, wv, hg_lb_logits, hg_out_norm[0])

    head_spec = lambda width: pl.BlockSpec((seq, width), lambda b, h: (b, h))
    o_mla = pl.pallas_call(
        functools.partial(_attn_kernel, tq=tq),
        name="mla_attn",
        grid=(bsz, MLA_HEADS),
        in_specs=[head_spec(HEAD_PAD), head_spec(HEAD_PAD), head_spec(V_DIM)],
        out_specs=head_spec(V_DIM),
        out_shape=jax.ShapeDtypeStruct((n, MLA_HEADS * V_DIM), BF16),
        scratch_shapes=[pltpu.VMEM((seq, 2 * V_DIM), BF16)],
        compiler_params=pltpu.CompilerParams(dimension_semantics=("parallel", "parallel"),
                                             vmem_limit_bytes=VMEM_LIMIT),
    )(qp, kp, vv)

    y = pl.pallas_call(
        _postmix_kernel,
        name="postmix",
        grid=(steps,),
        in_specs=[
            _rows(tm, D_MODEL), _rows(tm, HG_WIDTH), _rows(tm, MLA_HEADS * V_DIM), _rows(tm, PLE_DIM),
            _resident((HG_WIDTH, D_MODEL)), _resident((MLA_HEADS * V_DIM, D_MODEL)), _resident((1, D_MODEL)),
            _resident((D_MODEL, D_FF)), _resident((D_MODEL, D_FF)), _resident((D_FF, D_MODEL)),
            _resident((1, D_MODEL)), _resident((D_MODEL, D_MODEL)), _resident((PLE_DIM, D_MODEL)),
            _resident((1, D_MODEL)),
        ],
        out_specs=_rows(tm, D_MODEL),
        out_shape=jax.ShapeDtypeStruct((n, D_MODEL), F32),
        compiler_params=pltpu.CompilerParams(dimension_semantics=("parallel",), vmem_limit_bytes=VMEM_LIMIT),
    )(h1, o_hg, o_mla, p[0].reshape(n, PLE_DIM), wo[:HG_WIDTH], wo[HG_WIDTH:], row(ln_ffn2[0]),
      wg2, wu2, wd2, row(ln_ple[0]), w_ple_gate[0].astype(BF16), w_ple_proj[0].astype(BF16), row(ln_final))

    return y.reshape(bsz, seq, D_MODEL)
```

```python
import functools

import jax
import jax.numpy as jnp
from jax import lax
from jax.experimental import pallas as pl
from jax.experimental.pallas import tpu as pltpu

F32 = jnp.float32
BF16 = jnp.bfloat16

D_MODEL = 1024
D_FF = 2816
PLE_DIM = 256
HG_HEADS = 4
HG_DIM = 128
HG_WIDTH = HG_HEADS * HG_DIM
MLA_HEADS = 4
Q_LORA = 256
KV_LORA = 128
NOPE_DIM = 128
ROPE_DIM = 64
V_DIM = 128
QK_DIM = NOPE_DIM + ROPE_DIM
ROPE_THETA = 10000.0
EPS = 1e-6
LOG2_E = 1.4426950408889634

LANES = 128
SUBLANES = 8
HEAD_PAD = 2 * LANES
IN_WIDTH = 4 * HG_WIDTH + Q_LORA + KV_LORA + ROPE_DIM

FFN_CHUNK = 256
TOKEN_TILE = 512
HGRN_CHUNK = 256
ATTN_BLOCK = 256
ATTN_LOOKAHEAD = 3
HGRN_AFTER_FFN_CHUNKS = (1, 3, 5, 7, 9, 10)
VMEM_LIMIT = 56 * 1024 * 1024

NT_DIMS = (((1,), (1,)), ((), ()))
TN_DIMS = (((0,), (0,)), ((), ()))


def _dot(a, b):
    return jnp.dot(a, b, preferred_element_type=F32)


def _rms(x, g):
    return x * lax.rsqrt(jnp.mean(x * x, axis=-1, keepdims=True) + EPS) * g


def _swiglu(xn, wg_ref, wu_ref, wd_ref, between=None):
    acc = None
    for idx, c in enumerate(range(0, D_FF, FFN_CHUNK)):
        g = _dot(xn, wg_ref[:, c:c + FFN_CHUNK])
        u = _dot(xn, wu_ref[:, c:c + FFN_CHUNK])
        a = (g * jax.nn.sigmoid(g) * u).astype(BF16)
        d = _dot(a, wd_ref[c:c + FFN_CHUNK, :])
        acc = d if acc is None else acc + d
        for piece in (between or {}).get(idx, ()):
            piece()
    return acc


def _rope(x, cos_t, sin_lo, sin_hi):
    return (x * cos_t + pltpu.roll(x, LANES - ROPE_DIM // 2, 1) * sin_lo
            + pltpu.roll(x, ROPE_DIM // 2, 1) * sin_hi)


def _premix_kernel(x_ref, pos_ref, invf_ref, g1_ref, wg_ref, wu_ref, wd_ref, gmix_ref, win_ref,
                   qan_ref, wq_ref, kvan_ref, wk_ref, wv_ref, lbl_ref, gn_ref,
                   h1_ref, qp_ref, kp_ref, v_ref, ohg_ref,
                   hq_s, hf_s, hi_s, hg_s, st_ref, *, tiles_per_row, chunk):
    i = pl.program_id(0)
    tm = x_ref.shape[0]

    @pl.when(i == 0)
    def _():
        hq_s[...] = jnp.zeros_like(hq_s)
        hf_s[...] = jnp.zeros_like(hf_s)
        hi_s[...] = jnp.zeros_like(hi_s)
        hg_s[...] = jnp.zeros_like(hg_s)
        st_ref[...] = jnp.zeros_like(st_ref)

    @pl.when(lax.rem(i + tiles_per_row - 1, tiles_per_row) == 0)
    def _():
        st_ref[...] = jnp.zeros_like(st_ref)

    pieces = []
    for r0 in range(0, tm, chunk):
        pieces += _hgrn_pieces(lbl_ref, hq_s, hf_s, hi_s, hg_s, gn_ref, ohg_ref, st_ref, r0, chunk)
    early_chunks = [c for c in HGRN_AFTER_FFN_CHUNKS if c < D_FF // FFN_CHUNK][:len(pieces)]
    early = {c: [piece] for c, piece in zip(early_chunks, pieces)}
    late = iter(pieces[len(early_chunks):])

    ang = pos_ref[...].astype(F32) * invf_ref[...]
    lane = lax.broadcasted_iota(jnp.int32, ang.shape, 1)
    cos_a, sin_a = jnp.cos(ang), jnp.sin(ang)
    half = ROPE_DIM // 2
    cos_t = jnp.where(lane < ROPE_DIM, cos_a, 0.0)
    sin_lo = jnp.where(lane < half, -sin_a, 0.0)
    sin_hi = jnp.where((lane >= half) & (lane < ROPE_DIM), sin_a, 0.0)

    x = x_ref[...]
    xn = _rms(x, g1_ref[...]).astype(BF16)
    h1 = x + 0.5 * _swiglu(xn, wg_ref, wu_ref, wd_ref, early)
    h1_ref[...] = h1

    hn = _rms(h1, gmix_ref[...]).astype(BF16)
    w = HG_WIDTH
    new = []
    for j in range(4):
        new.append(_dot(hn, win_ref[:, j * w:(j + 1) * w]))
        piece = next(late, None)
        if piece is not None:
            piece()
    for piece in late:
        piece()
    c0 = 4 * w
    cq = _dot(hn, win_ref[:, c0:c0 + Q_LORA])
    ckv_kr = _dot(hn, win_ref[:, c0 + Q_LORA:IN_WIDTH])
    ckv = ckv_kr[:, 0:KV_LORA]
    kr = jnp.concatenate([ckv_kr[:, KV_LORA:KV_LORA + ROPE_DIM],
                          jnp.zeros((tm, LANES - ROPE_DIM), F32)], axis=1)

    q = _dot(_rms(cq, qan_ref[...]).astype(BF16), wq_ref[...]) * (QK_DIM ** -0.5 * LOG2_E)
    ckvn = _rms(ckv, kvan_ref[...]).astype(BF16)
    k_nope = _dot(ckvn, wk_ref[...])
    v_ref[...] = _dot(ckvn, wv_ref[...]).astype(v_ref.dtype)
    k_rope = _rope(kr, cos_t, sin_lo, sin_hi).astype(kp_ref.dtype)
    for h in range(MLA_HEADS):
        a = h * HEAD_PAD
        qp_ref[:, a:a + NOPE_DIM] = q[:, a:a + NOPE_DIM].astype(qp_ref.dtype)
        qp_ref[:, a + NOPE_DIM:a + HEAD_PAD] = _rope(
            q[:, a + NOPE_DIM:a + HEAD_PAD], cos_t, sin_lo, sin_hi).astype(qp_ref.dtype)
        kp_ref[:, a:a + NOPE_DIM] = k_nope[:, h * NOPE_DIM:(h + 1) * NOPE_DIM].astype(kp_ref.dtype)
        kp_ref[:, a + NOPE_DIM:a + HEAD_PAD] = k_rope

    hq_s[...] = new[0].astype(hq_s.dtype)
    hf_s[...] = new[1]
    hi_s[...] = new[2].astype(hi_s.dtype)
    hg_s[...] = new[3].astype(hg_s.dtype)


def _neg_gap(b, m):
    t = b.shape[0]
    if m >= SUBLANES:
        pieces = []
        for s in range(0, t, 2 * m):
            mid = b[s + m - 1:s + m, :]
            pieces += [mid - b[s:s + m], b[s + m:s + 2 * m] - mid]
        return jnp.concatenate(pieces, axis=0)
    b3 = b.reshape(t // SUBLANES, SUBLANES, LANES)
    sub = lax.broadcasted_iota(jnp.int32, b3.shape, 1)

    def row(i):
        return jnp.broadcast_to(b3[:, i:i + 1, :], b3.shape)

    mid = row(m - 1)
    for s in range(2 * m, SUBLANES, 2 * m):
        mid = jnp.where(sub >= s, row(s + m - 1), mid)
    return -jnp.abs(b3 - mid).reshape(t, LANES)


def _hgrn_pieces(lbl_ref, hq_ref, hf_ref, hi_ref, hg_ref, gn_ref, o_ref, st_ref, r0, t):
    rows = slice(r0, r0 + t)
    half = t // 2
    ctx = {}

    def gates():
        lg = lbl_ref[...]
        e = jnp.exp(lg - jnp.max(lg, axis=0, keepdims=True))
        lb = e[0:1, :] / jnp.sum(e, axis=0, keepdims=True)
        f_raw = hf_ref[rows, :]
        f = lb + (1.0 - lb) * jax.nn.sigmoid(f_raw)
        g = jnp.log(f)
        ctx["f"] = f
        ctx["kk"] = (1.0 - lb) * jax.nn.sigmoid(-f_raw)

        row = lax.broadcasted_iota(jnp.int32, (t, t), 0)
        col = lax.broadcasted_iota(jnp.int32, (t, t), 1)
        tri = (row >= col).astype(BF16)
        g1 = g.astype(BF16)
        r1 = g - g1.astype(F32)
        g2 = r1.astype(BF16)
        g3 = (r1 - g2.astype(F32)).astype(BF16)
        ctx["b"] = (_dot(tri, g1) + _dot(tri, g2) + _dot(tri, g3)) * LOG2_E

    def head(h):
        ri = lax.broadcasted_iota(jnp.int32, (half, half), 0)
        ci = lax.broadcasted_iota(jnp.int32, (half, half), 1)
        xm = jnp.where(ri > ci, ri ^ ci, 0)
        levels = [1 << i for i in range(half.bit_length() - 1)]
        masks = [ri == ci] + [(xm >= m) & (xm < 2 * m) for m in levels]
        odd_row = (lax.broadcasted_iota(jnp.int32, (t, HG_DIM), 0) & 1) == 1

        sl = slice(h * HG_DIM, (h + 1) * HG_DIM)
        q = hq_ref[rows, sl]
        v = hi_ref[rows, sl]
        kb = ctx["kk"][:, sl].astype(BF16)
        b = ctx["b"][:, sl]

        qk = [(q, kb)]
        for m in levels:
            if m == 1:
                decay = jnp.where(odd_row, ctx["f"][:, sl], 1.0).astype(BF16)
            else:
                decay = jnp.exp2(_neg_gap(b, m)).astype(BF16)
            qk.append((q * decay, kb * decay))
        diag = []
        for q0 in (0, half):
            p = None
            for (q_l, k_l), mask in zip(qk, masks):
                s_l = lax.dot_general(q_l[q0:q0 + half], k_l[q0:q0 + half], NT_DIMS, preferred_element_type=F32)
                p = jnp.where(mask, s_l, 0.0 if p is None else p)
            diag.append(p.astype(BF16))
        mid = b[half - 1:half, :]
        cross = lax.dot_general(q[half:] * jnp.exp2(b[half:] - mid).astype(BF16),
                                kb[:half] * jnp.exp2(mid - b[:half]).astype(BF16), NT_DIMS,
                                preferred_element_type=F32).astype(BF16)
        o = jnp.concatenate([_dot(diag[0], v[:half]),
                             _dot(jnp.concatenate([cross, diag[1]], axis=1), v)], axis=0)

        st = st_ref[h]
        b_last = b[t - 1:t, :]
        o = o + lax.dot_general(q * jnp.exp2(b).astype(BF16), st.astype(BF16), NT_DIMS,
                                preferred_element_type=F32)
        k_end = kb * jnp.exp2(b_last - b).astype(BF16)
        st_ref[h] = st * jnp.exp2(b_last) + lax.dot_general(v, k_end, TN_DIMS, preferred_element_type=F32)

        gate = hg_ref[rows, sl].astype(F32)
        o_ref[rows, sl] = (_rms(o, gn_ref[h:h + 1, :]) * (gate * jax.nn.sigmoid(gate))).astype(o_ref.dtype)

    return [gates] + [functools.partial(head, h) for h in range(HG_HEADS)]


def _attn_kernel(q_ref, k_ref, v_ref, o_ref, vaug_ref, *, tq):
    seq = q_ref.shape[0]
    vaug_ref[:, 0:V_DIM] = v_ref[...]
    vaug_ref[:, V_DIM:2 * V_DIM] = jnp.ones((seq, V_DIM), vaug_ref.dtype)
    row = lax.broadcasted_iota(jnp.int32, (tq, tq), 0)
    col = lax.broadcasted_iota(jnp.int32, (tq, tq), 1)
    nq = seq // tq

    def scores(i):
        lo = i * tq
        q = q_ref[lo:lo + tq, :]
        s_d = lax.dot_general(q, k_ref[lo:lo + tq, :], NT_DIMS, preferred_element_type=F32)
        s_d = jnp.where(row >= col, s_d, -jnp.inf)
        s_p = lax.dot_general(q, k_ref[0:lo, :], NT_DIMS, preferred_element_type=F32) if i > 0 else None
        return s_d, s_p

    pending = [scores(j) for j in range(min(ATTN_LOOKAHEAD, nq))]
    for i in range(nq):
        lo = i * tq
        s_d, s_p = pending.pop(0)
        if i + ATTN_LOOKAHEAD < nq:
            pending.append(scores(i + ATTN_LOOKAHEAD))
        m = jnp.max(s_d, axis=-1, keepdims=True)
        if i > 0:
            m = jnp.maximum(m, jnp.max(s_p, axis=-1, keepdims=True))
        acc = _dot(jnp.exp2(s_d - m).astype(BF16), vaug_ref[lo:lo + tq, :])
        if i > 0:
            acc += _dot(jnp.exp2(s_p - m).astype(BF16), vaug_ref[0:lo, :])
        o_ref[lo:lo + tq, :] = (acc[:, 0:V_DIM] / acc[:, V_DIM:2 * V_DIM]).astype(o_ref.dtype)


def _postmix_kernel(h1_ref, ohg_ref, omla_ref, p_ref, wo_ref, g2_ref, wg_ref, wu_ref, wd_ref,
                    gple_ref, wpg_ref, wpp_ref, gfin_ref, y_ref):
    h2 = (h1_ref[...] + _dot(ohg_ref[...], wo_ref[0:HG_WIDTH, :])
          + _dot(omla_ref[...], wo_ref[HG_WIDTH:HG_WIDTH + MLA_HEADS * V_DIM, :]))
    h3 = h2 + 0.5 * _swiglu(_rms(h2, g2_ref[...]).astype(BF16), wg_ref, wu_ref, wd_ref)
    gate = jax.nn.sigmoid(_dot(_rms(h3, gple_ref[...]).astype(BF16), wpg_ref[...]))
    h4 = h3 + gate * _dot(p_ref[...].astype(BF16), wpp_ref[...])
    y_ref[...] = _rms(h4, gfin_ref[...])


def _resident(shape):
    return pl.BlockSpec(shape, lambda *_: (0,) * len(shape), pipeline_mode=pl.Buffered(1))


def _rows(tile, width):
    return pl.BlockSpec((tile, width), lambda i: (i, 0))


def kernel(x, p, positions, ln_ffn1, w1_gate, w1_up, w1_down, ln_mix, w_in, hg_lb_logits, hg_out_norm,
           q_a_norm, w_q_up, kv_a_norm, w_kv_up, w_out, ln_ffn2, w2_gate, w2_up, w2_down, ln_ple,
           w_ple_gate, w_ple_proj, ln_final):
    bsz, seq, _ = x.shape
    assert p.shape[0] == 1 and hg_lb_logits.shape[0] == 2, "single-layer trunk"
    n = bsz * seq
    tm = min(TOKEN_TILE, seq)
    tc = min(HGRN_CHUNK, seq)
    tq = min(ATTN_BLOCK, seq)
    assert seq % tm == 0 and tm % tc == 0 and seq % tq == 0

    x2 = x.reshape(n, D_MODEL)
    pos2 = positions.reshape(n, 1)
    half = ROPE_DIM // 2
    inv_freq = ROPE_THETA ** (-jnp.arange(half, dtype=F32) / half)
    invf = jnp.concatenate([inv_freq, inv_freq, jnp.zeros((LANES - ROPE_DIM,), F32)]).reshape(1, LANES)
    wg1, wu1, wd1 = w1_gate[0].astype(BF16), w1_up[0].astype(BF16), w1_down[0].astype(BF16)
    wg2, wu2, wd2 = w2_gate[0].astype(BF16), w2_up[0].astype(BF16), w2_down[0].astype(BF16)
    win = w_in[0].astype(BF16)
    wq = jnp.pad(w_q_up[0].astype(BF16).reshape(Q_LORA, MLA_HEADS, QK_DIM),
                 ((0, 0), (0, 0), (0, HEAD_PAD - QK_DIM))).reshape(Q_LORA, MLA_HEADS * HEAD_PAD)
    wkv = w_kv_up[0].astype(BF16).reshape(KV_LORA, MLA_HEADS, NOPE_DIM + V_DIM)
    wk = wkv[:, :, :NOPE_DIM].reshape(KV_LORA, MLA_HEADS * NOPE_DIM)
    wv = wkv[:, :, NOPE_DIM:].reshape(KV_LORA, MLA_HEADS * V_DIM)
    wo = w_out[0].astype(BF16)
    row = lambda a: a.reshape(1, -1)

    steps = n // tm
    tile = lambda width: pl.BlockSpec((tm, width), lambda i: (jnp.minimum(i, steps - 1), 0))
    prev_tile = pl.BlockSpec((tm, HG_WIDTH), lambda i: (jnp.maximum(i - 1, 0), 0))
    h1, qp, kp, vv, o_hg = pl.pallas_call(
        functools.partial(_premix_kernel, tiles_per_row=seq // tm, chunk=tc),
        name="premix",
        grid=(steps + 1,),
        in_specs=[
            tile(D_MODEL), tile(1), _resident((1, LANES)), _resident((1, D_MODEL)),
            _resident((D_MODEL, D_FF)), _resident((D_MODEL, D_FF)), _resident((D_FF, D_MODEL)),
            _resident((1, D_MODEL)), _resident((D_MODEL, IN_WIDTH)),
            _resident((1, Q_LORA)), _resident((Q_LORA, MLA_HEADS * HEAD_PAD)), _resident((1, KV_LORA)),
            _resident((KV_LORA, MLA_HEADS * NOPE_DIM)), _resident((KV_LORA, MLA_HEADS * V_DIM)),
            _resident((2, HG_WIDTH)), _resident((HG_HEADS, HG_DIM)),
        ],
        out_specs=[tile(D_MODEL), tile(MLA_HEADS * HEAD_PAD), tile(MLA_HEADS * HEAD_PAD),
                   tile(MLA_HEADS * V_DIM), prev_tile],
        out_shape=[
            jax.ShapeDtypeStruct((n, D_MODEL), F32),
            jax.ShapeDtypeStruct((n, MLA_HEADS * HEAD_PAD), BF16),
            jax.ShapeDtypeStruct((n, MLA_HEADS * HEAD_PAD), BF16),
            jax.ShapeDtypeStruct((n, MLA_HEADS * V_DIM), BF16),
            jax.ShapeDtypeStruct((n, HG_WIDTH), BF16),
        ],
        scratch_shapes=[pltpu.VMEM((tm, HG_WIDTH), BF16), pltpu.VMEM((tm, HG_WIDTH), F32),
                        pltpu.VMEM((tm, HG_WIDTH), BF16), pltpu.VMEM((tm, HG_WIDTH), BF16),
                        pltpu.VMEM((HG_HEADS, HG_DIM, HG_DIM), F32)],
        compiler_params=pltpu.CompilerParams(dimension_semantics=("arbitrary",), vmem_limit_bytes=VMEM_LIMIT),
    )(x2, pos2, invf, row(ln_ffn1[0]), wg1, wu1, wd1, row(ln_mix[0]), win,
      row(q_a_norm[0]), wq, row(kv_a_norm[0]), wk, wv, hg_lb_logits, hg_out_norm[0])

    head_spec = lambda width: pl.BlockSpec((seq, width), lambda b, h: (b, h))
    o_mla = pl.pallas_call(
        functools.partial(_attn_kernel, tq=tq),
        name="mla_attn",
        grid=(bsz, MLA_HEADS),
        in_specs=[head_spec(HEAD_PAD), head_spec(HEAD_PAD), head_spec(V_DIM)],
        out_specs=head_spec(V_DIM),
        out_shape=jax.ShapeDtypeStruct((n, MLA_HEADS * V_DIM), BF16),
        scratch_shapes=[pltpu.VMEM((seq, 2 * V_DIM), BF16)],
        compiler_params=pltpu.CompilerParams(dimension_semantics=("parallel", "parallel"),
                                             vmem_limit_bytes=VMEM_LIMIT),
    )(qp, kp, vv)

    y = pl.pallas_call(
        _postmix_kernel,
        name="postmix",
        grid=(steps,),
        in_specs=[
            _rows(tm, D_MODEL), _rows(tm, HG_WIDTH), _rows(tm, MLA_HEADS * V_DIM), _rows(tm, PLE_DIM),
            _resident((HG_WIDTH + MLA_HEADS * V_DIM, D_MODEL)), _resident((1, D_MODEL)),
            _resident((D_MODEL, D_FF)), _resident((D_MODEL, D_FF)), _resident((D_FF, D_MODEL)),
            _resident((1, D_MODEL)), _resident((D_MODEL, D_MODEL)), _resident((PLE_DIM, D_MODEL)),
            _resident((1, D_MODEL)),
        ],
        out_specs=_rows(tm, D_MODEL),
        out_shape=jax.ShapeDtypeStruct((n, D_MODEL), F32),
        compiler_params=pltpu.CompilerParams(dimension_semantics=("parallel",), vmem_limit_bytes=VMEM_LIMIT),
    )(h1, o_hg, o_mla, p[0].reshape(n, PLE_DIM), wo, row(ln_ffn2[0]),
      wg2, wu2, wd2, row(ln_ple[0]), w_ple_gate[0].astype(BF16), w_ple_proj[0].astype(BF16), row(ln_final))

    return y.reshape(bsz, seq, D_MODEL)
```

```python
import functools

import jax
import jax.numpy as jnp
from jax import lax
from jax.experimental import pallas as pl
from jax.experimental.pallas import tpu as pltpu

F32 = jnp.float32
BF16 = jnp.bfloat16

D_MODEL = 1024
D_FF = 2816
PLE_DIM = 256
HG_HEADS = 4
HG_DIM = 128
HG_WIDTH = HG_HEADS * HG_DIM
MLA_HEADS = 4
Q_LORA = 256
KV_LORA = 128
NOPE_DIM = 128
ROPE_DIM = 64
V_DIM = 128
QK_DIM = NOPE_DIM + ROPE_DIM
ROPE_THETA = 10000.0
EPS = 1e-6
LOG2_E = 1.4426950408889634

LANES = 128
SUBLANES = 8
HEAD_PAD = 2 * LANES
IN_WIDTH = 4 * HG_WIDTH + Q_LORA + KV_LORA + ROPE_DIM

FFN_CHUNK = 256
TOKEN_TILE = 512
HGRN_CHUNK = 256
ATTN_BLOCK = 256
ATTN_LOOKAHEAD = 3
HGRN_AFTER_FFN_CHUNKS = (1, 3, 5, 7, 9, 10)
VMEM_LIMIT = 56 * 1024 * 1024

NT_DIMS = (((1,), (1,)), ((), ()))
TN_DIMS = (((0,), (0,)), ((), ()))


def _dot(a, b):
    return jnp.dot(a, b, preferred_element_type=F32)


def _rms(x, g):
    return x * lax.rsqrt(jnp.mean(x * x, axis=-1, keepdims=True) + EPS) * g


def _swiglu(xn, wg_ref, wu_ref, wd_ref, between=None):
    acc = None
    for idx, c in enumerate(range(0, D_FF, FFN_CHUNK)):
        g = _dot(xn, wg_ref[:, c:c + FFN_CHUNK])
        u = _dot(xn, wu_ref[:, c:c + FFN_CHUNK])
        a = (g * jax.nn.sigmoid(g) * u).astype(BF16)
        d = _dot(a, wd_ref[c:c + FFN_CHUNK, :])
        acc = d if acc is None else acc + d
        for piece in (between or {}).get(idx, ()):
            piece()
    return acc


def _rope(x, cos_t, sin_lo, sin_hi):
    return (x * cos_t + pltpu.roll(x, LANES - ROPE_DIM // 2, 1) * sin_lo
            + pltpu.roll(x, ROPE_DIM // 2, 1) * sin_hi)


def _premix_kernel(x_ref, pos_ref, invf_ref, g1_ref, wg_ref, wu_ref, wd_ref, gmix_ref, win_ref,
                   qan_ref, wq_ref, kvan_ref, wk_ref, wv_ref, lbl_ref, gn_ref,
                   h1_ref, qp_ref, kp_ref, v_ref, ohg_ref,
                   hq_s, hf_s, hi_s, hg_s, st_ref, *, tiles_per_row, chunk):
    i = pl.program_id(0)
    tm = x_ref.shape[0]

    @pl.when(i == 0)
    def _():
        hq_s[...] = jnp.zeros_like(hq_s)
        hf_s[...] = jnp.zeros_like(hf_s)
        hi_s[...] = jnp.zeros_like(hi_s)
        hg_s[...] = jnp.zeros_like(hg_s)
        st_ref[...] = jnp.zeros_like(st_ref)

    @pl.when(lax.rem(i + tiles_per_row - 1, tiles_per_row) == 0)
    def _():
        st_ref[...] = jnp.zeros_like(st_ref)

    pieces = []
    for r0 in range(0, tm, chunk):
        pieces += _hgrn_pieces(lbl_ref, hq_s, hf_s, hi_s, hg_s, gn_ref, ohg_ref, st_ref, r0, chunk)
    early_chunks = [c for c in HGRN_AFTER_FFN_CHUNKS if c < D_FF // FFN_CHUNK][:len(pieces)]
    early = {c: [piece] for c, piece in zip(early_chunks, pieces)}
    late = iter(pieces[len(early_chunks):])

    ang = pos_ref[...].astype(F32) * invf_ref[...]
    lane = lax.broadcasted_iota(jnp.int32, ang.shape, 1)
    cos_a, sin_a = jnp.cos(ang), jnp.sin(ang)
    half = ROPE_DIM // 2
    cos_t = jnp.where(lane < ROPE_DIM, cos_a, 0.0)
    sin_lo = jnp.where(lane < half, -sin_a, 0.0)
    sin_hi = jnp.where((lane >= half) & (lane < ROPE_DIM), sin_a, 0.0)

    x = x_ref[...]
    xn = _rms(x, g1_ref[...]).astype(BF16)
    h1 = x + 0.5 * _swiglu(xn, wg_ref, wu_ref, wd_ref, early)
    h1_ref[...] = h1

    hn = _rms(h1, gmix_ref[...]).astype(BF16)
    w = HG_WIDTH
    new = []
    for j in range(4):
        new.append(_dot(hn, win_ref[:, j * w:(j + 1) * w]))
        piece = next(late, None)
        if piece is not None:
            piece()
    for piece in late:
        piece()
    c0 = 4 * w
    cq = _dot(hn, win_ref[:, c0:c0 + Q_LORA])
    ckv_kr = _dot(hn, win_ref[:, c0 + Q_LORA:IN_WIDTH])
    ckv = ckv_kr[:, 0:KV_LORA]
    kr = jnp.concatenate([ckv_kr[:, KV_LORA:KV_LORA + ROPE_DIM],
                          jnp.zeros((tm, LANES - ROPE_DIM), F32)], axis=1)

    q = _dot(_rms(cq, qan_ref[...]).astype(BF16), wq_ref[...]) * (QK_DIM ** -0.5 * LOG2_E)
    ckvn = _rms(ckv, kvan_ref[...]).astype(BF16)
    k_nope = _dot(ckvn, wk_ref[...])
    v_ref[...] = _dot(ckvn, wv_ref[...]).astype(v_ref.dtype)
    k_rope = _rope(kr, cos_t, sin_lo, sin_hi).astype(kp_ref.dtype)
    for h in range(MLA_HEADS):
        a = h * HEAD_PAD
        qp_ref[:, a:a + NOPE_DIM] = q[:, a:a + NOPE_DIM].astype(qp_ref.dtype)
        qp_ref[:, a + NOPE_DIM:a + HEAD_PAD] = _rope(
            q[:, a + NOPE_DIM:a + HEAD_PAD], cos_t, sin_lo, sin_hi).astype(qp_ref.dtype)
        kp_ref[:, a:a + NOPE_DIM] = k_nope[:, h * NOPE_DIM:(h + 1) * NOPE_DIM].astype(kp_ref.dtype)
        kp_ref[:, a + NOPE_DIM:a + HEAD_PAD] = k_rope

    hq_s[...] = new[0].astype(hq_s.dtype)
    hf_s[...] = new[1]
    hi_s[...] = new[2].astype(hi_s.dtype)
    hg_s[...] = new[3].astype(hg_s.dtype)


def _neg_gap(b, m):
    t = b.shape[0]
    if m >= SUBLANES:
        pieces = []
        for s in range(0, t, 2 * m):
            mid = b[s + m - 1:s + m, :]
            pieces += [mid - b[s:s + m], b[s + m:s + 2 * m] - mid]
        return jnp.concatenate(pieces, axis=0)
    b3 = b.reshape(t // SUBLANES, SUBLANES, LANES)
    sub = lax.broadcasted_iota(jnp.int32, b3.shape, 1)

    def row(i):
        return jnp.broadcast_to(b3[:, i:i + 1, :], b3.shape)

    mid = row(m - 1)
    for s in range(2 * m, SUBLANES, 2 * m):
        mid = jnp.where(sub >= s, row(s + m - 1), mid)
    return -jnp.abs(b3 - mid).reshape(t, LANES)


def _hgrn_pieces(lbl_ref, hq_ref, hf_ref, hi_ref, hg_ref, gn_ref, o_ref, st_ref, r0, t):
    rows = slice(r0, r0 + t)
    half = t // 2
    ctx = {}

    def gates():
        lg = lbl_ref[...]
        e = jnp.exp(lg - jnp.max(lg, axis=0, keepdims=True))
        lb = e[0:1, :] / jnp.sum(e, axis=0, keepdims=True)
        f_raw = hf_ref[rows, :]
        f = lb + (1.0 - lb) * jax.nn.sigmoid(f_raw)
        g = jnp.log(f)
        ctx["f"] = f
        ctx["kk"] = (1.0 - lb) * jax.nn.sigmoid(-f_raw)

        row = lax.broadcasted_iota(jnp.int32, (t, t), 0)
        col = lax.broadcasted_iota(jnp.int32, (t, t), 1)
        tri = (row >= col).astype(BF16)
        g1 = g.astype(BF16)
        r1 = g - g1.astype(F32)
        g2 = r1.astype(BF16)
        g3 = (r1 - g2.astype(F32)).astype(BF16)
        ctx["b"] = (_dot(tri, g1) + _dot(tri, g2) + _dot(tri, g3)) * LOG2_E

    def head(h):
        ri = lax.broadcasted_iota(jnp.int32, (half, half), 0)
        ci = lax.broadcasted_iota(jnp.int32, (half, half), 1)
        xm = jnp.where(ri > ci, ri ^ ci, 0)
        levels = [1 << i for i in range(half.bit_length() - 1)]
        masks = [ri == ci] + [(xm >= m) & (xm < 2 * m) for m in levels]
        odd_row = (lax.broadcasted_iota(jnp.int32, (t, HG_DIM), 0) & 1) == 1

        sl = slice(h * HG_DIM, (h + 1) * HG_DIM)
        q = hq_ref[rows, sl]
        v = hi_ref[rows, sl]
        kb = ctx["kk"][:, sl].astype(BF16)
        b = ctx["b"][:, sl]

        qk = [(q, kb)]
        for m in levels:
            if m == 1:
                decay = jnp.where(odd_row, ctx["f"][:, sl], 1.0).astype(BF16)
            else:
                decay = jnp.exp2(_neg_gap(b, m)).astype(BF16)
            qk.append((q * decay, kb * decay))
        diag = []
        for q0 in (0, half):
            p = None
            for (q_l, k_l), mask in zip(qk, masks):
                s_l = lax.dot_general(q_l[q0:q0 + half], k_l[q0:q0 + half], NT_DIMS, preferred_element_type=F32)
                p = jnp.where(mask, s_l, 0.0 if p is None else p)
            diag.append(p.astype(BF16))
        mid = b[half - 1:half, :]
        cross = lax.dot_general(q[half:] * jnp.exp2(b[half:] - mid).astype(BF16),
                                kb[:half] * jnp.exp2(mid - b[:half]).astype(BF16), NT_DIMS,
                                preferred_element_type=F32).astype(BF16)
        o = jnp.concatenate([_dot(diag[0], v[:half]),
                             _dot(jnp.concatenate([cross, diag[1]], axis=1), v)], axis=0)

        st = st_ref[h]
        b_last = b[t - 1:t, :]
        o = o + lax.dot_general(q * jnp.exp2(b).astype(BF16), st.astype(BF16), NT_DIMS,
                                preferred_element_type=F32)
        k_end = kb * jnp.exp2(b_last - b).astype(BF16)
        st_ref[h] = st * jnp.exp2(b_last) + lax.dot_general(v, k_end, TN_DIMS, preferred_element_type=F32)

        gate = hg_ref[rows, sl].astype(F32)
        o_ref[rows, sl] = (_rms(o, gn_ref[h:h + 1, :]) * (gate * jax.nn.sigmoid(gate))).astype(o_ref.dtype)

    return [gates] + [functools.partial(head, h) for h in range(HG_HEADS)]


def _attn_kernel(q_ref, k_ref, v_ref, o_ref, vaug_ref, *, tq):
    seq = q_ref.shape[0]
    for h in range(MLA_HEADS):
        vaug_ref[h, :, 0:V_DIM] = v_ref[:, h * V_DIM:(h + 1) * V_DIM]
        vaug_ref[h, :, V_DIM:2 * V_DIM] = jnp.ones((seq, V_DIM), vaug_ref.dtype)
    row = lax.broadcasted_iota(jnp.int32, (tq, tq), 0)
    col = lax.broadcasted_iota(jnp.int32, (tq, tq), 1)

    def scores(job):
        h, i = job
        lo = i * tq
        cols = slice(h * HEAD_PAD, (h + 1) * HEAD_PAD)
        q = q_ref[lo:lo + tq, cols]
        s_d = lax.dot_general(q, k_ref[lo:lo + tq, cols], NT_DIMS, preferred_element_type=F32)
        s_d = jnp.where(row >= col, s_d, -jnp.inf)
        s_p = lax.dot_general(q, k_ref[0:lo, cols], NT_DIMS, preferred_element_type=F32) if i > 0 else None
        return s_d, s_p

    jobs = [(h, i) for i in range(seq // tq) for h in range(MLA_HEADS)]
    pending = [scores(job) for job in jobs[:ATTN_LOOKAHEAD]]
    for n_done, (h, i) in enumerate(jobs):
        lo = i * tq
        s_d, s_p = pending.pop(0)
        if n_done + ATTN_LOOKAHEAD < len(jobs):
            pending.append(scores(jobs[n_done + ATTN_LOOKAHEAD]))
        m = jnp.max(s_d, axis=-1, keepdims=True)
        if i > 0:
            m = jnp.maximum(m, jnp.max(s_p, axis=-1, keepdims=True))
        acc = _dot(jnp.exp2(s_d - m).astype(BF16), vaug_ref[h, lo:lo + tq, :])
        if i > 0:
            acc += _dot(jnp.exp2(s_p - m).astype(BF16), vaug_ref[h, 0:lo, :])
        o_ref[lo:lo + tq, h * V_DIM:(h + 1) * V_DIM] = (
            acc[:, 0:V_DIM] / acc[:, V_DIM:2 * V_DIM]).astype(o_ref.dtype)


def _postmix_kernel(h1_ref, ohg_ref, omla_ref, p_ref, wo_ref, g2_ref, wg_ref, wu_ref, wd_ref,
                    gple_ref, wpg_ref, wpp_ref, gfin_ref, y_ref):
    h2 = (h1_ref[...] + _dot(ohg_ref[...], wo_ref[0:HG_WIDTH, :])
          + _dot(omla_ref[...], wo_ref[HG_WIDTH:HG_WIDTH + MLA_HEADS * V_DIM, :]))
    h3 = h2 + 0.5 * _swiglu(_rms(h2, g2_ref[...]).astype(BF16), wg_ref, wu_ref, wd_ref)
    gate = jax.nn.sigmoid(_dot(_rms(h3, gple_ref[...]).astype(BF16), wpg_ref[...]))
    h4 = h3 + gate * _dot(p_ref[...].astype(BF16), wpp_ref[...])
    y_ref[...] = _rms(h4, gfin_ref[...])


def _resident(shape):
    return pl.BlockSpec(shape, lambda *_: (0,) * len(shape), pipeline_mode=pl.Buffered(1))


def _rows(tile, width):
    return pl.BlockSpec((tile, width), lambda i: (i, 0))


def kernel(x, p, positions, ln_ffn1, w1_gate, w1_up, w1_down, ln_mix, w_in, hg_lb_logits, hg_out_norm,
           q_a_norm, w_q_up, kv_a_norm, w_kv_up, w_out, ln_ffn2, w2_gate, w2_up, w2_down, ln_ple,
           w_ple_gate, w_ple_proj, ln_final):
    bsz, seq, _ = x.shape
    assert p.shape[0] == 1 and hg_lb_logits.shape[0] == 2, "single-layer trunk"
    n = bsz * seq
    tm = min(TOKEN_TILE, seq)
    tc = min(HGRN_CHUNK, seq)
    tq = min(ATTN_BLOCK, seq)
    assert seq % tm == 0 and tm % tc == 0 and seq % tq == 0

    x2 = x.reshape(n, D_MODEL)
    pos2 = positions.reshape(n, 1)
    half = ROPE_DIM // 2
    inv_freq = ROPE_THETA ** (-jnp.arange(half, dtype=F32) / half)
    invf = jnp.concatenate([inv_freq, inv_freq, jnp.zeros((LANES - ROPE_DIM,), F32)]).reshape(1, LANES)
    wg1, wu1, wd1 = w1_gate[0].astype(BF16), w1_up[0].astype(BF16), w1_down[0].astype(BF16)
    wg2, wu2, wd2 = w2_gate[0].astype(BF16), w2_up[0].astype(BF16), w2_down[0].astype(BF16)
    win = w_in[0].astype(BF16)
    wq = jnp.pad(w_q_up[0].astype(BF16).reshape(Q_LORA, MLA_HEADS, QK_DIM),
                 ((0, 0), (0, 0), (0, HEAD_PAD - QK_DIM))).reshape(Q_LORA, MLA_HEADS * HEAD_PAD)
    wkv = w_kv_up[0].astype(BF16).reshape(KV_LORA, MLA_HEADS, NOPE_DIM + V_DIM)
    wk = wkv[:, :, :NOPE_DIM].reshape(KV_LORA, MLA_HEADS * NOPE_DIM)
    wv = wkv[:, :, NOPE_DIM:].reshape(KV_LORA, MLA_HEADS * V_DIM)
    wo = w_out[0].astype(BF16)
    row = lambda a: a.reshape(1, -1)

    steps = n // tm
    tile = lambda width: pl.BlockSpec((tm, width), lambda i: (jnp.minimum(i, steps - 1), 0))
    prev_tile = pl.BlockSpec((tm, HG_WIDTH), lambda i: (jnp.maximum(i - 1, 0), 0))
    h1, qp, kp, vv, o_hg = pl.pallas_call(
        functools.partial(_premix_kernel, tiles_per_row=seq // tm, chunk=tc),
        name="premix",
        grid=(steps + 1,),
        in_specs=[
            tile(D_MODEL), tile(1), _resident((1, LANES)), _resident((1, D_MODEL)),
            _resident((D_MODEL, D_FF)), _resident((D_MODEL, D_FF)), _resident((D_FF, D_MODEL)),
            _resident((1, D_MODEL)), _resident((D_MODEL, IN_WIDTH)),
            _resident((1, Q_LORA)), _resident((Q_LORA, MLA_HEADS * HEAD_PAD)), _resident((1, KV_LORA)),
            _resident((KV_LORA, MLA_HEADS * NOPE_DIM)), _resident((KV_LORA, MLA_HEADS * V_DIM)),
            _resident((2, HG_WIDTH)), _resident((HG_HEADS, HG_DIM)),
        ],
        out_specs=[tile(D_MODEL), tile(MLA_HEADS * HEAD_PAD), tile(MLA_HEADS * HEAD_PAD),
                   tile(MLA_HEADS * V_DIM), prev_tile],
        out_shape=[
            jax.ShapeDtypeStruct((n, D_MODEL), F32),
            jax.ShapeDtypeStruct((n, MLA_HEADS * HEAD_PAD), BF16),
            jax.ShapeDtypeStruct((n, MLA_HEADS * HEAD_PAD), BF16),
            jax.ShapeDtypeStruct((n, MLA_HEADS * V_DIM), BF16),
            jax.ShapeDtypeStruct((n, HG_WIDTH), BF16),
        ],
        scratch_shapes=[pltpu.VMEM((tm, HG_WIDTH), BF16), pltpu.VMEM((tm, HG_WIDTH), F32),
                        pltpu.VMEM((tm, HG_WIDTH), BF16), pltpu.VMEM((tm, HG_WIDTH), BF16),
                        pltpu.VMEM((HG_HEADS, HG_DIM, HG_DIM), F32)],
        compiler_params=pltpu.CompilerParams(dimension_semantics=("arbitrary",), vmem_limit_bytes=VMEM_LIMIT),
    )(x2, pos2, invf, row(ln_ffn1[0]), wg1, wu1, wd1, row(ln_mix[0]), win,
      row(q_a_norm[0]), wq, row(kv_a_norm[0]), wk, wv, hg_lb_logits, hg_out_norm[0])

    seq_spec = lambda width: pl.BlockSpec((seq, MLA_HEADS * width), lambda b: (b, 0))
    o_mla = pl.pallas_call(
        functools.partial(_attn_kernel, tq=tq),
        name="mla_attn",
        grid=(bsz,),
        in_specs=[seq_spec(HEAD_PAD), seq_spec(HEAD_PAD), seq_spec(V_DIM)],
        out_specs=seq_spec(V_DIM),
        out_shape=jax.ShapeDtypeStruct((n, MLA_HEADS * V_DIM), BF16),
        scratch_shapes=[pltpu.VMEM((MLA_HEADS, seq, 2 * V_DIM), BF16)],
        compiler_params=pltpu.CompilerParams(dimension_semantics=("parallel",), vmem_limit_bytes=VMEM_LIMIT),
    )(qp, kp, vv)

    y = pl.pallas_call(
        _postmix_kernel,
        name="postmix",
        grid=(steps,),
        in_specs=[
            _rows(tm, D_MODEL), _rows(tm, HG_WIDTH), _rows(tm, MLA_HEADS * V_DIM), _rows(tm, PLE_DIM),
            _resident((HG_WIDTH + MLA_HEADS * V_DIM, D_MODEL)), _resident((1, D_MODEL)),
            _resident((D_MODEL, D_FF)), _resident((D_MODEL, D_FF)), _resident((D_FF, D_MODEL)),
            _resident((1, D_MODEL)), _resident((D_MODEL, D_MODEL)), _resident((PLE_DIM, D_MODEL)),
            _resident((1, D_MODEL)),
        ],
        out_specs=_rows(tm, D_MODEL),
        out_shape=jax.ShapeDtypeStruct((n, D_MODEL), F32),
        compiler_params=pltpu.CompilerParams(dimension_semantics=("parallel",), vmem_limit_bytes=VMEM_LIMIT),
    )(h1, o_hg, o_mla, p[0].reshape(n, PLE_DIM), wo, row(ln_ffn2[0]),
      wg2, wu2, wd2, row(ln_ple[0]), w_ple_gate[0].astype(BF16), w_ple_proj[0].astype(BF16), row(ln_final))

    return y.reshape(bsz, seq, D_MODEL)
```

```python
import functools

import jax
import jax.numpy as jnp
from jax import lax
from jax.experimental import pallas as pl
from jax.experimental.pallas import tpu as pltpu

F32 = jnp.float32
BF16 = jnp.bfloat16

D_MODEL = 1024
D_FF = 2816
PLE_DIM = 256
HG_HEADS = 4
HG_DIM = 128
HG_WIDTH = HG_HEADS * HG_DIM
MLA_HEADS = 4
Q_LORA = 256
KV_LORA = 128
NOPE_DIM = 128
ROPE_DIM = 64
V_DIM = 128
QK_DIM = NOPE_DIM + ROPE_DIM
ROPE_THETA = 10000.0
EPS = 1e-6
LOG2_E = 1.4426950408889634

LANES = 128
SUBLANES = 8
HEAD_PAD = 2 * LANES
IN_WIDTH = 4 * HG_WIDTH + Q_LORA + KV_LORA + ROPE_DIM

FFN_CHUNK = 256
TOKEN_TILE = 512
HGRN_CHUNK = 256
ATTN_BLOCK = 256
ATTN_LOOKAHEAD = 3
HGRN_AFTER_FFN_CHUNKS = (1, 3, 5, 7, 9, 10)
STAGE_ROWS_WIDE = 64
STAGE_ROWS_TALL = 128
VMEM_LIMIT = 56 * 1024 * 1024

NT_DIMS = (((1,), (1,)), ((), ()))
TN_DIMS = (((0,), (0,)), ((), ()))


def _dot(a, b):
    return jnp.dot(a, b, preferred_element_type=F32)


def _rms(x, g):
    return x * lax.rsqrt(jnp.mean(x * x, axis=-1, keepdims=True) + EPS) * g


def _swiglu(xn, wg_ref, wu_ref, wd_ref, between=None):
    acc = None
    for idx, c in enumerate(range(0, D_FF, FFN_CHUNK)):
        g = _dot(xn, wg_ref[:, c:c + FFN_CHUNK])
        u = _dot(xn, wu_ref[:, c:c + FFN_CHUNK])
        a = (g * jax.nn.sigmoid(g) * u).astype(BF16)
        d = _dot(a, wd_ref[c:c + FFN_CHUNK, :])
        acc = d if acc is None else acc + d
        for piece in (between or {}).get(idx, ()):
            piece()
    return acc


def _stage_bf16(src_ref, dst_ref, stage_ref, sems, sem_base, chunk_rows):
    rows = src_ref.shape[0]
    assert rows % chunk_rows == 0 and stage_ref.shape[1:] == (chunk_rows, src_ref.shape[1])

    def copy(j):
        return pltpu.make_async_copy(src_ref.at[j * chunk_rows:(j + 1) * chunk_rows, :],
                                     stage_ref.at[j % 2], sems.at[sem_base + j % 2])

    n_chunks = rows // chunk_rows
    copy(0).start()
    for j in range(n_chunks):
        if j + 1 < n_chunks:
            copy(j + 1).start()
        copy(j).wait()
        dst_ref[j * chunk_rows:(j + 1) * chunk_rows, :] = stage_ref[j % 2].astype(dst_ref.dtype)


def _stage_ffn_weights(wg_hbm, wu_hbm, wd_hbm, wg_ref, wu_ref, wd_ref, wide_stage, tall_stage, sems):
    _stage_bf16(wg_hbm, wg_ref, wide_stage, sems, 0, STAGE_ROWS_WIDE)
    _stage_bf16(wu_hbm, wu_ref, wide_stage, sems, 0, STAGE_ROWS_WIDE)
    _stage_bf16(wd_hbm, wd_ref, tall_stage, sems, 2, STAGE_ROWS_TALL)


def _ffn_weight_scratch():
    return [pltpu.VMEM((D_MODEL, D_FF), BF16), pltpu.VMEM((D_MODEL, D_FF), BF16), pltpu.VMEM((D_FF, D_MODEL), BF16),
            pltpu.VMEM((2, STAGE_ROWS_WIDE, D_FF), F32), pltpu.VMEM((2, STAGE_ROWS_TALL, D_MODEL), F32),
            pltpu.SemaphoreType.DMA((4,))]


def _rope(x, cos_t, sin_lo, sin_hi):
    return (x * cos_t + pltpu.roll(x, LANES - ROPE_DIM // 2, 1) * sin_lo
            + pltpu.roll(x, ROPE_DIM // 2, 1) * sin_hi)


def _premix_kernel(x_ref, pos_ref, invf_ref, g1_ref, wg_hbm, wu_hbm, wd_hbm, gmix_ref, win_ref,
                   qan_ref, wq_ref, kvan_ref, wk_ref, wv_ref, lbl_ref, gn_ref,
                   h1_ref, qp_ref, kp_ref, v_ref, ohg_ref,
                   hq_s, hf_s, hi_s, hg_s, st_ref, wg_ref, wu_ref, wd_ref, wide_stage, tall_stage, sems,
                   *, tiles_per_row, chunk):
    i = pl.program_id(0)
    tm = x_ref.shape[0]

    @pl.when(i == 0)
    def _():
        _stage_ffn_weights(wg_hbm, wu_hbm, wd_hbm, wg_ref, wu_ref, wd_ref, wide_stage, tall_stage, sems)
        hq_s[...] = jnp.zeros_like(hq_s)
        hf_s[...] = jnp.zeros_like(hf_s)
        hi_s[...] = jnp.zeros_like(hi_s)
        hg_s[...] = jnp.zeros_like(hg_s)
        st_ref[...] = jnp.zeros_like(st_ref)

    @pl.when(lax.rem(i + tiles_per_row - 1, tiles_per_row) == 0)
    def _():
        st_ref[...] = jnp.zeros_like(st_ref)

    pieces = []
    for r0 in range(0, tm, chunk):
        pieces += _hgrn_pieces(lbl_ref, hq_s, hf_s, hi_s, hg_s, gn_ref, ohg_ref, st_ref, r0, chunk)
    early_chunks = [c for c in HGRN_AFTER_FFN_CHUNKS if c < D_FF // FFN_CHUNK][:len(pieces)]
    early = {c: [piece] for c, piece in zip(early_chunks, pieces)}
    late = iter(pieces[len(early_chunks):])

    ang = pos_ref[...].astype(F32) * invf_ref[...]
    lane = lax.broadcasted_iota(jnp.int32, ang.shape, 1)
    cos_a, sin_a = jnp.cos(ang), jnp.sin(ang)
    half = ROPE_DIM // 2
    cos_t = jnp.where(lane < ROPE_DIM, cos_a, 0.0)
    sin_lo = jnp.where(lane < half, -sin_a, 0.0)
    sin_hi = jnp.where((lane >= half) & (lane < ROPE_DIM), sin_a, 0.0)

    x = x_ref[...]
    xn = _rms(x, g1_ref[...]).astype(BF16)
    h1 = x + 0.5 * _swiglu(xn, wg_ref, wu_ref, wd_ref, early)
    h1_ref[...] = h1

    hn = _rms(h1, gmix_ref[...]).astype(BF16)
    w = HG_WIDTH
    new = []
    for j in range(4):
        new.append(_dot(hn, win_ref[:, j * w:(j + 1) * w]))
        piece = next(late, None)
        if piece is not None:
            piece()
    for piece in late:
        piece()
    c0 = 4 * w
    cq = _dot(hn, win_ref[:, c0:c0 + Q_LORA])
    ckv_kr = _dot(hn, win_ref[:, c0 + Q_LORA:IN_WIDTH])
    ckv = ckv_kr[:, 0:KV_LORA]
    kr = jnp.concatenate([ckv_kr[:, KV_LORA:KV_LORA + ROPE_DIM],
                          jnp.zeros((tm, LANES - ROPE_DIM), F32)], axis=1)

    q = _dot(_rms(cq, qan_ref[...]).astype(BF16), wq_ref[...]) * (QK_DIM ** -0.5 * LOG2_E)
    ckvn = _rms(ckv, kvan_ref[...]).astype(BF16)
    k_nope = _dot(ckvn, wk_ref[...])
    v_ref[...] = _dot(ckvn, wv_ref[...]).astype(v_ref.dtype)
    k_rope = _rope(kr, cos_t, sin_lo, sin_hi).astype(kp_ref.dtype)
    for h in range(MLA_HEADS):
        a = h * HEAD_PAD
        qp_ref[:, a:a + NOPE_DIM] = q[:, a:a + NOPE_DIM].astype(qp_ref.dtype)
        qp_ref[:, a + NOPE_DIM:a + HEAD_PAD] = _rope(
            q[:, a + NOPE_DIM:a + HEAD_PAD], cos_t, sin_lo, sin_hi).astype(qp_ref.dtype)
        kp_ref[:, a:a + NOPE_DIM] = k_nope[:, h * NOPE_DIM:(h + 1) * NOPE_DIM].astype(kp_ref.dtype)
        kp_ref[:, a + NOPE_DIM:a + HEAD_PAD] = k_rope

    hq_s[...] = new[0].astype(hq_s.dtype)
    hf_s[...] = new[1]
    hi_s[...] = new[2].astype(hi_s.dtype)
    hg_s[...] = new[3].astype(hg_s.dtype)


def _neg_gap(b, m):
    t = b.shape[0]
    if m >= SUBLANES:
        pieces = []
        for s in range(0, t, 2 * m):
            mid = b[s + m - 1:s + m, :]
            pieces += [mid - b[s:s + m], b[s + m:s + 2 * m] - mid]
        return jnp.concatenate(pieces, axis=0)
    b3 = b.reshape(t // SUBLANES, SUBLANES, LANES)
    sub = lax.broadcasted_iota(jnp.int32, b3.shape, 1)

    def row(i):
        return jnp.broadcast_to(b3[:, i:i + 1, :], b3.shape)

    mid = row(m - 1)
    for s in range(2 * m, SUBLANES, 2 * m):
        mid = jnp.where(sub >= s, row(s + m - 1), mid)
    return -jnp.abs(b3 - mid).reshape(t, LANES)


def _hgrn_pieces(lbl_ref, hq_ref, hf_ref, hi_ref, hg_ref, gn_ref, o_ref, st_ref, r0, t):
    rows = slice(r0, r0 + t)
    half = t // 2
    ctx = {}

    def gates():
        lg = lbl_ref[...]
        e = jnp.exp(lg - jnp.max(lg, axis=0, keepdims=True))
        lb = e[0:1, :] / jnp.sum(e, axis=0, keepdims=True)
        f_raw = hf_ref[rows, :]
        f = lb + (1.0 - lb) * jax.nn.sigmoid(f_raw)
        g = jnp.log(f)
        ctx["f"] = f
        ctx["kk"] = (1.0 - lb) * jax.nn.sigmoid(-f_raw)

        row = lax.broadcasted_iota(jnp.int32, (t, t), 0)
        col = lax.broadcasted_iota(jnp.int32, (t, t), 1)
        tri = (row >= col).astype(BF16)
        g1 = g.astype(BF16)
        r1 = g - g1.astype(F32)
        g2 = r1.astype(BF16)
        g3 = (r1 - g2.astype(F32)).astype(BF16)
        ctx["b"] = (_dot(tri, g1) + _dot(tri, g2) + _dot(tri, g3)) * LOG2_E

    def head(h):
        ri = lax.broadcasted_iota(jnp.int32, (half, half), 0)
        ci = lax.broadcasted_iota(jnp.int32, (half, half), 1)
        xm = jnp.where(ri > ci, ri ^ ci, 0)
        levels = [1 << i for i in range(half.bit_length() - 1)]
        masks = [ri == ci] + [(xm >= m) & (xm < 2 * m) for m in levels]
        odd_row = (lax.broadcasted_iota(jnp.int32, (t, HG_DIM), 0) & 1) == 1

        sl = slice(h * HG_DIM, (h + 1) * HG_DIM)
        q = hq_ref[rows, sl]
        v = hi_ref[rows, sl]
        kb = ctx["kk"][:, sl].astype(BF16)
        b = ctx["b"][:, sl]

        qk = [(q, kb)]
        for m in levels:
            if m == 1:
                decay = jnp.where(odd_row, ctx["f"][:, sl], 1.0).astype(BF16)
            else:
                decay = jnp.exp2(_neg_gap(b, m)).astype(BF16)
            qk.append((q * decay, kb * decay))
        diag = []
        for q0 in (0, half):
            p = None
            for (q_l, k_l), mask in zip(qk, masks):
                s_l = lax.dot_general(q_l[q0:q0 + half], k_l[q0:q0 + half], NT_DIMS, preferred_element_type=F32)
                p = jnp.where(mask, s_l, 0.0 if p is None else p)
            diag.append(p.astype(BF16))
        mid = b[half - 1:half, :]
        cross = lax.dot_general(q[half:] * jnp.exp2(b[half:] - mid).astype(BF16),
                                kb[:half] * jnp.exp2(mid - b[:half]).astype(BF16), NT_DIMS,
                                preferred_element_type=F32).astype(BF16)
        o = jnp.concatenate([_dot(diag[0], v[:half]),
                             _dot(jnp.concatenate([cross, diag[1]], axis=1), v)], axis=0)

        st = st_ref[h]
        b_last = b[t - 1:t, :]
        o = o + lax.dot_general(q * jnp.exp2(b).astype(BF16), st.astype(BF16), NT_DIMS,
                                preferred_element_type=F32)
        k_end = kb * jnp.exp2(b_last - b).astype(BF16)
        st_ref[h] = st * jnp.exp2(b_last) + lax.dot_general(v, k_end, TN_DIMS, preferred_element_type=F32)

        gate = hg_ref[rows, sl].astype(F32)
        o_ref[rows, sl] = (_rms(o, gn_ref[h:h + 1, :]) * (gate * jax.nn.sigmoid(gate))).astype(o_ref.dtype)

    return [gates] + [functools.partial(head, h) for h in range(HG_HEADS)]


def _attn_kernel(q_ref, k_ref, v_ref, o_ref, vaug_ref, *, tq):
    seq = q_ref.shape[0]
    for h in range(MLA_HEADS):
        vaug_ref[h, :, 0:V_DIM] = v_ref[:, h * V_DIM:(h + 1) * V_DIM]
        vaug_ref[h, :, V_DIM:2 * V_DIM] = jnp.ones((seq, V_DIM), vaug_ref.dtype)
    row = lax.broadcasted_iota(jnp.int32, (tq, tq), 0)
    col = lax.broadcasted_iota(jnp.int32, (tq, tq), 1)

    def scores(job):
        h, i = job
        lo = i * tq
        cols = slice(h * HEAD_PAD, (h + 1) * HEAD_PAD)
        q = q_ref[lo:lo + tq, cols]
        s_d = lax.dot_general(q, k_ref[lo:lo + tq, cols], NT_DIMS, preferred_element_type=F32)
        s_d = jnp.where(row >= col, s_d, -jnp.inf)
        s_p = lax.dot_general(q, k_ref[0:lo, cols], NT_DIMS, preferred_element_type=F32) if i > 0 else None
        return s_d, s_p

    jobs = [(h, i) for i in range(seq // tq) for h in range(MLA_HEADS)]
    pending = [scores(job) for job in jobs[:ATTN_LOOKAHEAD]]
    for n_done, (h, i) in enumerate(jobs):
        lo = i * tq
        s_d, s_p = pending.pop(0)
        if n_done + ATTN_LOOKAHEAD < len(jobs):
            pending.append(scores(jobs[n_done + ATTN_LOOKAHEAD]))
        m = jnp.max(s_d, axis=-1, keepdims=True)
        if i > 0:
            m = jnp.maximum(m, jnp.max(s_p, axis=-1, keepdims=True))
        acc = _dot(jnp.exp2(s_d - m).astype(BF16), vaug_ref[h, lo:lo + tq, :])
        if i > 0:
            acc += _dot(jnp.exp2(s_p - m).astype(BF16), vaug_ref[h, 0:lo, :])
        o_ref[lo:lo + tq, h * V_DIM:(h + 1) * V_DIM] = (
            acc[:, 0:V_DIM] / acc[:, V_DIM:2 * V_DIM]).astype(o_ref.dtype)


def _postmix_kernel(h1_ref, ohg_ref, omla_ref, p_ref, wo_ref, g2_ref, wg_hbm, wu_hbm, wd_hbm,
                    gple_ref, wpg_ref, wpp_ref, gfin_ref, y_ref,
                    wg_ref, wu_ref, wd_ref, wide_stage, tall_stage, sems):
    @pl.when(pl.program_id(0) == 0)
    def _():
        _stage_ffn_weights(wg_hbm, wu_hbm, wd_hbm, wg_ref, wu_ref, wd_ref, wide_stage, tall_stage, sems)

    h2 = (h1_ref[...] + _dot(ohg_ref[...], wo_ref[0:HG_WIDTH, :])
          + _dot(omla_ref[...], wo_ref[HG_WIDTH:HG_WIDTH + MLA_HEADS * V_DIM, :]))
    h3 = h2 + 0.5 * _swiglu(_rms(h2, g2_ref[...]).astype(BF16), wg_ref, wu_ref, wd_ref)
    gate = jax.nn.sigmoid(_dot(_rms(h3, gple_ref[...]).astype(BF16), wpg_ref[...]))
    h4 = h3 + gate * _dot(p_ref[...].astype(BF16), wpp_ref[...])
    y_ref[...] = _rms(h4, gfin_ref[...])


def _resident(shape):
    return pl.BlockSpec(shape, lambda *_: (0,) * len(shape), pipeline_mode=pl.Buffered(1))


def _rows(tile, width):
    return pl.BlockSpec((tile, width), lambda i: (i, 0))


def kernel(x, p, positions, ln_ffn1, w1_gate, w1_up, w1_down, ln_mix, w_in, hg_lb_logits, hg_out_norm,
           q_a_norm, w_q_up, kv_a_norm, w_kv_up, w_out, ln_ffn2, w2_gate, w2_up, w2_down, ln_ple,
           w_ple_gate, w_ple_proj, ln_final):
    bsz, seq, _ = x.shape
    assert p.shape[0] == 1 and hg_lb_logits.shape[0] == 2, "single-layer trunk"
    n = bsz * seq
    tm = min(TOKEN_TILE, seq)
    tc = min(HGRN_CHUNK, seq)
    tq = min(ATTN_BLOCK, seq)
    assert seq % tm == 0 and tm % tc == 0 and seq % tq == 0

    x2 = x.reshape(n, D_MODEL)
    pos2 = positions.reshape(n, 1)
    half = ROPE_DIM // 2
    inv_freq = ROPE_THETA ** (-jnp.arange(half, dtype=F32) / half)
    invf = jnp.concatenate([inv_freq, inv_freq, jnp.zeros((LANES - ROPE_DIM,), F32)]).reshape(1, LANES)
    win = w_in[0].astype(BF16)
    in_hbm = pl.BlockSpec(memory_space=pl.ANY)
    wq = jnp.pad(w_q_up[0].astype(BF16).reshape(Q_LORA, MLA_HEADS, QK_DIM),
                 ((0, 0), (0, 0), (0, HEAD_PAD - QK_DIM))).reshape(Q_LORA, MLA_HEADS * HEAD_PAD)
    wkv = w_kv_up[0].astype(BF16).reshape(KV_LORA, MLA_HEADS, NOPE_DIM + V_DIM)
    wk = wkv[:, :, :NOPE_DIM].reshape(KV_LORA, MLA_HEADS * NOPE_DIM)
    wv = wkv[:, :, NOPE_DIM:].reshape(KV_LORA, MLA_HEADS * V_DIM)
    wo = w_out[0].astype(BF16)
    row = lambda a: a.reshape(1, -1)

    steps = n // tm
    tile = lambda width: pl.BlockSpec((tm, width), lambda i: (jnp.minimum(i, steps - 1), 0))
    prev_tile = pl.BlockSpec((tm, HG_WIDTH), lambda i: (jnp.maximum(i - 1, 0), 0))
    h1, qp, kp, vv, o_hg = pl.pallas_call(
        functools.partial(_premix_kernel, tiles_per_row=seq // tm, chunk=tc),
        name="premix",
        grid=(steps + 1,),
        in_specs=[
            tile(D_MODEL), tile(1), _resident((1, LANES)), _resident((1, D_MODEL)),
            in_hbm, in_hbm, in_hbm,
            _resident((1, D_MODEL)), _resident((D_MODEL, IN_WIDTH)),
            _resident((1, Q_LORA)), _resident((Q_LORA, MLA_HEADS * HEAD_PAD)), _resident((1, KV_LORA)),
            _resident((KV_LORA, MLA_HEADS * NOPE_DIM)), _resident((KV_LORA, MLA_HEADS * V_DIM)),
            _resident((2, HG_WIDTH)), _resident((HG_HEADS, HG_DIM)),
        ],
        out_specs=[tile(D_MODEL), tile(MLA_HEADS * HEAD_PAD), tile(MLA_HEADS * HEAD_PAD),
                   tile(MLA_HEADS * V_DIM), prev_tile],
        out_shape=[
            jax.ShapeDtypeStruct((n, D_MODEL), F32),
            jax.ShapeDtypeStruct((n, MLA_HEADS * HEAD_PAD), BF16),
            jax.ShapeDtypeStruct((n, MLA_HEADS * HEAD_PAD), BF16),
            jax.ShapeDtypeStruct((n, MLA_HEADS * V_DIM), BF16),
            jax.ShapeDtypeStruct((n, HG_WIDTH), BF16),
        ],
        scratch_shapes=[pltpu.VMEM((tm, HG_WIDTH), BF16), pltpu.VMEM((tm, HG_WIDTH), F32),
                        pltpu.VMEM((tm, HG_WIDTH), BF16), pltpu.VMEM((tm, HG_WIDTH), BF16),
                        pltpu.VMEM((HG_HEADS, HG_DIM, HG_DIM), F32)] + _ffn_weight_scratch(),
        compiler_params=pltpu.CompilerParams(dimension_semantics=("arbitrary",), vmem_limit_bytes=VMEM_LIMIT),
    )(x2, pos2, invf, row(ln_ffn1[0]), w1_gate[0], w1_up[0], w1_down[0], row(ln_mix[0]), win,
      row(q_a_norm[0]), wq, row(kv_a_norm[0]), wk, wv, hg_lb_logits, hg_out_norm[0])

    seq_spec = lambda width: pl.BlockSpec((seq, MLA_HEADS * width), lambda b: (b, 0))
    o_mla = pl.pallas_call(
        functools.partial(_attn_kernel, tq=tq),
        name="mla_attn",
        grid=(bsz,),
        in_specs=[seq_spec(HEAD_PAD), seq_spec(HEAD_PAD), seq_spec(V_DIM)],
        out_specs=seq_spec(V_DIM),
        out_shape=jax.ShapeDtypeStruct((n, MLA_HEADS * V_DIM), BF16),
        scratch_shapes=[pltpu.VMEM((MLA_HEADS, seq, 2 * V_DIM), BF16)],
        compiler_params=pltpu.CompilerParams(dimension_semantics=("parallel",), vmem_limit_bytes=VMEM_LIMIT),
    )(qp, kp, vv)

    y = pl.pallas_call(
        _postmix_kernel,
        name="postmix",
        grid=(steps,),
        in_specs=[
            _rows(tm, D_MODEL), _rows(tm, HG_WIDTH), _rows(tm, MLA_HEADS * V_DIM), _rows(tm, PLE_DIM),
            _resident((HG_WIDTH + MLA_HEADS * V_DIM, D_MODEL)), _resident((1, D_MODEL)),
            in_hbm, in_hbm, in_hbm,
            _resident((1, D_MODEL)), _resident((D_MODEL, D_MODEL)), _resident((PLE_DIM, D_MODEL)),
            _resident((1, D_MODEL)),
        ],
        out_specs=_rows(tm, D_MODEL),
        out_shape=jax.ShapeDtypeStruct((n, D_MODEL), F32),
        scratch_shapes=_ffn_weight_scratch(),
        compiler_params=pltpu.CompilerParams(dimension_semantics=("arbitrary",), vmem_limit_bytes=VMEM_LIMIT),
    )(h1, o_hg, o_mla, p[0].reshape(n, PLE_DIM), wo, row(ln_ffn2[0]),
      w2_gate[0], w2_up[0], w2_down[0], row(ln_ple[0]), w_ple_gate[0].astype(BF16), w_ple_proj[0].astype(BF16), row(ln_final))

    return y.reshape(bsz, seq, D_MODEL)
```

```python
import functools

import jax
import jax.numpy as jnp
from jax import lax
from jax.experimental import pallas as pl
from jax.experimental.pallas import tpu as pltpu

F32 = jnp.float32
BF16 = jnp.bfloat16

D_MODEL = 1024
D_FF = 2816
PLE_DIM = 256
HG_HEADS = 4
HG_DIM = 128
HG_WIDTH = HG_HEADS * HG_DIM
MLA_HEADS = 4
Q_LORA = 256
KV_LORA = 128
NOPE_DIM = 128
ROPE_DIM = 64
V_DIM = 128
QK_DIM = NOPE_DIM + ROPE_DIM
ROPE_THETA = 10000.0
EPS = 1e-6
LOG2_E = 1.4426950408889634

LANES = 128
SUBLANES = 8
BF16_ROWS = 2 * SUBLANES
HEAD_PAD = 2 * LANES
IN_WIDTH = 4 * HG_WIDTH + Q_LORA + KV_LORA + ROPE_DIM

FFN_CHUNK = 256
TOKEN_TILE = 512
POSTMIX_TILE = 1024
HGRN_CHUNK = 256
ATTN_BLOCK = 256
ATTN_LOOKAHEAD = 3
HGRN_AFTER_FFN_CHUNKS = (1, 3, 5, 7, 9, 10)
VMEM_LIMIT = 56 * 1024 * 1024

NT_DIMS = (((1,), (1,)), ((), ()))
TN_DIMS = (((0,), (0,)), ((), ()))


def _dot(a, b):
    return jnp.dot(a, b, preferred_element_type=F32)


def _rms(x, g):
    return x * lax.rsqrt(jnp.mean(x * x, axis=-1, keepdims=True) + EPS) * g


def _swiglu(xn, wg_ref, wu_ref, wd_ref, between=None):
    acc = None
    for idx, c in enumerate(range(0, D_FF, FFN_CHUNK)):
        g = _dot(xn, wg_ref[:, c:c + FFN_CHUNK])
        u = _dot(xn, wu_ref[:, c:c + FFN_CHUNK])
        a = (g * jax.nn.sigmoid(g) * u).astype(BF16)
        d = _dot(a, wd_ref[c:c + FFN_CHUNK, :])
        acc = d if acc is None else acc + d
        for piece in (between or {}).get(idx, ()):
            piece()
    return acc


def _rope(x, cos_t, sin_lo, sin_hi):
    return (x * cos_t + pltpu.roll(x, LANES - ROPE_DIM // 2, 1) * sin_lo
            + pltpu.roll(x, ROPE_DIM // 2, 1) * sin_hi)


def _premix_kernel(x_ref, pos_ref, invf_ref, g1_ref, wg_ref, wu_ref, wd_ref, gmix_ref, win_ref,
                   qan_ref, wq_ref, kvan_ref, wk_ref, wv_ref, lbl_ref, gn_ref,
                   h1_ref, qp_ref, kp_ref, v_ref, ohg_ref,
                   hq_s, hf_s, hi_s, hg_s, st_ref, *, tiles_per_row, chunk):
    i = pl.program_id(0)
    tm = x_ref.shape[0]

    @pl.when(i == 0)
    def _():
        hq_s[...] = jnp.zeros_like(hq_s)
        hf_s[...] = jnp.zeros_like(hf_s)
        hi_s[...] = jnp.zeros_like(hi_s)
        hg_s[...] = jnp.zeros_like(hg_s)
        st_ref[...] = jnp.zeros_like(st_ref)

    @pl.when(lax.rem(i + tiles_per_row - 1, tiles_per_row) == 0)
    def _():
        st_ref[...] = jnp.zeros_like(st_ref)

    pieces = []
    for r0 in range(0, tm, chunk):
        pieces += _hgrn_pieces(lbl_ref, hq_s, hf_s, hi_s, hg_s, gn_ref, ohg_ref, st_ref, r0, chunk)
    early_chunks = [c for c in HGRN_AFTER_FFN_CHUNKS if c < D_FF // FFN_CHUNK][:len(pieces)]
    early = {c: [piece] for c, piece in zip(early_chunks, pieces)}
    late = iter(pieces[len(early_chunks):])

    t_row = lax.broadcasted_iota(jnp.int32, (tm, tm), 0)
    t_col = lax.broadcasted_iota(jnp.int32, (tm, tm), 1)
    pos_col = jnp.sum(jnp.where(t_row == t_col, pos_ref[...], 0.0), axis=1, keepdims=True)
    ang = pos_col * invf_ref[...]
    lane = lax.broadcasted_iota(jnp.int32, ang.shape, 1)
    cos_a, sin_a = jnp.cos(ang), jnp.sin(ang)
    half = ROPE_DIM // 2
    cos_t = jnp.where(lane < ROPE_DIM, cos_a, 0.0)
    sin_lo = jnp.where(lane < half, -sin_a, 0.0)
    sin_hi = jnp.where((lane >= half) & (lane < ROPE_DIM), sin_a, 0.0)

    x = x_ref[...]
    xn = _rms(x, g1_ref[...]).astype(BF16)
    h1 = x + 0.5 * _swiglu(xn, wg_ref, wu_ref, wd_ref, early)
    h1_ref[...] = h1

    hn = _rms(h1, gmix_ref[...]).astype(BF16)
    w = HG_WIDTH
    new = []
    for j in range(4):
        new.append(_dot(hn, win_ref[:, j * w:(j + 1) * w]))
        piece = next(late, None)
        if piece is not None:
            piece()
    for piece in late:
        piece()
    c0 = 4 * w
    cq = _dot(hn, win_ref[:, c0:c0 + Q_LORA])
    ckv_kr = _dot(hn, win_ref[:, c0 + Q_LORA:IN_WIDTH])
    ckv = ckv_kr[:, 0:KV_LORA]
    kr = jnp.concatenate([ckv_kr[:, KV_LORA:KV_LORA + ROPE_DIM],
                          jnp.zeros((tm, LANES - ROPE_DIM), F32)], axis=1)

    q = _dot(_rms(cq, qan_ref[...]).astype(BF16), wq_ref[...]) * (QK_DIM ** -0.5 * LOG2_E)
    ckvn = _rms(ckv, kvan_ref[...]).astype(BF16)
    k_nope = _dot(ckvn, wk_ref[...])
    v_ref[...] = _dot(ckvn, wv_ref[...]).astype(v_ref.dtype)
    k_rope = _rope(kr, cos_t, sin_lo, sin_hi).astype(kp_ref.dtype)
    for h in range(MLA_HEADS):
        a = h * HEAD_PAD
        qp_ref[:, a:a + NOPE_DIM] = q[:, a:a + NOPE_DIM].astype(qp_ref.dtype)
        qp_ref[:, a + NOPE_DIM:a + HEAD_PAD] = _rope(
            q[:, a + NOPE_DIM:a + HEAD_PAD], cos_t, sin_lo, sin_hi).astype(qp_ref.dtype)
        kp_ref[:, a:a + NOPE_DIM] = k_nope[:, h * NOPE_DIM:(h + 1) * NOPE_DIM].astype(kp_ref.dtype)
        kp_ref[:, a + NOPE_DIM:a + HEAD_PAD] = k_rope

    hq_s[...] = new[0].astype(hq_s.dtype)
    hf_s[...] = new[1]
    hi_s[...] = new[2].astype(hi_s.dtype)
    hg_s[...] = new[3].astype(hg_s.dtype)


def _neg_gap(b, m):
    t = b.shape[0]
    if m >= SUBLANES:
        pieces = []
        for s in range(0, t, 2 * m):
            mid = b[s + m - 1:s + m, :]
            pieces += [mid - b[s:s + m], b[s + m:s + 2 * m] - mid]
        return jnp.concatenate(pieces, axis=0)
    b3 = b.reshape(t // SUBLANES, SUBLANES, LANES)
    sub = lax.broadcasted_iota(jnp.int32, b3.shape, 1)

    def row(i):
        return jnp.broadcast_to(b3[:, i:i + 1, :], b3.shape)

    mid = row(m - 1)
    for s in range(2 * m, SUBLANES, 2 * m):
        mid = jnp.where(sub >= s, row(s + m - 1), mid)
    return -jnp.abs(b3 - mid).reshape(t, LANES)


def _hgrn_pieces(lbl_ref, hq_ref, hf_ref, hi_ref, hg_ref, gn_ref, o_ref, st_ref, r0, t):
    rows = slice(r0, r0 + t)
    half = t // 2
    ctx = {}

    def gates():
        lg = lbl_ref[...]
        e = jnp.exp(lg - jnp.max(lg, axis=0, keepdims=True))
        lb = e[0:1, :] / jnp.sum(e, axis=0, keepdims=True)
        f_raw = hf_ref[rows, :]
        f = lb + (1.0 - lb) * jax.nn.sigmoid(f_raw)
        g = jnp.log(f)
        ctx["f"] = f
        ctx["kk"] = (1.0 - lb) * jax.nn.sigmoid(-f_raw)

        row = lax.broadcasted_iota(jnp.int32, (t, t), 0)
        col = lax.broadcasted_iota(jnp.int32, (t, t), 1)
        tri = (row >= col).astype(BF16)
        g1 = g.astype(BF16)
        r1 = g - g1.astype(F32)
        g2 = r1.astype(BF16)
        g3 = (r1 - g2.astype(F32)).astype(BF16)
        ctx["b"] = (_dot(tri, g1) + _dot(tri, g2) + _dot(tri, g3)) * LOG2_E

    def head(h):
        ri = lax.broadcasted_iota(jnp.int32, (half, half), 0)
        ci = lax.broadcasted_iota(jnp.int32, (half, half), 1)
        xm = jnp.where(ri > ci, ri ^ ci, 0)
        levels = [1 << i for i in range(half.bit_length() - 1)]
        masks = [ri == ci] + [(xm >= m) & (xm < 2 * m) for m in levels]
        odd_row = (lax.broadcasted_iota(jnp.int32, (t, HG_DIM), 0) & 1) == 1

        sl = slice(h * HG_DIM, (h + 1) * HG_DIM)
        q = hq_ref[rows, sl]
        v = hi_ref[rows, sl]
        kb = ctx["kk"][:, sl].astype(BF16)
        b = ctx["b"][:, sl]

        qk = [(q, kb)]
        for m in levels:
            if m == 1:
                decay = jnp.where(odd_row, ctx["f"][:, sl], 1.0).astype(BF16)
            else:
                decay = jnp.exp2(_neg_gap(b, m)).astype(BF16)
            qk.append((q * decay, kb * decay))
        diag = []
        for q0 in (0, half):
            p = None
            for (q_l, k_l), mask in zip(qk, masks):
                s_l = lax.dot_general(q_l[q0:q0 + half], k_l[q0:q0 + half], NT_DIMS, preferred_element_type=F32)
                p = jnp.where(mask, s_l, 0.0 if p is None else p)
            diag.append(p.astype(BF16))
        mid = b[half - 1:half, :]
        cross = lax.dot_general(q[half:] * jnp.exp2(b[half:] - mid).astype(BF16),
                                kb[:half] * jnp.exp2(mid - b[:half]).astype(BF16), NT_DIMS,
                                preferred_element_type=F32).astype(BF16)
        o = jnp.concatenate([_dot(diag[0], v[:half]),
                             _dot(jnp.concatenate([cross, diag[1]], axis=1), v)], axis=0)

        st = st_ref[h]
        b_last = b[t - 1:t, :]
        o = o + lax.dot_general(q * jnp.exp2(b).astype(BF16), st.astype(BF16), NT_DIMS,
                                preferred_element_type=F32)
        k_end = kb * jnp.exp2(b_last - b).astype(BF16)
        st_ref[h] = st * jnp.exp2(b_last) + lax.dot_general(v, k_end, TN_DIMS, preferred_element_type=F32)

        gate = hg_ref[rows, sl].astype(F32)
        o_ref[rows, sl] = (_rms(o, gn_ref[h:h + 1, :]) * (gate * jax.nn.sigmoid(gate))).astype(o_ref.dtype)

    return [gates] + [functools.partial(head, h) for h in range(HG_HEADS)]


def _attn_kernel(q_ref, k_ref, v_ref, wg_ref, wu_ref, wd_ref, o_ref, wg_bf_ref, wu_bf_ref, wd_bf_ref,
                 vaug_ref, *, tq):
    wg_bf_ref[...] = wg_ref[...].astype(wg_bf_ref.dtype)
    wu_bf_ref[...] = wu_ref[...].astype(wu_bf_ref.dtype)
    wd_bf_ref[...] = wd_ref[...].astype(wd_bf_ref.dtype)

    seq = q_ref.shape[0]
    for h in range(MLA_HEADS):
        vaug_ref[h, :, 0:V_DIM] = v_ref[:, h * V_DIM:(h + 1) * V_DIM]
        vaug_ref[h, :, V_DIM:2 * V_DIM] = jnp.ones((seq, V_DIM), vaug_ref.dtype)
    row = lax.broadcasted_iota(jnp.int32, (tq, tq), 0)
    col = lax.broadcasted_iota(jnp.int32, (tq, tq), 1)

    def scores(job):
        h, i = job
        lo = i * tq
        cols = slice(h * HEAD_PAD, (h + 1) * HEAD_PAD)
        q = q_ref[lo:lo + tq, cols]
        s_d = lax.dot_general(q, k_ref[lo:lo + tq, cols], NT_DIMS, preferred_element_type=F32)
        s_d = jnp.where(row >= col, s_d, -jnp.inf)
        s_p = lax.dot_general(q, k_ref[0:lo, cols], NT_DIMS, preferred_element_type=F32) if i > 0 else None
        return s_d, s_p

    jobs = [(h, i) for i in range(seq // tq) for h in range(MLA_HEADS)]
    pending = [scores(job) for job in jobs[:ATTN_LOOKAHEAD]]
    for n_done, (h, i) in enumerate(jobs):
        lo = i * tq
        s_d, s_p = pending.pop(0)
        if n_done + ATTN_LOOKAHEAD < len(jobs):
            pending.append(scores(jobs[n_done + ATTN_LOOKAHEAD]))
        m = jnp.max(s_d, axis=-1, keepdims=True)
        if i > 0:
            m = jnp.maximum(m, jnp.max(s_p, axis=-1, keepdims=True))
        acc = _dot(jnp.exp2(s_d - m).astype(BF16), vaug_ref[h, lo:lo + tq, :])
        if i > 0:
            acc += _dot(jnp.exp2(s_p - m).astype(BF16), vaug_ref[h, 0:lo, :])
        o_ref[lo:lo + tq, h * V_DIM:(h + 1) * V_DIM] = (
            acc[:, 0:V_DIM] / acc[:, V_DIM:2 * V_DIM]).astype(o_ref.dtype)


def _postmix_kernel(h1_ref, ohg_ref, omla_ref, p_ref, wo_ref, g2_ref, wg_ref, wu_ref, wd_ref,
                    gple_ref, wpg_ref, wpp_ref, gfin_ref, y_ref):
    h2 = (h1_ref[...] + _dot(ohg_ref[...], wo_ref[0:HG_WIDTH, :])
          + _dot(omla_ref[...], wo_ref[HG_WIDTH:HG_WIDTH + MLA_HEADS * V_DIM, :]))
    h3 = h2 + 0.5 * _swiglu(_rms(h2, g2_ref[...]).astype(BF16), wg_ref, wu_ref, wd_ref)
    gate = jax.nn.sigmoid(_dot(_rms(h3, gple_ref[...]).astype(BF16), wpg_ref[...]))
    h4 = h3 + gate * _dot(p_ref[...].astype(BF16), wpp_ref[...])
    y_ref[...] = _rms(h4, gfin_ref[...])


def _resident(shape):
    return pl.BlockSpec(shape, lambda *_: (0,) * len(shape), pipeline_mode=pl.Buffered(1))


def _rows(tile, width):
    return pl.BlockSpec((tile, width), lambda i: (i, 0))


def kernel(x, p, positions, ln_ffn1, w1_gate, w1_up, w1_down, ln_mix, w_in, hg_lb_logits, hg_out_norm,
           q_a_norm, w_q_up, kv_a_norm, w_kv_up, w_out, ln_ffn2, w2_gate, w2_up, w2_down, ln_ple,
           w_ple_gate, w_ple_proj, ln_final):
    bsz, seq, _ = x.shape
    assert p.shape[0] == 1 and hg_lb_logits.shape[0] == 2, "single-layer trunk"
    n = bsz * seq
    tm = min(TOKEN_TILE, seq)
    tc = min(HGRN_CHUNK, seq)
    tq = min(ATTN_BLOCK, seq)
    tp = min(POSTMIX_TILE, n)
    assert seq % tm == 0 and tm % tc == 0 and seq % tq == 0 and n % tp == 0

    x2 = x.reshape(n, D_MODEL)
    pos2 = positions.astype(F32).reshape(n // tm, 1, tm)
    half = ROPE_DIM // 2
    inv_freq = ROPE_THETA ** (-jnp.arange(half, dtype=F32) / half)
    invf = jnp.concatenate([inv_freq, inv_freq, jnp.zeros((LANES - ROPE_DIM,), F32)]).reshape(1, LANES)
    wg1, wu1, wd1 = w1_gate[0].astype(BF16), w1_up[0].astype(BF16), w1_down[0].astype(BF16)
    win = w_in[0].astype(BF16)
    wq = jnp.pad(w_q_up[0].astype(BF16).reshape(Q_LORA, MLA_HEADS, QK_DIM),
                 ((0, 0), (0, 0), (0, HEAD_PAD - QK_DIM))).reshape(Q_LORA, MLA_HEADS * HEAD_PAD)
    wkv = w_kv_up[0].astype(BF16).reshape(KV_LORA, MLA_HEADS, NOPE_DIM + V_DIM)
    wk = wkv[:, :, :NOPE_DIM].reshape(KV_LORA, MLA_HEADS * NOPE_DIM)
    wv = wkv[:, :, NOPE_DIM:].reshape(KV_LORA, MLA_HEADS * V_DIM)
    wo = w_out[0].astype(BF16)
    row = lambda a: a.reshape(1, -1)
    layer0 = lambda w: w.reshape(w.shape[1:])

    steps = n // tm
    tile = lambda width: pl.BlockSpec((tm, width), lambda i: (jnp.minimum(i, steps - 1), 0))
    prev_tile = pl.BlockSpec((tm, HG_WIDTH), lambda i: (jnp.maximum(i - 1, 0), 0))
    h1, qp, kp, vv, o_hg = pl.pallas_call(
        functools.partial(_premix_kernel, tiles_per_row=seq // tm, chunk=tc),
        name="premix",
        grid=(steps + 1,),
        in_specs=[
            tile(D_MODEL), pl.BlockSpec((None, 1, tm), lambda i: (jnp.minimum(i, steps - 1), 0, 0)),
            _resident((1, LANES)), _resident((1, D_MODEL)),
            _resident((D_MODEL, D_FF)), _resident((D_MODEL, D_FF)), _resident((D_FF, D_MODEL)),
            _resident((1, D_MODEL)), _resident((D_MODEL, IN_WIDTH)),
            _resident((1, Q_LORA)), _resident((Q_LORA, MLA_HEADS * HEAD_PAD)), _resident((1, KV_LORA)),
            _resident((KV_LORA, MLA_HEADS * NOPE_DIM)), _resident((KV_LORA, MLA_HEADS * V_DIM)),
            _resident((2, HG_WIDTH)), _resident((HG_HEADS, HG_DIM)),
        ],
        out_specs=[tile(D_MODEL), tile(MLA_HEADS * HEAD_PAD), tile(MLA_HEADS * HEAD_PAD),
                   tile(MLA_HEADS * V_DIM), prev_tile],
        out_shape=[
            jax.ShapeDtypeStruct((n, D_MODEL), F32),
            jax.ShapeDtypeStruct((n, MLA_HEADS * HEAD_PAD), BF16),
            jax.ShapeDtypeStruct((n, MLA_HEADS * HEAD_PAD), BF16),
            jax.ShapeDtypeStruct((n, MLA_HEADS * V_DIM), BF16),
            jax.ShapeDtypeStruct((n, HG_WIDTH), BF16),
        ],
        scratch_shapes=[pltpu.VMEM((tm, HG_WIDTH), BF16), pltpu.VMEM((tm, HG_WIDTH), F32),
                        pltpu.VMEM((tm, HG_WIDTH), BF16), pltpu.VMEM((tm, HG_WIDTH), BF16),
                        pltpu.VMEM((HG_HEADS, HG_DIM, HG_DIM), F32)],
        compiler_params=pltpu.CompilerParams(dimension_semantics=("arbitrary",), vmem_limit_bytes=VMEM_LIMIT),
    )(x2, pos2, invf, row(ln_ffn1[0]), wg1, wu1, wd1, row(ln_mix[0]), win,
      row(q_a_norm[0]), wq, row(kv_a_norm[0]), wk, wv, hg_lb_logits, hg_out_norm[0])

    seq_spec = lambda width: pl.BlockSpec((seq, MLA_HEADS * width), lambda b: (b, 0))
    assert D_MODEL % (bsz * BF16_ROWS) == 0 and D_FF % (bsz * BF16_ROWS) == 0
    slab = lambda rows, cols: pl.BlockSpec((rows // bsz, cols), lambda b: (b, 0))
    o_mla, wg2, wu2, wd2 = pl.pallas_call(
        functools.partial(_attn_kernel, tq=tq),
        name="mla_attn",
        grid=(bsz,),
        in_specs=[seq_spec(HEAD_PAD), seq_spec(HEAD_PAD), seq_spec(V_DIM),
                  slab(D_MODEL, D_FF), slab(D_MODEL, D_FF), slab(D_FF, D_MODEL)],
        out_specs=[seq_spec(V_DIM), slab(D_MODEL, D_FF), slab(D_MODEL, D_FF), slab(D_FF, D_MODEL)],
        out_shape=[jax.ShapeDtypeStruct((n, MLA_HEADS * V_DIM), BF16),
                   jax.ShapeDtypeStruct((D_MODEL, D_FF), BF16), jax.ShapeDtypeStruct((D_MODEL, D_FF), BF16),
                   jax.ShapeDtypeStruct((D_FF, D_MODEL), BF16)],
        scratch_shapes=[pltpu.VMEM((MLA_HEADS, seq, 2 * V_DIM), BF16)],
        compiler_params=pltpu.CompilerParams(dimension_semantics=("parallel",), vmem_limit_bytes=VMEM_LIMIT),
    )(qp, kp, vv, layer0(w2_gate), layer0(w2_up), layer0(w2_down))

    y = pl.pallas_call(
        _postmix_kernel,
        name="postmix",
        grid=(n // tp,),
        in_specs=[
            _rows(tp, D_MODEL), _rows(tp, HG_WIDTH), _rows(tp, MLA_HEADS * V_DIM), _rows(tp, PLE_DIM),
            _resident((HG_WIDTH + MLA_HEADS * V_DIM, D_MODEL)), _resident((1, D_MODEL)),
            _resident((D_MODEL, D_FF)), _resident((D_MODEL, D_FF)), _resident((D_FF, D_MODEL)),
            _resident((1, D_MODEL)), _resident((D_MODEL, D_MODEL)), _resident((PLE_DIM, D_MODEL)),
            _resident((1, D_MODEL)),
        ],
        out_specs=_rows(tp, D_MODEL),
        out_shape=jax.ShapeDtypeStruct((n, D_MODEL), F32),
        compiler_params=pltpu.CompilerParams(dimension_semantics=("parallel",), vmem_limit_bytes=VMEM_LIMIT),
    )(h1, o_hg, o_mla, p[0].reshape(n, PLE_DIM), wo, row(ln_ffn2[0]),
      wg2, wu2, wd2, row(ln_ple[0]), w_ple_gate[0].astype(BF16), w_ple_proj[0].astype(BF16), row(ln_final))

    return y.reshape(bsz, seq, D_MODEL)
```

```python
import functools

import jax
import jax.numpy as jnp
from jax import lax
from jax.experimental import pallas as pl
from jax.experimental.pallas import tpu as pltpu

F32 = jnp.float32
BF16 = jnp.bfloat16

D_MODEL = 1024
D_FF = 2816
PLE_DIM = 256
HG_HEADS = 4
HG_DIM = 128
HG_WIDTH = HG_HEADS * HG_DIM
MLA_HEADS = 4
Q_LORA = 256
KV_LORA = 128
NOPE_DIM = 128
ROPE_DIM = 64
V_DIM = 128
QK_DIM = NOPE_DIM + ROPE_DIM
ROPE_THETA = 10000.0
EPS = 1e-6
LOG2_E = 1.4426950408889634

LANES = 128
SUBLANES = 8
BF16_ROWS = 2 * SUBLANES
HEAD_PAD = 2 * LANES
HG_GROUPS = 4
IN_WIDTH = HG_GROUPS * HG_WIDTH + Q_LORA + KV_LORA + ROPE_DIM

FFN_CHUNK = 256
TOKEN_TILE = 512
POSTMIX_TILE = 1024
HGRN_CHUNK = 256
ATTN_BLOCK = 256
ATTN_LOOKAHEAD = 3
HGRN_AFTER_FFN_CHUNKS = (1, 3, 5, 7, 9, 10)
VMEM_LIMIT = 56 * 1024 * 1024

NT_DIMS = (((1,), (1,)), ((), ()))
TN_DIMS = (((0,), (0,)), ((), ()))


def _dot(a, b):
    return jnp.dot(a, b, preferred_element_type=F32)


def _rms(x, g):
    return x * lax.rsqrt(jnp.mean(x * x, axis=-1, keepdims=True) + EPS) * g


def _swiglu(xn, wg_ref, wu_ref, wd_ref, between=None):
    acc = None
    for idx, c in enumerate(range(0, D_FF, FFN_CHUNK)):
        g = _dot(xn, wg_ref[:, c:c + FFN_CHUNK])
        u = _dot(xn, wu_ref[:, c:c + FFN_CHUNK])
        a = (g * jax.nn.sigmoid(g) * u).astype(BF16)
        d = _dot(a, wd_ref[c:c + FFN_CHUNK, :])
        acc = d if acc is None else acc + d
        for piece in (between or {}).get(idx, ()):
            piece()
    return acc


def _rope(x, cos_t, sin_lo, sin_hi):
    return (x * cos_t + pltpu.roll(x, LANES - ROPE_DIM // 2, 1) * sin_lo
            + pltpu.roll(x, ROPE_DIM // 2, 1) * sin_hi)


def _premix_kernel(x_ref, pos_ref, invf_ref, g1_ref, wg_ref, wu_ref, wd_ref, gmix_ref, win_ref,
                   qan_ref, wq_ref, kvan_ref, wk_ref, wv_ref, lbl_ref, gn_ref,
                   h1_ref, qp_ref, kp_ref, v_ref, ohg_ref,
                   hq_s, hf_s, hi_s, hg_s, st_ref, *, tiles_per_row, chunk):
    i = pl.program_id(0)
    tm = x_ref.shape[0]

    @pl.when(i == 0)
    def _():
        hq_s[...] = jnp.zeros_like(hq_s)
        hf_s[...] = jnp.zeros_like(hf_s)
        hi_s[...] = jnp.zeros_like(hi_s)
        hg_s[...] = jnp.zeros_like(hg_s)
        st_ref[...] = jnp.zeros_like(st_ref)

    @pl.when(lax.rem(i + tiles_per_row - 1, tiles_per_row) == 0)
    def _():
        st_ref[...] = jnp.zeros_like(st_ref)

    pieces = []
    for r0 in range(0, tm, chunk):
        pieces += _hgrn_pieces(lbl_ref, hq_s, hf_s, hi_s, hg_s, gn_ref, ohg_ref, st_ref, r0, chunk)
    early_chunks = [c for c in HGRN_AFTER_FFN_CHUNKS if c < D_FF // FFN_CHUNK][:len(pieces)]
    early = {c: [piece] for c, piece in zip(early_chunks, pieces)}
    late = iter(pieces[len(early_chunks):])

    t_row = lax.broadcasted_iota(jnp.int32, (tm, tm), 0)
    t_col = lax.broadcasted_iota(jnp.int32, (tm, tm), 1)
    pos_col = jnp.sum(jnp.where(t_row == t_col, pos_ref[...], 0.0), axis=1, keepdims=True)
    ang = pos_col * invf_ref[...]
    lane = lax.broadcasted_iota(jnp.int32, ang.shape, 1)
    cos_a, sin_a = jnp.cos(ang), jnp.sin(ang)
    half = ROPE_DIM // 2
    cos_t = jnp.where(lane < ROPE_DIM, cos_a, 0.0)
    sin_lo = jnp.where(lane < half, -sin_a, 0.0)
    sin_hi = jnp.where((lane >= half) & (lane < ROPE_DIM), sin_a, 0.0)

    x = x_ref[...]
    xn = _rms(x, g1_ref[...]).astype(BF16)
    h1 = x + 0.5 * _swiglu(xn, wg_ref, wu_ref, wd_ref, early)
    h1_ref[...] = h1

    hn = _rms(h1, gmix_ref[...]).astype(BF16)
    w = HG_WIDTH
    new = []
    for j in range(HG_GROUPS):
        new.append(_dot(hn, win_ref[:, j * w:(j + 1) * w]))
        piece = next(late, None)
        if piece is not None:
            piece()
    for piece in late:
        piece()
    c0 = HG_GROUPS * w
    cq = _dot(hn, win_ref[:, c0:c0 + Q_LORA])
    ckv_kr = _dot(hn, win_ref[:, c0 + Q_LORA:IN_WIDTH])
    ckv = ckv_kr[:, 0:KV_LORA]
    kr = jnp.concatenate([ckv_kr[:, KV_LORA:KV_LORA + ROPE_DIM],
                          jnp.zeros((tm, LANES - ROPE_DIM), F32)], axis=1)

    q = _dot(_rms(cq, qan_ref[...]).astype(BF16), wq_ref[...]) * (QK_DIM ** -0.5 * LOG2_E)
    ckvn = _rms(ckv, kvan_ref[...]).astype(BF16)
    k_nope = _dot(ckvn, wk_ref[...])
    v_ref[...] = _dot(ckvn, wv_ref[...]).astype(v_ref.dtype)
    k_rope = _rope(kr, cos_t, sin_lo, sin_hi).astype(kp_ref.dtype)
    for h in range(MLA_HEADS):
        a = h * HEAD_PAD
        qp_ref[:, a:a + NOPE_DIM] = q[:, a:a + NOPE_DIM].astype(qp_ref.dtype)
        qp_ref[:, a + NOPE_DIM:a + HEAD_PAD] = _rope(
            q[:, a + NOPE_DIM:a + HEAD_PAD], cos_t, sin_lo, sin_hi).astype(qp_ref.dtype)
        kp_ref[:, a:a + NOPE_DIM] = k_nope[:, h * NOPE_DIM:(h + 1) * NOPE_DIM].astype(kp_ref.dtype)
        kp_ref[:, a + NOPE_DIM:a + HEAD_PAD] = k_rope

    hq_s[...] = new[0].astype(hq_s.dtype)
    hf_s[...] = new[1]
    hi_s[...] = new[2].astype(hi_s.dtype)
    hg_s[...] = new[3].astype(hg_s.dtype)


def _neg_gap(b, m):
    t = b.shape[0]
    if m >= SUBLANES:
        pieces = []
        for s in range(0, t, 2 * m):
            mid = b[s + m - 1:s + m, :]
            pieces += [mid - b[s:s + m], b[s + m:s + 2 * m] - mid]
        return jnp.concatenate(pieces, axis=0)
    b3 = b.reshape(t // SUBLANES, SUBLANES, LANES)
    sub = lax.broadcasted_iota(jnp.int32, b3.shape, 1)

    def row(i):
        return jnp.broadcast_to(b3[:, i:i + 1, :], b3.shape)

    mid = row(m - 1)
    for s in range(2 * m, SUBLANES, 2 * m):
        mid = jnp.where(sub >= s, row(s + m - 1), mid)
    return -jnp.abs(b3 - mid).reshape(t, LANES)


def _hgrn_pieces(lbl_ref, hq_ref, hf_ref, hi_ref, hg_ref, gn_ref, o_ref, st_ref, r0, t):
    rows = slice(r0, r0 + t)
    half = t // 2
    ctx = {}

    def gates():
        lg = lbl_ref[...]
        e = jnp.exp(lg - jnp.max(lg, axis=0, keepdims=True))
        lb = e[0:1, :] / jnp.sum(e, axis=0, keepdims=True)
        f_raw = hf_ref[rows, :]
        f = lb + (1.0 - lb) * jax.nn.sigmoid(f_raw)
        g = jnp.log(f)
        ctx["f"] = f
        ctx["kk"] = (1.0 - lb) * jax.nn.sigmoid(-f_raw)

        row = lax.broadcasted_iota(jnp.int32, (t, t), 0)
        col = lax.broadcasted_iota(jnp.int32, (t, t), 1)
        tri = (row >= col).astype(BF16)
        g1 = g.astype(BF16)
        r1 = g - g1.astype(F32)
        g2 = r1.astype(BF16)
        g3 = (r1 - g2.astype(F32)).astype(BF16)
        ctx["b"] = (_dot(tri, g1) + _dot(tri, g2) + _dot(tri, g3)) * LOG2_E

    def head(h):
        ri = lax.broadcasted_iota(jnp.int32, (half, half), 0)
        ci = lax.broadcasted_iota(jnp.int32, (half, half), 1)
        xm = jnp.where(ri > ci, ri ^ ci, 0)
        levels = [1 << i for i in range(half.bit_length() - 1)]
        masks = [ri == ci] + [(xm >= m) & (xm < 2 * m) for m in levels]
        odd_row = (lax.broadcasted_iota(jnp.int32, (t, HG_DIM), 0) & 1) == 1

        sl = slice(h * HG_DIM, (h + 1) * HG_DIM)
        q = hq_ref[rows, sl]
        v = hi_ref[rows, sl]
        kb = ctx["kk"][:, sl].astype(BF16)
        b = ctx["b"][:, sl]

        qk = [(q, kb)]
        for m in levels:
            if m == 1:
                decay = jnp.where(odd_row, ctx["f"][:, sl], 1.0).astype(BF16)
            else:
                decay = jnp.exp2(_neg_gap(b, m)).astype(BF16)
            qk.append((q * decay, kb * decay))
        diag = []
        for q0 in (0, half):
            p = None
            for (q_l, k_l), mask in zip(qk, masks):
                s_l = lax.dot_general(q_l[q0:q0 + half], k_l[q0:q0 + half], NT_DIMS, preferred_element_type=F32)
                p = jnp.where(mask, s_l, 0.0 if p is None else p)
            diag.append(p.astype(BF16))
        mid = b[half - 1:half, :]
        cross = lax.dot_general(q[half:] * jnp.exp2(b[half:] - mid).astype(BF16),
                                kb[:half] * jnp.exp2(mid - b[:half]).astype(BF16), NT_DIMS,
                                preferred_element_type=F32).astype(BF16)
        o = jnp.concatenate([_dot(diag[0], v[:half]),
                             _dot(jnp.concatenate([cross, diag[1]], axis=1), v)], axis=0)

        st = st_ref[h]
        b_last = b[t - 1:t, :]
        o = o + lax.dot_general(q * jnp.exp2(b).astype(BF16), st.astype(BF16), NT_DIMS,
                                preferred_element_type=F32)
        k_end = kb * jnp.exp2(b_last - b).astype(BF16)
        st_ref[h] = st * jnp.exp2(b_last) + lax.dot_general(v, k_end, TN_DIMS, preferred_element_type=F32)

        gate = hg_ref[rows, sl].astype(F32)
        o_ref[rows, sl] = (_rms(o, gn_ref[h:h + 1, :]) * (gate * jax.nn.sigmoid(gate))).astype(o_ref.dtype)

    return [gates] + [functools.partial(head, h) for h in range(HG_HEADS)]


def _attn_kernel(q_ref, k_ref, v_ref, wg_ref, wu_ref, wd_ref, o_ref, wg_bf_ref, wu_bf_ref, wd_bf_ref,
                 vaug_ref, *, tq):
    wg_bf_ref[...] = wg_ref[...].astype(wg_bf_ref.dtype)
    wu_bf_ref[...] = wu_ref[...].astype(wu_bf_ref.dtype)
    wd_bf_ref[...] = wd_ref[...].astype(wd_bf_ref.dtype)

    seq = q_ref.shape[0]
    for h in range(MLA_HEADS):
        vaug_ref[h, :, 0:V_DIM] = v_ref[:, h * V_DIM:(h + 1) * V_DIM]
        vaug_ref[h, :, V_DIM:2 * V_DIM] = jnp.ones((seq, V_DIM), vaug_ref.dtype)
    row = lax.broadcasted_iota(jnp.int32, (tq, tq), 0)
    col = lax.broadcasted_iota(jnp.int32, (tq, tq), 1)

    def scores(job):
        h, i = job
        lo = i * tq
        cols = slice(h * HEAD_PAD, (h + 1) * HEAD_PAD)
        q = q_ref[lo:lo + tq, cols]
        s_d = lax.dot_general(q, k_ref[lo:lo + tq, cols], NT_DIMS, preferred_element_type=F32)
        s_d = jnp.where(row >= col, s_d, -jnp.inf)
        s_p = lax.dot_general(q, k_ref[0:lo, cols], NT_DIMS, preferred_element_type=F32) if i > 0 else None
        return s_d, s_p

    jobs = [(h, i) for i in range(seq // tq) for h in range(MLA_HEADS)]
    pending = [scores(job) for job in jobs[:ATTN_LOOKAHEAD]]
    for n_done, (h, i) in enumerate(jobs):
        lo = i * tq
        s_d, s_p = pending.pop(0)
        if n_done + ATTN_LOOKAHEAD < len(jobs):
            pending.append(scores(jobs[n_done + ATTN_LOOKAHEAD]))
        m = jnp.max(s_d, axis=-1, keepdims=True)
        if i > 0:
            m = jnp.maximum(m, jnp.max(s_p, axis=-1, keepdims=True))
        acc = _dot(jnp.exp2(s_d - m).astype(BF16), vaug_ref[h, lo:lo + tq, :])
        if i > 0:
            acc += _dot(jnp.exp2(s_p - m).astype(BF16), vaug_ref[h, 0:lo, :])
        o_ref[lo:lo + tq, h * V_DIM:(h + 1) * V_DIM] = (
            acc[:, 0:V_DIM] / acc[:, V_DIM:2 * V_DIM]).astype(o_ref.dtype)


def _postmix_kernel(h1_ref, ohg_ref, omla_ref, p_ref, wo_ref, g2_ref, wg_ref, wu_ref, wd_ref,
                    gple_ref, wpg_ref, wpp_ref, gfin_ref, y_ref):
    h2 = (h1_ref[...] + _dot(ohg_ref[...], wo_ref[0:HG_WIDTH, :])
          + _dot(omla_ref[...], wo_ref[HG_WIDTH:HG_WIDTH + MLA_HEADS * V_DIM, :]))
    h3 = h2 + 0.5 * _swiglu(_rms(h2, g2_ref[...]).astype(BF16), wg_ref, wu_ref, wd_ref)
    gate = jax.nn.sigmoid(_dot(_rms(h3, gple_ref[...]).astype(BF16), wpg_ref[...]))
    h4 = h3 + gate * _dot(p_ref[...].astype(BF16), wpp_ref[...])
    y_ref[...] = _rms(h4, gfin_ref[...])


def _resident(shape):
    return pl.BlockSpec(shape, lambda *_: (0,) * len(shape), pipeline_mode=pl.Buffered(1))


def _rows(tile, width):
    return pl.BlockSpec((tile, width), lambda i: (i, 0))


def kernel(x, p, positions, ln_ffn1, w1_gate, w1_up, w1_down, ln_mix, w_in, hg_lb_logits, hg_out_norm,
           q_a_norm, w_q_up, kv_a_norm, w_kv_up, w_out, ln_ffn2, w2_gate, w2_up, w2_down, ln_ple,
           w_ple_gate, w_ple_proj, ln_final):
    bsz, seq, _ = x.shape
    assert p.shape[0] == 1 and hg_lb_logits.shape[0] == 2, "single-layer trunk"
    n = bsz * seq
    tm = min(TOKEN_TILE, seq)
    tc = min(HGRN_CHUNK, seq)
    tq = min(ATTN_BLOCK, seq)
    tp = min(POSTMIX_TILE, n)
    assert seq % tm == 0 and tm % tc == 0 and seq % tq == 0 and n % tp == 0

    x2 = x.reshape(n, D_MODEL)
    pos2 = positions.astype(F32).reshape(n // tm, 1, tm)
    half = ROPE_DIM // 2
    inv_freq = ROPE_THETA ** (-jnp.arange(half, dtype=F32) / half)
    invf = jnp.concatenate([inv_freq, inv_freq, jnp.zeros((LANES - ROPE_DIM,), F32)]).reshape(1, LANES)
    wg1, wu1, wd1 = w1_gate[0].astype(BF16), w1_up[0].astype(BF16), w1_down[0].astype(BF16)
    win = w_in[0].astype(BF16)
    wq = jnp.pad(w_q_up[0].astype(BF16).reshape(Q_LORA, MLA_HEADS, QK_DIM),
                 ((0, 0), (0, 0), (0, HEAD_PAD - QK_DIM))).reshape(Q_LORA, MLA_HEADS * HEAD_PAD)
    wkv = w_kv_up[0].astype(BF16).reshape(KV_LORA, MLA_HEADS, NOPE_DIM + V_DIM)
    wk = wkv[:, :, :NOPE_DIM].reshape(KV_LORA, MLA_HEADS * NOPE_DIM)
    wv = wkv[:, :, NOPE_DIM:].reshape(KV_LORA, MLA_HEADS * V_DIM)
    wo = w_out[0].astype(BF16)
    row = lambda a: a.reshape(1, -1)
    layer0 = lambda w: w.reshape(w.shape[1:])

    steps = n // tm
    tile = lambda width: pl.BlockSpec((tm, width), lambda i: (jnp.minimum(i, steps - 1), 0))
    prev_tile = pl.BlockSpec((tm, HG_WIDTH), lambda i: (jnp.maximum(i - 1, 0), 0))
    h1, qp, kp, vv, o_hg = pl.pallas_call(
        functools.partial(_premix_kernel, tiles_per_row=seq // tm, chunk=tc),
        name="premix",
        grid=(steps + 1,),
        in_specs=[
            tile(D_MODEL), pl.BlockSpec((None, 1, tm), lambda i: (jnp.minimum(i, steps - 1), 0, 0)),
            _resident((1, LANES)), _resident((1, D_MODEL)),
            _resident((D_MODEL, D_FF)), _resident((D_MODEL, D_FF)), _resident((D_FF, D_MODEL)),
            _resident((1, D_MODEL)), _resident((D_MODEL, IN_WIDTH)),
            _resident((1, Q_LORA)), _resident((Q_LORA, MLA_HEADS * HEAD_PAD)), _resident((1, KV_LORA)),
            _resident((KV_LORA, MLA_HEADS * NOPE_DIM)), _resident((KV_LORA, MLA_HEADS * V_DIM)),
            _resident((2, HG_WIDTH)), _resident((HG_HEADS, HG_DIM)),
        ],
        out_specs=[tile(D_MODEL), tile(MLA_HEADS * HEAD_PAD), tile(MLA_HEADS * HEAD_PAD),
                   tile(MLA_HEADS * V_DIM), prev_tile],
        out_shape=[
            jax.ShapeDtypeStruct((n, D_MODEL), F32),
            jax.ShapeDtypeStruct((n, MLA_HEADS * HEAD_PAD), BF16),
            jax.ShapeDtypeStruct((n, MLA_HEADS * HEAD_PAD), BF16),
            jax.ShapeDtypeStruct((n, MLA_HEADS * V_DIM), BF16),
            jax.ShapeDtypeStruct((n, HG_WIDTH), BF16),
        ],
        scratch_shapes=[pltpu.VMEM((tm, HG_WIDTH), BF16), pltpu.VMEM((tm, HG_WIDTH), F32),
                        pltpu.VMEM((tm, HG_WIDTH), BF16), pltpu.VMEM((tm, HG_WIDTH), BF16),
                        pltpu.VMEM((HG_HEADS, HG_DIM, HG_DIM), F32)],
        compiler_params=pltpu.CompilerParams(dimension_semantics=("arbitrary",), vmem_limit_bytes=VMEM_LIMIT),
    )(x2, pos2, invf, row(ln_ffn1[0]), wg1, wu1, wd1, row(ln_mix[0]), win,
      row(q_a_norm[0]), wq, row(kv_a_norm[0]), wk, wv, hg_lb_logits, hg_out_norm[0])

    seq_spec = lambda width: pl.BlockSpec((seq, MLA_HEADS * width), lambda b: (b, 0))
    assert D_MODEL % (bsz * BF16_ROWS) == 0 and D_FF % (bsz * BF16_ROWS) == 0
    slab = lambda rows, cols: pl.BlockSpec((rows // bsz, cols), lambda b: (b, 0))
    o_mla, wg2, wu2, wd2 = pl.pallas_call(
        functools.partial(_attn_kernel, tq=tq),
        name="mla_attn",
        grid=(bsz,),
        in_specs=[seq_spec(HEAD_PAD), seq_spec(HEAD_PAD), seq_spec(V_DIM),
                  slab(D_MODEL, D_FF), slab(D_MODEL, D_FF), slab(D_FF, D_MODEL)],
        out_specs=[seq_spec(V_DIM), slab(D_MODEL, D_FF), slab(D_MODEL, D_FF), slab(D_FF, D_MODEL)],
        out_shape=[jax.ShapeDtypeStruct((n, MLA_HEADS * V_DIM), BF16),
                   jax.ShapeDtypeStruct((D_MODEL, D_FF), BF16), jax.ShapeDtypeStruct((D_MODEL, D_FF), BF16),
                   jax.ShapeDtypeStruct((D_FF, D_MODEL), BF16)],
        scratch_shapes=[pltpu.VMEM((MLA_HEADS, seq, 2 * V_DIM), BF16)],
        compiler_params=pltpu.CompilerParams(dimension_semantics=("parallel",), vmem_limit_bytes=VMEM_LIMIT),
    )(qp, kp, vv, layer0(w2_gate), layer0(w2_up), layer0(w2_down))

    y = pl.pallas_call(
        _postmix_kernel,
        name="postmix",
        grid=(n // tp,),
        in_specs=[
            _rows(tp, D_MODEL), _rows(tp, HG_WIDTH), _rows(tp, MLA_HEADS * V_DIM), _rows(tp, PLE_DIM),
            _resident((HG_WIDTH + MLA_HEADS * V_DIM, D_MODEL)), _resident((1, D_MODEL)),
            _resident((D_MODEL, D_FF)), _resident((D_MODEL, D_FF)), _resident((D_FF, D_MODEL)),
            _resident((1, D_MODEL)), _resident((D_MODEL, D_MODEL)), _resident((PLE_DIM, D_MODEL)),
            _resident((1, D_MODEL)),
        ],
        out_specs=_rows(tp, D_MODEL),
        out_shape=jax.ShapeDtypeStruct((n, D_MODEL), F32),
        compiler_params=pltpu.CompilerParams(dimension_semantics=("parallel",), vmem_limit_bytes=VMEM_LIMIT),
    )(h1, o_hg, o_mla, p[0].reshape(n, PLE_DIM), wo, row(ln_ffn2[0]),
      wg2, wu2, wd2, row(ln_ple[0]), w_ple_gate[0].astype(BF16), w_ple_proj[0].astype(BF16), row(ln_final))

    return y.reshape(bsz, seq, D_MODEL)
```

```python
import functools

import jax
import jax.numpy as jnp
from jax import lax
from jax.experimental import pallas as pl
from jax.experimental.pallas import tpu as pltpu

F32 = jnp.float32
BF16 = jnp.bfloat16

D_MODEL = 1024
D_FF = 2816
PLE_DIM = 256
HG_HEADS = 4
HG_DIM = 128
HG_WIDTH = HG_HEADS * HG_DIM
MLA_HEADS = 4
Q_LORA = 256
KV_LORA = 128
NOPE_DIM = 128
ROPE_DIM = 64
V_DIM = 128
QK_DIM = NOPE_DIM + ROPE_DIM
ROPE_THETA = 10000.0
EPS = 1e-6
LOG2_E = 1.4426950408889634

LANES = 128
SUBLANES = 8
BF16_ROWS = 2 * SUBLANES
HEAD_PAD = 2 * LANES
HG_GROUPS = 4
IN_WIDTH = HG_GROUPS * HG_WIDTH + Q_LORA + KV_LORA + ROPE_DIM

FFN_CHUNK = 256
TOKEN_TILE = 512
POSTMIX_TILE = 1024
HGRN_CHUNK = 256
ATTN_BLOCK = 256
ATTN_LOOKAHEAD = 3
HGRN_AFTER_FFN_CHUNKS = (1, 3, 5, 7, 9, 10)
VMEM_LIMIT = 56 * 1024 * 1024

NT_DIMS = (((1,), (1,)), ((), ()))
TN_DIMS = (((0,), (0,)), ((), ()))


def _dot(a, b):
    return jnp.dot(a, b, preferred_element_type=F32)


def _rms(x, g):
    return x * lax.rsqrt(jnp.mean(x * x, axis=-1, keepdims=True) + EPS) * g


def _swiglu(xn, wg_ref, wu_ref, wd_ref, between=None):
    acc = None
    for idx, c in enumerate(range(0, D_FF, FFN_CHUNK)):
        g = _dot(xn, wg_ref[:, c:c + FFN_CHUNK])
        u = _dot(xn, wu_ref[:, c:c + FFN_CHUNK])
        a = (g * jax.nn.sigmoid(g) * u).astype(BF16)
        d = _dot(a, wd_ref[c:c + FFN_CHUNK, :])
        acc = d if acc is None else acc + d
        for piece in (between or {}).get(idx, ()):
            piece()
    return acc


def _rope(x, cos_t, sin_lo, sin_hi):
    return (x * cos_t + pltpu.roll(x, LANES - ROPE_DIM // 2, 1) * sin_lo
            + pltpu.roll(x, ROPE_DIM // 2, 1) * sin_hi)


def _premix_kernel(x_ref, pos_ref, invf_ref, g1_ref, wg_ref, wu_ref, wd_ref, gmix_ref, win_ref,
                   qan_ref, wq_ref, kvan_ref, wk_ref, wv_ref, lbl_ref, gn_ref,
                   h1_ref, qp_ref, kp_ref, v_ref, ohg_ref,
                   hq_s, hf_s, hi_s, hg_s, st_ref, *, tiles_per_row, chunk):
    i = pl.program_id(0)
    tm = x_ref.shape[0]

    @pl.when(i == 0)
    def _():
        hq_s[...] = jnp.zeros_like(hq_s)
        hf_s[...] = jnp.zeros_like(hf_s)
        hi_s[...] = jnp.zeros_like(hi_s)
        hg_s[...] = jnp.zeros_like(hg_s)
        st_ref[...] = jnp.zeros_like(st_ref)

    @pl.when(lax.rem(i + tiles_per_row - 1, tiles_per_row) == 0)
    def _():
        st_ref[...] = jnp.zeros_like(st_ref)

    pieces = []
    for r0 in range(0, tm, chunk):
        pieces += _hgrn_pieces(lbl_ref, hq_s, hf_s, hi_s, hg_s, gn_ref, ohg_ref, st_ref, r0, chunk)
    early_chunks = [c for c in HGRN_AFTER_FFN_CHUNKS if c < D_FF // FFN_CHUNK][:len(pieces)]
    early = {c: [piece] for c, piece in zip(early_chunks, pieces)}
    late = iter(pieces[len(early_chunks):])

    t_row = lax.broadcasted_iota(jnp.int32, (tm, tm), 0)
    t_col = lax.broadcasted_iota(jnp.int32, (tm, tm), 1)
    pos_col = jnp.sum(jnp.where(t_row == t_col, pos_ref[...], 0.0), axis=1, keepdims=True)
    ang = pos_col * invf_ref[...]
    lane = lax.broadcasted_iota(jnp.int32, ang.shape, 1)
    cos_a, sin_a = jnp.cos(ang), jnp.sin(ang)
    half = ROPE_DIM // 2
    cos_t = jnp.where(lane < ROPE_DIM, cos_a, 0.0)
    sin_lo = jnp.where(lane < half, -sin_a, 0.0)
    sin_hi = jnp.where((lane >= half) & (lane < ROPE_DIM), sin_a, 0.0)

    x = x_ref[...]
    xn = _rms(x, g1_ref[...]).astype(BF16)
    h1 = x + 0.5 * _swiglu(xn, wg_ref, wu_ref, wd_ref, early)
    h1_ref[...] = h1

    hn = _rms(h1, gmix_ref[...]).astype(BF16)
    w = HG_WIDTH
    new = []
    for j in range(HG_GROUPS):
        new.append(_dot(hn, win_ref[:, j * w:(j + 1) * w]))
        piece = next(late, None)
        if piece is not None:
            piece()
    for piece in late:
        piece()
    c0 = HG_GROUPS * w
    cq = _dot(hn, win_ref[:, c0:c0 + Q_LORA])
    ckv_kr = _dot(hn, win_ref[:, c0 + Q_LORA:IN_WIDTH])
    ckv = ckv_kr[:, 0:KV_LORA]
    kr = jnp.concatenate([ckv_kr[:, KV_LORA:KV_LORA + ROPE_DIM],
                          jnp.zeros((tm, LANES - ROPE_DIM), F32)], axis=1)

    q = _dot(_rms(cq, qan_ref[...]).astype(BF16), wq_ref[...]) * (QK_DIM ** -0.5 * LOG2_E)
    ckvn = _rms(ckv, kvan_ref[...]).astype(BF16)
    k_nope = _dot(ckvn, wk_ref[...])
    v_ref[...] = _dot(ckvn, wv_ref[...]).astype(v_ref.dtype)
    k_rope = _rope(kr, cos_t, sin_lo, sin_hi).astype(kp_ref.dtype)
    for h in range(MLA_HEADS):
        a = h * HEAD_PAD
        qp_ref[:, a:a + NOPE_DIM] = q[:, a:a + NOPE_DIM].astype(qp_ref.dtype)
        qp_ref[:, a + NOPE_DIM:a + HEAD_PAD] = _rope(
            q[:, a + NOPE_DIM:a + HEAD_PAD], cos_t, sin_lo, sin_hi).astype(qp_ref.dtype)
        kp_ref[:, a:a + NOPE_DIM] = k_nope[:, h * NOPE_DIM:(h + 1) * NOPE_DIM].astype(kp_ref.dtype)
        kp_ref[:, a + NOPE_DIM:a + HEAD_PAD] = k_rope

    hq_s[...] = new[0].astype(hq_s.dtype)
    hf_s[...] = new[1]
    hi_s[...] = new[2].astype(hi_s.dtype)
    hg_s[...] = new[3].astype(hg_s.dtype)


def _neg_gap(b, m):
    t = b.shape[0]
    if m >= SUBLANES:
        pieces = []
        for s in range(0, t, 2 * m):
            mid = b[s + m - 1:s + m, :]
            pieces += [mid - b[s:s + m], b[s + m:s + 2 * m] - mid]
        return jnp.concatenate(pieces, axis=0)
    b3 = b.reshape(t // SUBLANES, SUBLANES, LANES)
    sub = lax.broadcasted_iota(jnp.int32, b3.shape, 1)

    def row(i):
        return jnp.broadcast_to(b3[:, i:i + 1, :], b3.shape)

    mid = row(m - 1)
    for s in range(2 * m, SUBLANES, 2 * m):
        mid = jnp.where(sub >= s, row(s + m - 1), mid)
    return -jnp.abs(b3 - mid).reshape(t, LANES)


def _hgrn_pieces(lbl_ref, hq_ref, hf_ref, hi_ref, hg_ref, gn_ref, o_ref, st_ref, r0, t):
    rows = slice(r0, r0 + t)
    half = t // 2
    ctx = {}

    def gates():
        lg = lbl_ref[...]
        e = jnp.exp(lg - jnp.max(lg, axis=0, keepdims=True))
        lb = e[0:1, :] / jnp.sum(e, axis=0, keepdims=True)
        f_raw = hf_ref[rows, :]
        sig = jax.nn.sigmoid(f_raw)
        f = lb + (1.0 - lb) * sig
        g = jnp.log(f)
        ctx["f"] = f
        ctx["kk"] = (1.0 - lb) * (1.0 - sig)

        row = lax.broadcasted_iota(jnp.int32, (t, t), 0)
        col = lax.broadcasted_iota(jnp.int32, (t, t), 1)
        tri = (row >= col).astype(BF16)
        g1 = g.astype(BF16)
        r1 = g - g1.astype(F32)
        g2 = r1.astype(BF16)
        g3 = (r1 - g2.astype(F32)).astype(BF16)
        ctx["b"] = (_dot(tri, g1) + _dot(tri, g2) + _dot(tri, g3)) * LOG2_E

    def head(h):
        ri = lax.broadcasted_iota(jnp.int32, (half, half), 0)
        ci = lax.broadcasted_iota(jnp.int32, (half, half), 1)
        xm = jnp.where(ri > ci, ri ^ ci, 0)
        levels = [1 << i for i in range(half.bit_length() - 1)]
        masks = [(xm >= m) & (xm < 2 * m) for m in levels]
        odd_row = (lax.broadcasted_iota(jnp.int32, (t, HG_DIM), 0) & 1) == 1

        sl = slice(h * HG_DIM, (h + 1) * HG_DIM)
        q = hq_ref[rows, sl]
        v = hi_ref[rows, sl]
        kb = ctx["kk"][:, sl].astype(BF16)
        b = ctx["b"][:, sl]

        qk = []
        for m in levels:
            if m == 1:
                decay = jnp.where(odd_row, ctx["f"][:, sl], 1.0).astype(BF16)
            else:
                decay = jnp.exp2(_neg_gap(b, m)).astype(BF16)
            qk.append((q * decay, kb * decay))
        diag = []
        for q0 in (0, half):
            p = None
            for (q_l, k_l), mask in zip(qk, masks):
                s_l = lax.dot_general(q_l[q0:q0 + half], k_l[q0:q0 + half], NT_DIMS, preferred_element_type=F32)
                p = jnp.where(mask, s_l, 0.0 if p is None else p)
            diag.append(p.astype(BF16))
        mid = b[half - 1:half, :]
        cross = lax.dot_general(q[half:] * jnp.exp2(b[half:] - mid).astype(BF16),
                                kb[:half] * jnp.exp2(mid - b[:half]).astype(BF16), NT_DIMS,
                                preferred_element_type=F32).astype(BF16)
        o = jnp.concatenate([_dot(diag[0], v[:half]),
                             _dot(jnp.concatenate([cross, diag[1]], axis=1), v)], axis=0)
        o = o + jnp.sum((q * kb).astype(F32), axis=-1, keepdims=True) * v.astype(F32)

        st = st_ref[h]
        b_last = b[t - 1:t, :]
        o = o + lax.dot_general(q * jnp.exp2(b).astype(BF16), st.astype(BF16), NT_DIMS,
                                preferred_element_type=F32)
        k_end = kb * jnp.exp2(b_last - b).astype(BF16)
        st_ref[h] = st * jnp.exp2(b_last) + lax.dot_general(v, k_end, TN_DIMS, preferred_element_type=F32)

        gate = hg_ref[rows, sl].astype(F32)
        o_ref[rows, sl] = (_rms(o, gn_ref[h:h + 1, :]) * (gate * jax.nn.sigmoid(gate))).astype(o_ref.dtype)

    return [gates] + [functools.partial(head, h) for h in range(HG_HEADS)]


def _attn_kernel(q_ref, k_ref, v_ref, wg_ref, wu_ref, wd_ref, o_ref, wg_bf_ref, wu_bf_ref, wd_bf_ref,
                 vaug_ref, *, tq):
    wg_bf_ref[...] = wg_ref[...].astype(wg_bf_ref.dtype)
    wu_bf_ref[...] = wu_ref[...].astype(wu_bf_ref.dtype)
    wd_bf_ref[...] = wd_ref[...].astype(wd_bf_ref.dtype)

    seq = q_ref.shape[0]
    for h in range(MLA_HEADS):
        vaug_ref[h, :, 0:V_DIM] = v_ref[:, h * V_DIM:(h + 1) * V_DIM]
        vaug_ref[h, :, V_DIM:2 * V_DIM] = jnp.ones((seq, V_DIM), vaug_ref.dtype)
    row = lax.broadcasted_iota(jnp.int32, (tq, tq), 0)
    col = lax.broadcasted_iota(jnp.int32, (tq, tq), 1)

    def scores(job):
        h, i = job
        lo = i * tq
        cols = slice(h * HEAD_PAD, (h + 1) * HEAD_PAD)
        q = q_ref[lo:lo + tq, cols]
        s_d = lax.dot_general(q, k_ref[lo:lo + tq, cols], NT_DIMS, preferred_element_type=F32)
        s_d = jnp.where(row >= col, s_d, -jnp.inf)
        s_p = lax.dot_general(q, k_ref[0:lo, cols], NT_DIMS, preferred_element_type=F32) if i > 0 else None
        return s_d, s_p

    jobs = [(h, i) for i in range(seq // tq) for h in range(MLA_HEADS)]
    pending = [scores(job) for job in jobs[:ATTN_LOOKAHEAD]]
    for n_done, (h, i) in enumerate(jobs):
        lo = i * tq
        s_d, s_p = pending.pop(0)
        if n_done + ATTN_LOOKAHEAD < len(jobs):
            pending.append(scores(jobs[n_done + ATTN_LOOKAHEAD]))
        m = jnp.max(s_d, axis=-1, keepdims=True)
        if i > 0:
            m = jnp.maximum(m, jnp.max(s_p, axis=-1, keepdims=True))
        acc = _dot(jnp.exp2(s_d - m).astype(BF16), vaug_ref[h, lo:lo + tq, :])
        if i > 0:
            acc += _dot(jnp.exp2(s_p - m).astype(BF16), vaug_ref[h, 0:lo, :])
        o_ref[lo:lo + tq, h * V_DIM:(h + 1) * V_DIM] = (
            acc[:, 0:V_DIM] / acc[:, V_DIM:2 * V_DIM]).astype(o_ref.dtype)


def _postmix_kernel(h1_ref, ohg_ref, omla_ref, p_ref, wo_ref, g2_ref, wg_ref, wu_ref, wd_ref,
                    gple_ref, wpg_ref, wpp_ref, gfin_ref, y_ref):
    h2 = (h1_ref[...] + _dot(ohg_ref[...], wo_ref[0:HG_WIDTH, :])
          + _dot(omla_ref[...], wo_ref[HG_WIDTH:HG_WIDTH + MLA_HEADS * V_DIM, :]))
    h3 = h2 + 0.5 * _swiglu(_rms(h2, g2_ref[...]).astype(BF16), wg_ref, wu_ref, wd_ref)
    gate = jax.nn.sigmoid(_dot(_rms(h3, gple_ref[...]).astype(BF16), wpg_ref[...]))
    h4 = h3 + gate * _dot(p_ref[...].astype(BF16), wpp_ref[...])
    y_ref[...] = _rms(h4, gfin_ref[...])


def _resident(shape):
    return pl.BlockSpec(shape, lambda *_: (0,) * len(shape), pipeline_mode=pl.Buffered(1))


def _rows(tile, width):
    return pl.BlockSpec((tile, width), lambda i: (i, 0))


def kernel(x, p, positions, ln_ffn1, w1_gate, w1_up, w1_down, ln_mix, w_in, hg_lb_logits, hg_out_norm,
           q_a_norm, w_q_up, kv_a_norm, w_kv_up, w_out, ln_ffn2, w2_gate, w2_up, w2_down, ln_ple,
           w_ple_gate, w_ple_proj, ln_final):
    bsz, seq, _ = x.shape
    assert p.shape[0] == 1 and hg_lb_logits.shape[0] == 2, "single-layer trunk"
    n = bsz * seq
    tm = min(TOKEN_TILE, seq)
    tc = min(HGRN_CHUNK, seq)
    tq = min(ATTN_BLOCK, seq)
    tp = min(POSTMIX_TILE, n)
    assert seq % tm == 0 and tm % tc == 0 and seq % tq == 0 and n % tp == 0

    x2 = x.reshape(n, D_MODEL)
    pos2 = positions.astype(F32).reshape(n // tm, 1, tm)
    half = ROPE_DIM // 2
    inv_freq = ROPE_THETA ** (-jnp.arange(half, dtype=F32) / half)
    invf = jnp.concatenate([inv_freq, inv_freq, jnp.zeros((LANES - ROPE_DIM,), F32)]).reshape(1, LANES)
    wg1, wu1, wd1 = w1_gate[0].astype(BF16), w1_up[0].astype(BF16), w1_down[0].astype(BF16)
    win = w_in[0].astype(BF16)
    wq = jnp.pad(w_q_up[0].astype(BF16).reshape(Q_LORA, MLA_HEADS, QK_DIM),
                 ((0, 0), (0, 0), (0, HEAD_PAD - QK_DIM))).reshape(Q_LORA, MLA_HEADS * HEAD_PAD)
    wkv = w_kv_up[0].astype(BF16).reshape(KV_LORA, MLA_HEADS, NOPE_DIM + V_DIM)
    wk = wkv[:, :, :NOPE_DIM].reshape(KV_LORA, MLA_HEADS * NOPE_DIM)
    wv = wkv[:, :, NOPE_DIM:].reshape(KV_LORA, MLA_HEADS * V_DIM)
    wo = w_out[0].astype(BF16)
    row = lambda a: a.reshape(1, -1)
    layer0 = lambda w: w.reshape(w.shape[1:])

    steps = n // tm
    tile = lambda width: pl.BlockSpec((tm, width), lambda i: (jnp.minimum(i, steps - 1), 0))
    prev_tile = pl.BlockSpec((tm, HG_WIDTH), lambda i: (jnp.maximum(i - 1, 0), 0))
    h1, qp, kp, vv, o_hg = pl.pallas_call(
        functools.partial(_premix_kernel, tiles_per_row=seq // tm, chunk=tc),
        name="premix",
        grid=(steps + 1,),
        in_specs=[
            tile(D_MODEL), pl.BlockSpec((None, 1, tm), lambda i: (jnp.minimum(i, steps - 1), 0, 0)),
            _resident((1, LANES)), _resident((1, D_MODEL)),
            _resident((D_MODEL, D_FF)), _resident((D_MODEL, D_FF)), _resident((D_FF, D_MODEL)),
            _resident((1, D_MODEL)), _resident((D_MODEL, IN_WIDTH)),
            _resident((1, Q_LORA)), _resident((Q_LORA, MLA_HEADS * HEAD_PAD)), _resident((1, KV_LORA)),
            _resident((KV_LORA, MLA_HEADS * NOPE_DIM)), _resident((KV_LORA, MLA_HEADS * V_DIM)),
            _resident((2, HG_WIDTH)), _resident((HG_HEADS, HG_DIM)),
        ],
        out_specs=[tile(D_MODEL), tile(MLA_HEADS * HEAD_PAD), tile(MLA_HEADS * HEAD_PAD),
                   tile(MLA_HEADS * V_DIM), prev_tile],
        out_shape=[
            jax.ShapeDtypeStruct((n, D_MODEL), F32),
            jax.ShapeDtypeStruct((n, MLA_HEADS * HEAD_PAD), BF16),
            jax.ShapeDtypeStruct((n, MLA_HEADS * HEAD_PAD), BF16),
            jax.ShapeDtypeStruct((n, MLA_HEADS * V_DIM), BF16),
            jax.ShapeDtypeStruct((n, HG_WIDTH), BF16),
        ],
        scratch_shapes=[pltpu.VMEM((tm, HG_WIDTH), BF16), pltpu.VMEM((tm, HG_WIDTH), F32),
                        pltpu.VMEM((tm, HG_WIDTH), BF16), pltpu.VMEM((tm, HG_WIDTH), BF16),
                        pltpu.VMEM((HG_HEADS, HG_DIM, HG_DIM), F32)],
        compiler_params=pltpu.CompilerParams(dimension_semantics=("arbitrary",), vmem_limit_bytes=VMEM_LIMIT),
    )(x2, pos2, invf, row(ln_ffn1[0]), wg1, wu1, wd1, row(ln_mix[0]), win,
      row(q_a_norm[0]), wq, row(kv_a_norm[0]), wk, wv, hg_lb_logits, hg_out_norm[0])

    seq_spec = lambda width: pl.BlockSpec((seq, MLA_HEADS * width), lambda b: (b, 0))
    assert D_MODEL % (bsz * BF16_ROWS) == 0 and D_FF % (bsz * BF16_ROWS) == 0
    slab = lambda rows, cols: pl.BlockSpec((rows // bsz, cols), lambda b: (b, 0))
    o_mla, wg2, wu2, wd2 = pl.pallas_call(
        functools.partial(_attn_kernel, tq=tq),
        name="mla_attn",
        grid=(bsz,),
        in_specs=[seq_spec(HEAD_PAD), seq_spec(HEAD_PAD), seq_spec(V_DIM),
                  slab(D_MODEL, D_FF), slab(D_MODEL, D_FF), slab(D_FF, D_MODEL)],
        out_specs=[seq_spec(V_DIM), slab(D_MODEL, D_FF), slab(D_MODEL, D_FF), slab(D_FF, D_MODEL)],
        out_shape=[jax.ShapeDtypeStruct((n, MLA_HEADS * V_DIM), BF16),
                   jax.ShapeDtypeStruct((D_MODEL, D_FF), BF16), jax.ShapeDtypeStruct((D_MODEL, D_FF), BF16),
                   jax.ShapeDtypeStruct((D_FF, D_MODEL), BF16)],
        scratch_shapes=[pltpu.VMEM((MLA_HEADS, seq, 2 * V_DIM), BF16)],
        compiler_params=pltpu.CompilerParams(dimension_semantics=("parallel",), vmem_limit_bytes=VMEM_LIMIT),
    )(qp, kp, vv, layer0(w2_gate), layer0(w2_up), layer0(w2_down))

    y = pl.pallas_call(
        _postmix_kernel,
        name="postmix",
        grid=(n // tp,),
        in_specs=[
            _rows(tp, D_MODEL), _rows(tp, HG_WIDTH), _rows(tp, MLA_HEADS * V_DIM), _rows(tp, PLE_DIM),
            _resident((HG_WIDTH + MLA_HEADS * V_DIM, D_MODEL)), _resident((1, D_MODEL)),
            _resident((D_MODEL, D_FF)), _resident((D_MODEL, D_FF)), _resident((D_FF, D_MODEL)),
            _resident((1, D_MODEL)), _resident((D_MODEL, D_MODEL)), _resident((PLE_DIM, D_MODEL)),
            _resident((1, D_MODEL)),
        ],
        out_specs=_rows(tp, D_MODEL),
        out_shape=jax.ShapeDtypeStruct((n, D_MODEL), F32),
        compiler_params=pltpu.CompilerParams(dimension_semantics=("parallel",), vmem_limit_bytes=VMEM_LIMIT),
    )(h1, o_hg, o_mla, p[0].reshape(n, PLE_DIM), wo, row(ln_ffn2[0]),
      wg2, wu2, wd2, row(ln_ple[0]), w_ple_gate[0].astype(BF16), w_ple_proj[0].astype(BF16), row(ln_final))

    return y.reshape(bsz, seq, D_MODEL)
```

```python
import functools

import jax
import jax.numpy as jnp
from jax import lax
from jax.experimental import pallas as pl
from jax.experimental.pallas import tpu as pltpu

F32 = jnp.float32
BF16 = jnp.bfloat16

D_MODEL = 1024
D_FF = 2816
PLE_DIM = 256
HG_HEADS = 4
HG_DIM = 128
HG_WIDTH = HG_HEADS * HG_DIM
MLA_HEADS = 4
Q_LORA = 256
KV_LORA = 128
NOPE_DIM = 128
ROPE_DIM = 64
V_DIM = 128
QK_DIM = NOPE_DIM + ROPE_DIM
ROPE_THETA = 10000.0
EPS = 1e-6
LOG2_E = 1.4426950408889634

LANES = 128
SUBLANES = 8
BF16_ROWS = 2 * SUBLANES
HEAD_PAD = 2 * LANES
HG_GROUPS = 4
IN_WIDTH = HG_GROUPS * HG_WIDTH + Q_LORA + KV_LORA + ROPE_DIM

FFN_CHUNK = 256
TOKEN_TILE = 512
POSTMIX_TILE = 1024
HGRN_CHUNK = 256
ATTN_BLOCK = 256
ATTN_LOOKAHEAD = 3
POSTMIX_TAIL_AFTER_FFN_CHUNKS = (1, 4)
HGRN_AFTER_FFN_CHUNKS = (0, 1, 2, 3)
VMEM_LIMIT = 56 * 1024 * 1024

NT_DIMS = (((1,), (1,)), ((), ()))
TN_DIMS = (((0,), (0,)), ((), ()))


def _dot(a, b):
    return jnp.dot(a, b, preferred_element_type=F32)


def _rms(x, g):
    return x * lax.rsqrt(jnp.mean(x * x, axis=-1, keepdims=True) + EPS) * g


def _swiglu(xn, wg_ref, wu_ref, wd_ref, between=None):
    acc = None
    for idx, c in enumerate(range(0, D_FF, FFN_CHUNK)):
        g = _dot(xn, wg_ref[:, c:c + FFN_CHUNK])
        u = _dot(xn, wu_ref[:, c:c + FFN_CHUNK])
        a = (g * jax.nn.sigmoid(g) * u).astype(BF16)
        d = _dot(a, wd_ref[c:c + FFN_CHUNK, :])
        acc = d if acc is None else acc + d
        for piece in (between or {}).get(idx, ()):
            piece()
    return acc


def _rope(x, cos_t, sin_lo, sin_hi):
    return (x * cos_t + pltpu.roll(x, LANES - ROPE_DIM // 2, 1) * sin_lo
            + pltpu.roll(x, ROPE_DIM // 2, 1) * sin_hi)


def _premix_kernel(x_ref, pos_ref, invf_ref, g1_ref, wg_ref, wu_ref, wd_ref, gmix_ref, win_ref,
                   qan_ref, wq_ref, kvan_ref, wk_ref, wv_ref, lbl_ref, gn_ref,
                   h1_ref, qp_ref, kp_ref, v_ref, ohg_ref,
                   hq_s, hf_s, hi_s, hg_s, st_ref, *, tiles_per_row, chunk):
    i = pl.program_id(0)
    tm = x_ref.shape[0]

    @pl.when(i == 0)
    def _():
        hq_s[...] = jnp.zeros_like(hq_s)
        hf_s[...] = jnp.zeros_like(hf_s)
        hi_s[...] = jnp.zeros_like(hi_s)
        hg_s[...] = jnp.zeros_like(hg_s)
        st_ref[...] = jnp.zeros_like(st_ref)

    @pl.when(lax.rem(i + tiles_per_row - 1, tiles_per_row) == 0)
    def _():
        st_ref[...] = jnp.zeros_like(st_ref)

    pieces = []
    for r0 in range(0, tm, chunk):
        pieces += _hgrn_pieces(lbl_ref, hq_s, hf_s, hi_s, hg_s, gn_ref, ohg_ref, st_ref, r0, chunk)
    early_chunks = [c for c in HGRN_AFTER_FFN_CHUNKS if c < D_FF // FFN_CHUNK][:len(pieces)]
    early = {c: [piece] for c, piece in zip(early_chunks, pieces)}
    late = iter(pieces[len(early_chunks):])

    t_row = lax.broadcasted_iota(jnp.int32, (tm, tm), 0)
    t_col = lax.broadcasted_iota(jnp.int32, (tm, tm), 1)
    pos_col = jnp.sum(jnp.where(t_row == t_col, pos_ref[...], 0.0), axis=1, keepdims=True)
    ang = pos_col * invf_ref[...]
    lane = lax.broadcasted_iota(jnp.int32, ang.shape, 1)
    cos_a, sin_a = jnp.cos(ang), jnp.sin(ang)
    half = ROPE_DIM // 2
    cos_t = jnp.where(lane < ROPE_DIM, cos_a, 0.0)
    sin_lo = jnp.where(lane < half, -sin_a, 0.0)
    sin_hi = jnp.where((lane >= half) & (lane < ROPE_DIM), sin_a, 0.0)

    x = x_ref[...]
    xn = _rms(x, g1_ref[...]).astype(BF16)
    h1 = x + 0.5 * _swiglu(xn, wg_ref, wu_ref, wd_ref, early)
    h1_ref[...] = h1

    hn = _rms(h1, gmix_ref[...]).astype(BF16)
    w = HG_WIDTH
    new = []
    for j in range(HG_GROUPS):
        new.append(_dot(hn, win_ref[:, j * w:(j + 1) * w]))
        piece = next(late, None)
        if piece is not None:
            piece()
    for piece in late:
        piece()
    c0 = HG_GROUPS * w
    cq = _dot(hn, win_ref[:, c0:c0 + Q_LORA])
    ckv_kr = _dot(hn, win_ref[:, c0 + Q_LORA:IN_WIDTH])
    ckv = ckv_kr[:, 0:KV_LORA]
    kr = jnp.concatenate([ckv_kr[:, KV_LORA:KV_LORA + ROPE_DIM],
                          jnp.zeros((tm, LANES - ROPE_DIM), F32)], axis=1)

    q = _dot(_rms(cq, qan_ref[...]).astype(BF16), wq_ref[...]) * (QK_DIM ** -0.5 * LOG2_E)
    ckvn = _rms(ckv, kvan_ref[...]).astype(BF16)
    k_nope = _dot(ckvn, wk_ref[...])
    v_ref[...] = _dot(ckvn, wv_ref[...]).astype(v_ref.dtype)
    k_rope = _rope(kr, cos_t, sin_lo, sin_hi).astype(kp_ref.dtype)
    for h in range(MLA_HEADS):
        a = h * HEAD_PAD
        qp_ref[:, a:a + NOPE_DIM] = q[:, a:a + NOPE_DIM].astype(qp_ref.dtype)
        qp_ref[:, a + NOPE_DIM:a + HEAD_PAD] = _rope(
            q[:, a + NOPE_DIM:a + HEAD_PAD], cos_t, sin_lo, sin_hi).astype(qp_ref.dtype)
        kp_ref[:, a:a + NOPE_DIM] = k_nope[:, h * NOPE_DIM:(h + 1) * NOPE_DIM].astype(kp_ref.dtype)
        kp_ref[:, a + NOPE_DIM:a + HEAD_PAD] = k_rope

    hq_s[...] = new[0].astype(hq_s.dtype)
    hf_s[...] = new[1]
    hi_s[...] = new[2].astype(hi_s.dtype)
    hg_s[...] = new[3].astype(hg_s.dtype)


def _neg_gap(b, m):
    t = b.shape[0]
    if m >= SUBLANES:
        pieces = []
        for s in range(0, t, 2 * m):
            mid = b[s + m - 1:s + m, :]
            pieces += [mid - b[s:s + m], b[s + m:s + 2 * m] - mid]
        return jnp.concatenate(pieces, axis=0)
    b3 = b.reshape(t // SUBLANES, SUBLANES, LANES)
    sub = lax.broadcasted_iota(jnp.int32, b3.shape, 1)

    def row(i):
        return jnp.broadcast_to(b3[:, i:i + 1, :], b3.shape)

    mid = row(m - 1)
    for s in range(2 * m, SUBLANES, 2 * m):
        mid = jnp.where(sub >= s, row(s + m - 1), mid)
    return -jnp.abs(b3 - mid).reshape(t, LANES)


def _hgrn_pieces(lbl_ref, hq_ref, hf_ref, hi_ref, hg_ref, gn_ref, o_ref, st_ref, r0, t):
    rows = slice(r0, r0 + t)
    half = t // 2
    ctx = {}

    def gates():
        lg = lbl_ref[...]
        e = jnp.exp(lg - jnp.max(lg, axis=0, keepdims=True))
        lb = e[0:1, :] / jnp.sum(e, axis=0, keepdims=True)
        f_raw = hf_ref[rows, :]
        sig = jax.nn.sigmoid(f_raw)
        f = lb + (1.0 - lb) * sig
        g = jnp.log(f)
        ctx["f"] = f
        ctx["kk"] = (1.0 - lb) * (1.0 - sig)

        row = lax.broadcasted_iota(jnp.int32, (t, t), 0)
        col = lax.broadcasted_iota(jnp.int32, (t, t), 1)
        tri = (row >= col).astype(BF16)
        g1 = g.astype(BF16)
        r1 = g - g1.astype(F32)
        g2 = r1.astype(BF16)
        g3 = (r1 - g2.astype(F32)).astype(BF16)
        ctx["b"] = (_dot(tri, g1) + _dot(tri, g2) + _dot(tri, g3)) * LOG2_E

    def levels():
        ri = lax.broadcasted_iota(jnp.int32, (half, half), 0)
        ci = lax.broadcasted_iota(jnp.int32, (half, half), 1)
        xm = jnp.where(ri > ci, ri ^ ci, 0)
        odd_row = (lax.broadcasted_iota(jnp.int32, (t, HG_DIM), 0) & 1) == 1

        sls = [slice(h * HG_DIM, (h + 1) * HG_DIM) for h in range(HG_HEADS)]
        ctx["q"] = [hq_ref[rows, sl] for sl in sls]
        ctx["v"] = [hi_ref[rows, sl] for sl in sls]
        ctx["kb"] = [ctx["kk"][:, sl].astype(BF16) for sl in sls]
        ctx["bh"] = [ctx["b"][:, sl] for sl in sls]

        ps = [[None, None] for _ in sls]
        for i in range(half.bit_length() - 1):
            m = 1 << i
            mask = (xm >= m) & (xm < 2 * m)
            for h, sl in enumerate(sls):
                if m == 1:
                    decay = jnp.where(odd_row, ctx["f"][:, sl], 1.0).astype(BF16)
                else:
                    decay = jnp.exp2(_neg_gap(ctx["bh"][h], m)).astype(BF16)
                q_l, k_l = ctx["q"][h] * decay, ctx["kb"][h] * decay
                for qi, q0 in enumerate((0, half)):
                    s_l = lax.dot_general(q_l[q0:q0 + half], k_l[q0:q0 + half], NT_DIMS,
                                          preferred_element_type=F32)
                    ps[h][qi] = jnp.where(mask, s_l, 0.0 if ps[h][qi] is None else ps[h][qi])
        ctx["diag"] = [[p.astype(BF16) for p in pair] for pair in ps]

    def head(h):
        sl = slice(h * HG_DIM, (h + 1) * HG_DIM)
        q, v, kb, b, diag = ctx["q"][h], ctx["v"][h], ctx["kb"][h], ctx["bh"][h], ctx["diag"][h]
        mid = b[half - 1:half, :]
        cross = lax.dot_general(q[half:] * jnp.exp2(b[half:] - mid).astype(BF16),
                                kb[:half] * jnp.exp2(mid - b[:half]).astype(BF16), NT_DIMS,
                                preferred_element_type=F32).astype(BF16)
        o = jnp.concatenate([_dot(diag[0], v[:half]),
                             _dot(jnp.concatenate([cross, diag[1]], axis=1), v)], axis=0)
        o = o + jnp.sum((q * kb).astype(F32), axis=-1, keepdims=True) * v.astype(F32)

        st = st_ref[h]
        b_last = b[t - 1:t, :]
        o = o + lax.dot_general(q * jnp.exp2(b).astype(BF16), st.astype(BF16), NT_DIMS,
                                preferred_element_type=F32)
        k_end = kb * jnp.exp2(b_last - b).astype(BF16)
        st_ref[h] = st * jnp.exp2(b_last) + lax.dot_general(v, k_end, TN_DIMS, preferred_element_type=F32)

        gate = hg_ref[rows, sl].astype(F32)
        o_ref[rows, sl] = (_rms(o, gn_ref[h:h + 1, :]) * (gate * jax.nn.sigmoid(gate))).astype(o_ref.dtype)

    def heads():
        levels()
        for h in range(HG_HEADS):
            head(h)

    return [gates, heads]


def _attn_kernel(q_ref, k_ref, v_ref, wg_ref, wu_ref, wd_ref, o_ref, wg_bf_ref, wu_bf_ref, wd_bf_ref,
                 vaug_ref, *, tq):
    wg_bf_ref[...] = wg_ref[...].astype(wg_bf_ref.dtype)
    wu_bf_ref[...] = wu_ref[...].astype(wu_bf_ref.dtype)
    wd_bf_ref[...] = wd_ref[...].astype(wd_bf_ref.dtype)

    seq = q_ref.shape[0]
    for h in range(MLA_HEADS):
        vaug_ref[h, :, 0:V_DIM] = v_ref[:, h * V_DIM:(h + 1) * V_DIM]
        vaug_ref[h, :, V_DIM:2 * V_DIM] = jnp.ones((seq, V_DIM), vaug_ref.dtype)
    row = lax.broadcasted_iota(jnp.int32, (tq, tq), 0)
    col = lax.broadcasted_iota(jnp.int32, (tq, tq), 1)

    def scores(job):
        h, i = job
        lo = i * tq
        cols = slice(h * HEAD_PAD, (h + 1) * HEAD_PAD)
        q = q_ref[lo:lo + tq, cols]
        s_d = lax.dot_general(q, k_ref[lo:lo + tq, cols], NT_DIMS, preferred_element_type=F32)
        s_d = jnp.where(row >= col, s_d, -jnp.inf)
        s_p = lax.dot_general(q, k_ref[0:lo, cols], NT_DIMS, preferred_element_type=F32) if i > 0 else None
        return s_d, s_p

    jobs = [(h, i) for i in range(seq // tq) for h in range(MLA_HEADS)]
    pending = [scores(job) for job in jobs[:ATTN_LOOKAHEAD]]
    for n_done, (h, i) in enumerate(jobs):
        lo = i * tq
        s_d, s_p = pending.pop(0)
        if n_done + ATTN_LOOKAHEAD < len(jobs):
            pending.append(scores(jobs[n_done + ATTN_LOOKAHEAD]))
        m = jnp.max(s_d, axis=-1, keepdims=True)
        if i > 0:
            m = jnp.maximum(m, jnp.max(s_p, axis=-1, keepdims=True))
        acc = _dot(jnp.exp2(s_d - m).astype(BF16), vaug_ref[h, lo:lo + tq, :])
        if i > 0:
            acc += _dot(jnp.exp2(s_p - m).astype(BF16), vaug_ref[h, 0:lo, :])
        o_ref[lo:lo + tq, h * V_DIM:(h + 1) * V_DIM] = (
            acc[:, 0:V_DIM] / acc[:, V_DIM:2 * V_DIM]).astype(o_ref.dtype)


def _postmix_kernel(h1_ref, ohg_ref, omla_ref, p_ref, wo_ref, g2_ref, wg_ref, wu_ref, wd_ref,
                    gple_ref, wpg_ref, wpp_ref, gfin_ref, y_ref):
    tp = h1_ref.shape[0]
    parts = [slice(r, r + TOKEN_TILE) for r in range(0, tp, TOKEN_TILE)] if tp % TOKEN_TILE == 0 else [slice(0, tp)]

    def tail_pieces(rows, h3):
        ctx = {}

        def gate():
            ctx["gate"] = jax.nn.sigmoid(_dot(_rms(h3, gple_ref[...]).astype(BF16), wpg_ref[...]))

        def finish():
            h4 = h3 + ctx["gate"] * _dot(p_ref[rows, :].astype(BF16), wpp_ref[...])
            y_ref[rows, :] = _rms(h4, gfin_ref[...])

        return [gate, finish]

    pending = []
    for rows in parts:
        h2 = (h1_ref[rows, :] + _dot(ohg_ref[rows, :], wo_ref[0:HG_WIDTH, :])
              + _dot(omla_ref[rows, :], wo_ref[HG_WIDTH:HG_WIDTH + MLA_HEADS * V_DIM, :]))
        slots = [c for c in POSTMIX_TAIL_AFTER_FFN_CHUNKS if c < D_FF // FFN_CHUNK][:len(pending)]
        between = {c: [piece] for c, piece in zip(slots, pending)}
        h3 = h2 + 0.5 * _swiglu(_rms(h2, g2_ref[...]).astype(BF16), wg_ref, wu_ref, wd_ref, between)
        for piece in pending[len(slots):]:
            piece()
        pending = tail_pieces(rows, h3)
    for piece in pending:
        piece()


def _resident(shape):
    return pl.BlockSpec(shape, lambda *_: (0,) * len(shape), pipeline_mode=pl.Buffered(1))


def _rows(tile, width):
    return pl.BlockSpec((tile, width), lambda i: (i, 0))


def kernel(x, p, positions, ln_ffn1, w1_gate, w1_up, w1_down, ln_mix, w_in, hg_lb_logits, hg_out_norm,
           q_a_norm, w_q_up, kv_a_norm, w_kv_up, w_out, ln_ffn2, w2_gate, w2_up, w2_down, ln_ple,
           w_ple_gate, w_ple_proj, ln_final):
    bsz, seq, _ = x.shape
    assert p.shape[0] == 1 and hg_lb_logits.shape[0] == 2, "single-layer trunk"
    n = bsz * seq
    tm = min(TOKEN_TILE, seq)
    tc = min(HGRN_CHUNK, seq)
    tq = min(ATTN_BLOCK, seq)
    tp = min(POSTMIX_TILE, n)
    assert seq % tm == 0 and tm % tc == 0 and seq % tq == 0 and n % tp == 0

    x2 = x.reshape(n, D_MODEL)
    pos2 = positions.astype(F32).reshape(n // tm, 1, tm)
    half = ROPE_DIM // 2
    inv_freq = ROPE_THETA ** (-jnp.arange(half, dtype=F32) / half)
    invf = jnp.concatenate([inv_freq, inv_freq, jnp.zeros((LANES - ROPE_DIM,), F32)]).reshape(1, LANES)
    wg1, wu1, wd1 = w1_gate[0].astype(BF16), w1_up[0].astype(BF16), w1_down[0].astype(BF16)
    win = w_in[0].astype(BF16)
    wq = jnp.pad(w_q_up[0].astype(BF16).reshape(Q_LORA, MLA_HEADS, QK_DIM),
                 ((0, 0), (0, 0), (0, HEAD_PAD - QK_DIM))).reshape(Q_LORA, MLA_HEADS * HEAD_PAD)
    wkv = w_kv_up[0].astype(BF16).reshape(KV_LORA, MLA_HEADS, NOPE_DIM + V_DIM)
    wk = wkv[:, :, :NOPE_DIM].reshape(KV_LORA, MLA_HEADS * NOPE_DIM)
    wv = wkv[:, :, NOPE_DIM:].reshape(KV_LORA, MLA_HEADS * V_DIM)
    wo = w_out[0].astype(BF16)
    row = lambda a: a.reshape(1, -1)
    layer0 = lambda w: w.reshape(w.shape[1:])

    steps = n // tm
    tile = lambda width: pl.BlockSpec((tm, width), lambda i: (jnp.minimum(i, steps - 1), 0))
    prev_tile = pl.BlockSpec((tm, HG_WIDTH), lambda i: (jnp.maximum(i - 1, 0), 0))
    h1, qp, kp, vv, o_hg = pl.pallas_call(
        functools.partial(_premix_kernel, tiles_per_row=seq // tm, chunk=tc),
        name="premix",
        grid=(steps + 1,),
        in_specs=[
            tile(D_MODEL), pl.BlockSpec((None, 1, tm), lambda i: (jnp.minimum(i, steps - 1), 0, 0)),
            _resident((1, LANES)), _resident((1, D_MODEL)),
            _resident((D_MODEL, D_FF)), _resident((D_MODEL, D_FF)), _resident((D_FF, D_MODEL)),
            _resident((1, D_MODEL)), _resident((D_MODEL, IN_WIDTH)),
            _resident((1, Q_LORA)), _resident((Q_LORA, MLA_HEADS * HEAD_PAD)), _resident((1, KV_LORA)),
            _resident((KV_LORA, MLA_HEADS * NOPE_DIM)), _resident((KV_LORA, MLA_HEADS * V_DIM)),
            _resident((2, HG_WIDTH)), _resident((HG_HEADS, HG_DIM)),
        ],
        out_specs=[tile(D_MODEL), tile(MLA_HEADS * HEAD_PAD), tile(MLA_HEADS * HEAD_PAD),
                   tile(MLA_HEADS * V_DIM), prev_tile],
        out_shape=[
            jax.ShapeDtypeStruct((n, D_MODEL), F32),
            jax.ShapeDtypeStruct((n, MLA_HEADS * HEAD_PAD), BF16),
            jax.ShapeDtypeStruct((n, MLA_HEADS * HEAD_PAD), BF16),
            jax.ShapeDtypeStruct((n, MLA_HEADS * V_DIM), BF16),
            jax.ShapeDtypeStruct((n, HG_WIDTH), BF16),
        ],
        scratch_shapes=[pltpu.VMEM((tm, HG_WIDTH), BF16), pltpu.VMEM((tm, HG_WIDTH), F32),
                        pltpu.VMEM((tm, HG_WIDTH), BF16), pltpu.VMEM((tm, HG_WIDTH), BF16),
                        pltpu.VMEM((HG_HEADS, HG_DIM, HG_DIM), F32)],
        compiler_params=pltpu.CompilerParams(dimension_semantics=("arbitrary",), vmem_limit_bytes=VMEM_LIMIT),
    )(x2, pos2, invf, row(ln_ffn1[0]), wg1, wu1, wd1, row(ln_mix[0]), win,
      row(q_a_norm[0]), wq, row(kv_a_norm[0]), wk, wv, hg_lb_logits, hg_out_norm[0])

    seq_spec = lambda width: pl.BlockSpec((seq, MLA_HEADS * width), lambda b: (b, 0))
    assert D_MODEL % (bsz * BF16_ROWS) == 0 and D_FF % (bsz * BF16_ROWS) == 0
    slab = lambda rows, cols: pl.BlockSpec((rows // bsz, cols), lambda b: (b, 0))
    o_mla, wg2, wu2, wd2 = pl.pallas_call(
        functools.partial(_attn_kernel, tq=tq),
        name="mla_attn",
        grid=(bsz,),
        in_specs=[seq_spec(HEAD_PAD), seq_spec(HEAD_PAD), seq_spec(V_DIM),
                  slab(D_MODEL, D_FF), slab(D_MODEL, D_FF), slab(D_FF, D_MODEL)],
        out_specs=[seq_spec(V_DIM), slab(D_MODEL, D_FF), slab(D_MODEL, D_FF), slab(D_FF, D_MODEL)],
        out_shape=[jax.ShapeDtypeStruct((n, MLA_HEADS * V_DIM), BF16),
                   jax.ShapeDtypeStruct((D_MODEL, D_FF), BF16), jax.ShapeDtypeStruct((D_MODEL, D_FF), BF16),
                   jax.ShapeDtypeStruct((D_FF, D_MODEL), BF16)],
        scratch_shapes=[pltpu.VMEM((MLA_HEADS, seq, 2 * V_DIM), BF16)],
        compiler_params=pltpu.CompilerParams(dimension_semantics=("parallel",), vmem_limit_bytes=VMEM_LIMIT),
    )(qp, kp, vv, layer0(w2_gate), layer0(w2_up), layer0(w2_down))

    y = pl.pallas_call(
        _postmix_kernel,
        name="postmix",
        grid=(n // tp,),
        in_specs=[
            _rows(tp, D_MODEL), _rows(tp, HG_WIDTH), _rows(tp, MLA_HEADS * V_DIM), _rows(tp, PLE_DIM),
            _resident((HG_WIDTH + MLA_HEADS * V_DIM, D_MODEL)), _resident((1, D_MODEL)),
            _resident((D_MODEL, D_FF)), _resident((D_MODEL, D_FF)), _resident((D_FF, D_MODEL)),
            _resident((1, D_MODEL)), _resident((D_MODEL, D_MODEL)), _resident((PLE_DIM, D_MODEL)),
            _resident((1, D_MODEL)),
        ],
        out_specs=_rows(tp, D_MODEL),
        out_shape=jax.ShapeDtypeStruct((n, D_MODEL), F32),
        compiler_params=pltpu.CompilerParams(dimension_semantics=("parallel",), vmem_limit_bytes=VMEM_LIMIT),
    )(h1, o_hg, o_mla, p[0].reshape(n, PLE_DIM), wo, row(ln_ffn2[0]),
      wg2, wu2, wd2, row(ln_ple[0]), w_ple_gate[0].astype(BF16), w_ple_proj[0].astype(BF16), row(ln_final))

    return y.reshape(bsz, seq, D_MODEL)
```

```python
import functools

import jax
import jax.numpy as jnp
from jax import lax
from jax.experimental import pallas as pl
from jax.experimental.pallas import tpu as pltpu

F32 = jnp.float32
BF16 = jnp.bfloat16

D_MODEL = 1024
D_FF = 2816
PLE_DIM = 256
HG_HEADS = 4
HG_DIM = 128
HG_WIDTH = HG_HEADS * HG_DIM
MLA_HEADS = 4
Q_LORA = 256
KV_LORA = 128
NOPE_DIM = 128
ROPE_DIM = 64
V_DIM = 128
QK_DIM = NOPE_DIM + ROPE_DIM
ROPE_THETA = 10000.0
EPS = 1e-6
LOG2_E = 1.4426950408889634

LANES = 128
SUBLANES = 8
BF16_ROWS = 2 * SUBLANES
HEAD_PAD = 2 * LANES
HG_GROUPS = 4
IN_WIDTH = HG_GROUPS * HG_WIDTH + Q_LORA + KV_LORA + ROPE_DIM

FFN_CHUNK = 256
TOKEN_TILE = 512
POSTMIX_TILE = 1024
HGRN_CHUNK = 256
ATTN_BLOCK = 256
ATTN_LOOKAHEAD = 3
POSTMIX_TAIL_AFTER_FFN_CHUNKS = (1, 4)
HGRN_AFTER_FFN_CHUNKS = ()
VMEM_LIMIT = 56 * 1024 * 1024

NT_DIMS = (((1,), (1,)), ((), ()))
TN_DIMS = (((0,), (0,)), ((), ()))


def _dot(a, b):
    return jnp.dot(a, b, preferred_element_type=F32)


def _rms(x, g):
    return x * lax.rsqrt(jnp.mean(x * x, axis=-1, keepdims=True) + EPS) * g


def _swiglu(xn, wg_ref, wu_ref, wd_ref, between=None):
    acc = None
    for idx, c in enumerate(range(0, D_FF, FFN_CHUNK)):
        g = _dot(xn, wg_ref[:, c:c + FFN_CHUNK])
        u = _dot(xn, wu_ref[:, c:c + FFN_CHUNK])
        a = (g * jax.nn.sigmoid(g) * u).astype(BF16)
        d = _dot(a, wd_ref[c:c + FFN_CHUNK, :])
        acc = d if acc is None else acc + d
        for piece in (between or {}).get(idx, ()):
            piece()
    return acc


def _rope(x, cos_t, sin_lo, sin_hi):
    return (x * cos_t + pltpu.roll(x, LANES - ROPE_DIM // 2, 1) * sin_lo
            + pltpu.roll(x, ROPE_DIM // 2, 1) * sin_hi)


def _premix_kernel(x_ref, pos_ref, invf_ref, g1_ref, wg_ref, wu_ref, wd_ref, gmix_ref, win_ref,
                   qan_ref, wq_ref, kvan_ref, wk_ref, wv_ref, lbl_ref, gn_ref,
                   h1_ref, qp_ref, kp_ref, v_ref, ohg_ref,
                   hq_s, hf_s, hi_s, hg_s, st_ref, *, tiles_per_row, chunk):
    i = pl.program_id(0)
    tm = x_ref.shape[0]

    @pl.when(i == 0)
    def _():
        hq_s[...] = jnp.zeros_like(hq_s)
        hf_s[...] = jnp.zeros_like(hf_s)
        hi_s[...] = jnp.zeros_like(hi_s)
        hg_s[...] = jnp.zeros_like(hg_s)
        st_ref[...] = jnp.zeros_like(st_ref)

    @pl.when(lax.rem(i + tiles_per_row - 1, tiles_per_row) == 0)
    def _():
        st_ref[...] = jnp.zeros_like(st_ref)

    pieces = []
    for r0 in range(0, tm, chunk):
        pieces += _hgrn_pieces(lbl_ref, hq_s, hf_s, hi_s, hg_s, gn_ref, ohg_ref, st_ref, r0, chunk)
    early_chunks = [c for c in HGRN_AFTER_FFN_CHUNKS if c < D_FF // FFN_CHUNK][:len(pieces)]
    early = {}
    for c, piece in zip(early_chunks, pieces):
        early.setdefault(c, []).append(piece)
    late = iter(pieces[len(early_chunks):])

    t_row = lax.broadcasted_iota(jnp.int32, (tm, tm), 0)
    t_col = lax.broadcasted_iota(jnp.int32, (tm, tm), 1)
    pos_col = jnp.sum(jnp.where(t_row == t_col, pos_ref[...], 0.0), axis=1, keepdims=True)
    ang = pos_col * invf_ref[...]
    lane = lax.broadcasted_iota(jnp.int32, ang.shape, 1)
    cos_a, sin_a = jnp.cos(ang), jnp.sin(ang)
    half = ROPE_DIM // 2
    cos_t = jnp.where(lane < ROPE_DIM, cos_a, 0.0)
    sin_lo = jnp.where(lane < half, -sin_a, 0.0)
    sin_hi = jnp.where((lane >= half) & (lane < ROPE_DIM), sin_a, 0.0)

    x = x_ref[...]
    xn = _rms(x, g1_ref[...]).astype(BF16)
    h1 = x + 0.5 * _swiglu(xn, wg_ref, wu_ref, wd_ref, early)
    h1_ref[...] = h1

    hn = _rms(h1, gmix_ref[...]).astype(BF16)
    w = HG_WIDTH
    new = []
    for j in range(HG_GROUPS):
        new.append(_dot(hn, win_ref[:, j * w:(j + 1) * w]))
        piece = next(late, None)
        if piece is not None:
            piece()
    for piece in late:
        piece()
    c0 = HG_GROUPS * w
    cq = _dot(hn, win_ref[:, c0:c0 + Q_LORA])
    ckv_kr = _dot(hn, win_ref[:, c0 + Q_LORA:IN_WIDTH])
    ckv = ckv_kr[:, 0:KV_LORA]
    kr = jnp.concatenate([ckv_kr[:, KV_LORA:KV_LORA + ROPE_DIM],
                          jnp.zeros((tm, LANES - ROPE_DIM), F32)], axis=1)

    q = _dot(_rms(cq, qan_ref[...]).astype(BF16), wq_ref[...]) * (QK_DIM ** -0.5 * LOG2_E)
    ckvn = _rms(ckv, kvan_ref[...]).astype(BF16)
    k_nope = _dot(ckvn, wk_ref[...])
    v_ref[...] = _dot(ckvn, wv_ref[...]).astype(v_ref.dtype)
    k_rope = _rope(kr, cos_t, sin_lo, sin_hi).astype(kp_ref.dtype)
    for h in range(MLA_HEADS):
        a = h * HEAD_PAD
        qp_ref[:, a:a + NOPE_DIM] = q[:, a:a + NOPE_DIM].astype(qp_ref.dtype)
        qp_ref[:, a + NOPE_DIM:a + HEAD_PAD] = _rope(
            q[:, a + NOPE_DIM:a + HEAD_PAD], cos_t, sin_lo, sin_hi).astype(qp_ref.dtype)
        kp_ref[:, a:a + NOPE_DIM] = k_nope[:, h * NOPE_DIM:(h + 1) * NOPE_DIM].astype(kp_ref.dtype)
        kp_ref[:, a + NOPE_DIM:a + HEAD_PAD] = k_rope

    hq_s[...] = new[0].astype(hq_s.dtype)
    hf_s[...] = new[1]
    hi_s[...] = new[2].astype(hi_s.dtype)
    hg_s[...] = new[3].astype(hg_s.dtype)


def _neg_gap(b, m):
    t = b.shape[0]
    if m >= SUBLANES:
        pieces = []
        for s in range(0, t, 2 * m):
            mid = b[s + m - 1:s + m, :]
            pieces += [mid - b[s:s + m], b[s + m:s + 2 * m] - mid]
        return jnp.concatenate(pieces, axis=0)
    b3 = b.reshape(t // SUBLANES, SUBLANES, LANES)
    sub = lax.broadcasted_iota(jnp.int32, b3.shape, 1)

    def row(i):
        return jnp.broadcast_to(b3[:, i:i + 1, :], b3.shape)

    mid = row(m - 1)
    for s in range(2 * m, SUBLANES, 2 * m):
        mid = jnp.where(sub >= s, row(s + m - 1), mid)
    return -jnp.abs(b3 - mid).reshape(t, LANES)


def _hgrn_pieces(lbl_ref, hq_ref, hf_ref, hi_ref, hg_ref, gn_ref, o_ref, st_ref, r0, t):
    rows = slice(r0, r0 + t)
    half = t // 2
    ctx = {}

    def gates():
        lg = lbl_ref[...]
        e = jnp.exp(lg - jnp.max(lg, axis=0, keepdims=True))
        lb = e[0:1, :] / jnp.sum(e, axis=0, keepdims=True)
        f_raw = hf_ref[rows, :]
        sig = jax.nn.sigmoid(f_raw)
        f = lb + (1.0 - lb) * sig
        g = jnp.log(f)
        ctx["f"] = f
        ctx["kk"] = (1.0 - lb) * (1.0 - sig)

        row = lax.broadcasted_iota(jnp.int32, (t, t), 0)
        col = lax.broadcasted_iota(jnp.int32, (t, t), 1)
        tri = (row >= col).astype(BF16)
        g1 = g.astype(BF16)
        r1 = g - g1.astype(F32)
        g2 = r1.astype(BF16)
        g3 = (r1 - g2.astype(F32)).astype(BF16)
        ctx["b"] = (_dot(tri, g1) + _dot(tri, g2) + _dot(tri, g3)) * LOG2_E

    def levels():
        ri = lax.broadcasted_iota(jnp.int32, (half, half), 0)
        ci = lax.broadcasted_iota(jnp.int32, (half, half), 1)
        xm = jnp.where(ri > ci, ri ^ ci, 0)
        odd_row = (lax.broadcasted_iota(jnp.int32, (t, HG_DIM), 0) & 1) == 1

        sls = [slice(h * HG_DIM, (h + 1) * HG_DIM) for h in range(HG_HEADS)]
        ctx["q"] = [hq_ref[rows, sl] for sl in sls]
        ctx["v"] = [hi_ref[rows, sl] for sl in sls]
        ctx["kb"] = [ctx["kk"][:, sl].astype(BF16) for sl in sls]
        ctx["bh"] = [ctx["b"][:, sl] for sl in sls]

        ps = [[None, None] for _ in sls]
        for i in range(half.bit_length() - 1):
            m = 1 << i
            mask = (xm >= m) & (xm < 2 * m)
            for h, sl in enumerate(sls):
                if m == 1:
                    decay = jnp.where(odd_row, ctx["f"][:, sl], 1.0).astype(BF16)
                else:
                    decay = jnp.exp2(_neg_gap(ctx["bh"][h], m)).astype(BF16)
                q_l, k_l = ctx["q"][h] * decay, ctx["kb"][h] * decay
                for qi, q0 in enumerate((0, half)):
                    s_l = lax.dot_general(q_l[q0:q0 + half], k_l[q0:q0 + half], NT_DIMS,
                                          preferred_element_type=F32)
                    ps[h][qi] = jnp.where(mask, s_l, 0.0 if ps[h][qi] is None else ps[h][qi])
        ctx["diag"] = [[p.astype(BF16) for p in pair] for pair in ps]

    def head(h):
        sl = slice(h * HG_DIM, (h + 1) * HG_DIM)
        q, v, kb, b, diag = ctx["q"][h], ctx["v"][h], ctx["kb"][h], ctx["bh"][h], ctx["diag"][h]
        mid = b[half - 1:half, :]
        cross = lax.dot_general(q[half:] * jnp.exp2(b[half:] - mid).astype(BF16),
                                kb[:half] * jnp.exp2(mid - b[:half]).astype(BF16), NT_DIMS,
                                preferred_element_type=F32).astype(BF16)
        o = jnp.concatenate([_dot(diag[0], v[:half]),
                             _dot(jnp.concatenate([cross, diag[1]], axis=1), v)], axis=0)
        o = o + jnp.sum((q * kb).astype(F32), axis=-1, keepdims=True) * v.astype(F32)

        st = st_ref[h]
        b_last = b[t - 1:t, :]
        o = o + lax.dot_general(q * jnp.exp2(b).astype(BF16), st.astype(BF16), NT_DIMS,
                                preferred_element_type=F32)
        k_end = kb * jnp.exp2(b_last - b).astype(BF16)
        st_ref[h] = st * jnp.exp2(b_last) + lax.dot_general(v, k_end, TN_DIMS, preferred_element_type=F32)

        gate = hg_ref[rows, sl].astype(F32)
        o_ref[rows, sl] = (_rms(o, gn_ref[h:h + 1, :]) * (gate * jax.nn.sigmoid(gate))).astype(o_ref.dtype)

    def heads():
        levels()
        for h in range(HG_HEADS):
            head(h)

    return [gates, heads]


def _attn_kernel(q_ref, k_ref, v_ref, wg_ref, wu_ref, wd_ref, o_ref, wg_bf_ref, wu_bf_ref, wd_bf_ref,
                 vaug_ref, *, tq):
    wg_bf_ref[...] = wg_ref[...].astype(wg_bf_ref.dtype)
    wu_bf_ref[...] = wu_ref[...].astype(wu_bf_ref.dtype)
    wd_bf_ref[...] = wd_ref[...].astype(wd_bf_ref.dtype)

    seq = q_ref.shape[0]
    for h in range(MLA_HEADS):
        vaug_ref[h, :, 0:V_DIM] = v_ref[:, h * V_DIM:(h + 1) * V_DIM]
        vaug_ref[h, :, V_DIM:2 * V_DIM] = jnp.ones((seq, V_DIM), vaug_ref.dtype)
    row = lax.broadcasted_iota(jnp.int32, (tq, tq), 0)
    col = lax.broadcasted_iota(jnp.int32, (tq, tq), 1)

    def scores(job):
        h, i = job
        lo = i * tq
        cols = slice(h * HEAD_PAD, (h + 1) * HEAD_PAD)
        q = q_ref[lo:lo + tq, cols]
        s_d = lax.dot_general(q, k_ref[lo:lo + tq, cols], NT_DIMS, preferred_element_type=F32)
        s_d = jnp.where(row >= col, s_d, -jnp.inf)
        s_p = lax.dot_general(q, k_ref[0:lo, cols], NT_DIMS, preferred_element_type=F32) if i > 0 else None
        return s_d, s_p

    jobs = [(h, i) for i in range(seq // tq) for h in range(MLA_HEADS)]
    pending = [scores(job) for job in jobs[:ATTN_LOOKAHEAD]]
    for n_done, (h, i) in enumerate(jobs):
        lo = i * tq
        s_d, s_p = pending.pop(0)
        if n_done + ATTN_LOOKAHEAD < len(jobs):
            pending.append(scores(jobs[n_done + ATTN_LOOKAHEAD]))
        m = jnp.max(s_d, axis=-1, keepdims=True)
        if i > 0:
            m = jnp.maximum(m, jnp.max(s_p, axis=-1, keepdims=True))
        acc = _dot(jnp.exp2(s_d - m).astype(BF16), vaug_ref[h, lo:lo + tq, :])
        if i > 0:
            acc += _dot(jnp.exp2(s_p - m).astype(BF16), vaug_ref[h, 0:lo, :])
        o_ref[lo:lo + tq, h * V_DIM:(h + 1) * V_DIM] = (
            acc[:, 0:V_DIM] / acc[:, V_DIM:2 * V_DIM]).astype(o_ref.dtype)


def _postmix_kernel(h1_ref, ohg_ref, omla_ref, p_ref, wo_ref, g2_ref, wg_ref, wu_ref, wd_ref,
                    gple_ref, wpg_ref, wpp_ref, gfin_ref, y_ref):
    tp = h1_ref.shape[0]
    parts = [slice(r, r + TOKEN_TILE) for r in range(0, tp, TOKEN_TILE)] if tp % TOKEN_TILE == 0 else [slice(0, tp)]

    def tail_pieces(rows, h3):
        ctx = {}

        def gate():
            ctx["gate"] = jax.nn.sigmoid(_dot(_rms(h3, gple_ref[...]).astype(BF16), wpg_ref[...]))

        def finish():
            h4 = h3 + ctx["gate"] * _dot(p_ref[rows, :].astype(BF16), wpp_ref[...])
            y_ref[rows, :] = _rms(h4, gfin_ref[...])

        return [gate, finish]

    pending = []
    for rows in parts:
        h2 = (h1_ref[rows, :] + _dot(ohg_ref[rows, :], wo_ref[0:HG_WIDTH, :])
              + _dot(omla_ref[rows, :], wo_ref[HG_WIDTH:HG_WIDTH + MLA_HEADS * V_DIM, :]))
        slots = [c for c in POSTMIX_TAIL_AFTER_FFN_CHUNKS if c < D_FF // FFN_CHUNK][:len(pending)]
        between = {c: [piece] for c, piece in zip(slots, pending)}
        h3 = h2 + 0.5 * _swiglu(_rms(h2, g2_ref[...]).astype(BF16), wg_ref, wu_ref, wd_ref, between)
        for piece in pending[len(slots):]:
            piece()
        pending = tail_pieces(rows, h3)
    for piece in pending:
        piece()


def _resident(shape):
    return pl.BlockSpec(shape, lambda *_: (0,) * len(shape), pipeline_mode=pl.Buffered(1))


def _rows(tile, width):
    return pl.BlockSpec((tile, width), lambda i: (i, 0))


def kernel(x, p, positions, ln_ffn1, w1_gate, w1_up, w1_down, ln_mix, w_in, hg_lb_logits, hg_out_norm,
           q_a_norm, w_q_up, kv_a_norm, w_kv_up, w_out, ln_ffn2, w2_gate, w2_up, w2_down, ln_ple,
           w_ple_gate, w_ple_proj, ln_final):
    bsz, seq, _ = x.shape
    assert p.shape[0] == 1 and hg_lb_logits.shape[0] == 2, "single-layer trunk"
    n = bsz * seq
    tm = min(TOKEN_TILE, seq)
    tc = min(HGRN_CHUNK, seq)
    tq = min(ATTN_BLOCK, seq)
    tp = min(POSTMIX_TILE, n)
    assert seq % tm == 0 and tm % tc == 0 and seq % tq == 0 and n % tp == 0

    x2 = x.reshape(n, D_MODEL)
    pos2 = positions.astype(F32).reshape(n // tm, 1, tm)
    half = ROPE_DIM // 2
    inv_freq = ROPE_THETA ** (-jnp.arange(half, dtype=F32) / half)
    invf = jnp.concatenate([inv_freq, inv_freq, jnp.zeros((LANES - ROPE_DIM,), F32)]).reshape(1, LANES)
    wg1, wu1, wd1 = w1_gate[0].astype(BF16), w1_up[0].astype(BF16), w1_down[0].astype(BF16)
    win = w_in[0].astype(BF16)
    wq = jnp.pad(w_q_up[0].astype(BF16).reshape(Q_LORA, MLA_HEADS, QK_DIM),
                 ((0, 0), (0, 0), (0, HEAD_PAD - QK_DIM))).reshape(Q_LORA, MLA_HEADS * HEAD_PAD)
    wkv = w_kv_up[0].astype(BF16).reshape(KV_LORA, MLA_HEADS, NOPE_DIM + V_DIM)
    wk = wkv[:, :, :NOPE_DIM].reshape(KV_LORA, MLA_HEADS * NOPE_DIM)
    wv = wkv[:, :, NOPE_DIM:].reshape(KV_LORA, MLA_HEADS * V_DIM)
    wo = w_out[0].astype(BF16)
    row = lambda a: a.reshape(1, -1)
    layer0 = lambda w: w.reshape(w.shape[1:])

    steps = n // tm
    tile = lambda width: pl.BlockSpec((tm, width), lambda i: (jnp.minimum(i, steps - 1), 0))
    prev_tile = pl.BlockSpec((tm, HG_WIDTH), lambda i: (jnp.maximum(i - 1, 0), 0))
    h1, qp, kp, vv, o_hg = pl.pallas_call(
        functools.partial(_premix_kernel, tiles_per_row=seq // tm, chunk=tc),
        name="premix",
        grid=(steps + 1,),
        in_specs=[
            tile(D_MODEL), pl.BlockSpec((None, 1, tm), lambda i: (jnp.minimum(i, steps - 1), 0, 0)),
            _resident((1, LANES)), _resident((1, D_MODEL)),
            _resident((D_MODEL, D_FF)), _resident((D_MODEL, D_FF)), _resident((D_FF, D_MODEL)),
            _resident((1, D_MODEL)), _resident((D_MODEL, IN_WIDTH)),
            _resident((1, Q_LORA)), _resident((Q_LORA, MLA_HEADS * HEAD_PAD)), _resident((1, KV_LORA)),
            _resident((KV_LORA, MLA_HEADS * NOPE_DIM)), _resident((KV_LORA, MLA_HEADS * V_DIM)),
            _resident((2, HG_WIDTH)), _resident((HG_HEADS, HG_DIM)),
        ],
        out_specs=[tile(D_MODEL), tile(MLA_HEADS * HEAD_PAD), tile(MLA_HEADS * HEAD_PAD),
                   tile(MLA_HEADS * V_DIM), prev_tile],
        out_shape=[
            jax.ShapeDtypeStruct((n, D_MODEL), F32),
            jax.ShapeDtypeStruct((n, MLA_HEADS * HEAD_PAD), BF16),
            jax.ShapeDtypeStruct((n, MLA_HEADS * HEAD_PAD), BF16),
            jax.ShapeDtypeStruct((n, MLA_HEADS * V_DIM), BF16),
            jax.ShapeDtypeStruct((n, HG_WIDTH), BF16),
        ],
        scratch_shapes=[pltpu.VMEM((tm, HG_WIDTH), BF16), pltpu.VMEM((tm, HG_WIDTH), F32),
                        pltpu.VMEM((tm, HG_WIDTH), BF16), pltpu.VMEM((tm, HG_WIDTH), BF16),
                        pltpu.VMEM((HG_HEADS, HG_DIM, HG_DIM), F32)],
        compiler_params=pltpu.CompilerParams(dimension_semantics=("arbitrary",), vmem_limit_bytes=VMEM_LIMIT),
    )(x2, pos2, invf, row(ln_ffn1[0]), wg1, wu1, wd1, row(ln_mix[0]), win,
      row(q_a_norm[0]), wq, row(kv_a_norm[0]), wk, wv, hg_lb_logits, hg_out_norm[0])

    seq_spec = lambda width: pl.BlockSpec((seq, MLA_HEADS * width), lambda b: (b, 0))
    assert D_MODEL % (bsz * BF16_ROWS) == 0 and D_FF % (bsz * BF16_ROWS) == 0
    slab = lambda rows, cols: pl.BlockSpec((rows // bsz, cols), lambda b: (b, 0))
    o_mla, wg2, wu2, wd2 = pl.pallas_call(
        functools.partial(_attn_kernel, tq=tq),
        name="mla_attn",
        grid=(bsz,),
        in_specs=[seq_spec(HEAD_PAD), seq_spec(HEAD_PAD), seq_spec(V_DIM),
                  slab(D_MODEL, D_FF), slab(D_MODEL, D_FF), slab(D_FF, D_MODEL)],
        out_specs=[seq_spec(V_DIM), slab(D_MODEL, D_FF), slab(D_MODEL, D_FF), slab(D_FF, D_MODEL)],
        out_shape=[jax.ShapeDtypeStruct((n, MLA_HEADS * V_DIM), BF16),
                   jax.ShapeDtypeStruct((D_MODEL, D_FF), BF16), jax.ShapeDtypeStruct((D_MODEL, D_FF), BF16),
                   jax.ShapeDtypeStruct((D_FF, D_MODEL), BF16)],
        scratch_shapes=[pltpu.VMEM((MLA_HEADS, seq, 2 * V_DIM), BF16)],
        compiler_params=pltpu.CompilerParams(dimension_semantics=("parallel",), vmem_limit_bytes=VMEM_LIMIT),
    )(qp, kp, vv, layer0(w2_gate), layer0(w2_up), layer0(w2_down))

    y = pl.pallas_call(
        _postmix_kernel,
        name="postmix",
        grid=(n // tp,),
        in_specs=[
            _rows(tp, D_MODEL), _rows(tp, HG_WIDTH), _rows(tp, MLA_HEADS * V_DIM), _rows(tp, PLE_DIM),
            _resident((HG_WIDTH + MLA_HEADS * V_DIM, D_MODEL)), _resident((1, D_MODEL)),
            _resident((D_MODEL, D_FF)), _resident((D_MODEL, D_FF)), _resident((D_FF, D_MODEL)),
            _resident((1, D_MODEL)), _resident((D_MODEL, D_MODEL)), _resident((PLE_DIM, D_MODEL)),
            _resident((1, D_MODEL)),
        ],
        out_specs=_rows(tp, D_MODEL),
        out_shape=jax.ShapeDtypeStruct((n, D_MODEL), F32),
        compiler_params=pltpu.CompilerParams(dimension_semantics=("parallel",), vmem_limit_bytes=VMEM_LIMIT),
    )(h1, o_hg, o_mla, p[0].reshape(n, PLE_DIM), wo, row(ln_ffn2[0]),
      wg2, wu2, wd2, row(ln_ple[0]), w_ple_gate[0].astype(BF16), w_ple_proj[0].astype(BF16), row(ln_final))

    return y.reshape(bsz, seq, D_MODEL)
```

```python
import functools

import jax
import jax.numpy as jnp
from jax import lax
from jax.experimental import pallas as pl
from jax.experimental.pallas import tpu as pltpu

F32 = jnp.float32
BF16 = jnp.bfloat16

D_MODEL = 1024
D_FF = 2816
PLE_DIM = 256
HG_HEADS = 4
HG_DIM = 128
HG_WIDTH = HG_HEADS * HG_DIM
MLA_HEADS = 4
Q_LORA = 256
KV_LORA = 128
NOPE_DIM = 128
ROPE_DIM = 64
V_DIM = 128
QK_DIM = NOPE_DIM + ROPE_DIM
ROPE_THETA = 10000.0
EPS = 1e-6
LOG2_E = 1.4426950408889634

LANES = 128
SUBLANES = 8
BF16_ROWS = 2 * SUBLANES
HEAD_PAD = 2 * LANES
HG_GROUPS = 4
IN_WIDTH = HG_GROUPS * HG_WIDTH + Q_LORA + KV_LORA + ROPE_DIM

FFN_CHUNK = 256
TOKEN_TILE = 512
POSTMIX_TILE = 1024
HGRN_CHUNK = 256
ATTN_BLOCK = 256
ATTN_LOOKAHEAD = 3
POSTMIX_TAIL_AFTER_FFN_CHUNKS = (1, 4)
HGRN_AFTER_FFN_CHUNKS = (0, 0, 0, 0)
VMEM_LIMIT = 56 * 1024 * 1024

NT_DIMS = (((1,), (1,)), ((), ()))
TN_DIMS = (((0,), (0,)), ((), ()))


def _dot(a, b):
    return jnp.dot(a, b, preferred_element_type=F32)


def _rms(x, g):
    return x * lax.rsqrt(jnp.mean(x * x, axis=-1, keepdims=True) + EPS) * g


def _swiglu(xn, wg_ref, wu_ref, wd_ref, between=None):
    acc = None
    for idx, c in enumerate(range(0, D_FF, FFN_CHUNK)):
        g = _dot(xn, wg_ref[:, c:c + FFN_CHUNK])
        u = _dot(xn, wu_ref[:, c:c + FFN_CHUNK])
        a = (g * jax.nn.sigmoid(g) * u).astype(BF16)
        d = _dot(a, wd_ref[c:c + FFN_CHUNK, :])
        acc = d if acc is None else acc + d
        for piece in (between or {}).get(idx, ()):
            piece()
    return acc


def _rope(x, cos_t, sin_lo, sin_hi):
    return (x * cos_t + pltpu.roll(x, LANES - ROPE_DIM // 2, 1) * sin_lo
            + pltpu.roll(x, ROPE_DIM // 2, 1) * sin_hi)


def _premix_kernel(x_ref, pos_ref, invf_ref, g1_ref, wg_ref, wu_ref, wd_ref, gmix_ref, win_ref,
                   qan_ref, wq_ref, kvan_ref, wk_ref, wv_ref, lbl_ref, gn_ref,
                   h1_ref, qp_ref, kp_ref, v_ref, ohg_ref,
                   hq_s, hf_s, hi_s, hg_s, st_ref, *, tiles_per_row, chunk):
    i = pl.program_id(0)
    tm = x_ref.shape[0]

    @pl.when(i == 0)
    def _():
        hq_s[...] = jnp.zeros_like(hq_s)
        hf_s[...] = jnp.zeros_like(hf_s)
        hi_s[...] = jnp.zeros_like(hi_s)
        hg_s[...] = jnp.zeros_like(hg_s)
        st_ref[...] = jnp.zeros_like(st_ref)

    @pl.when(lax.rem(i + tiles_per_row - 1, tiles_per_row) == 0)
    def _():
        st_ref[...] = jnp.zeros_like(st_ref)

    pieces = []
    for r0 in range(0, tm, chunk):
        pieces += _hgrn_pieces(lbl_ref, hq_s, hf_s, hi_s, hg_s, gn_ref, ohg_ref, st_ref, r0, chunk)
    early_chunks = [c for c in HGRN_AFTER_FFN_CHUNKS if c < D_FF // FFN_CHUNK][:len(pieces)]
    early = {}
    for c, piece in zip(early_chunks, pieces):
        early.setdefault(c, []).append(piece)
    late = iter(pieces[len(early_chunks):])

    t_row = lax.broadcasted_iota(jnp.int32, (tm, tm), 0)
    t_col = lax.broadcasted_iota(jnp.int32, (tm, tm), 1)
    pos_col = jnp.sum(jnp.where(t_row == t_col, pos_ref[...], 0.0), axis=1, keepdims=True)
    ang = pos_col * invf_ref[...]
    lane = lax.broadcasted_iota(jnp.int32, ang.shape, 1)
    cos_a, sin_a = jnp.cos(ang), jnp.sin(ang)
    half = ROPE_DIM // 2
    cos_t = jnp.where(lane < ROPE_DIM, cos_a, 0.0)
    sin_lo = jnp.where(lane < half, -sin_a, 0.0)
    sin_hi = jnp.where((lane >= half) & (lane < ROPE_DIM), sin_a, 0.0)

    x = x_ref[...]
    xn = _rms(x, g1_ref[...]).astype(BF16)
    h1 = x + 0.5 * _swiglu(xn, wg_ref, wu_ref, wd_ref, early)
    h1_ref[...] = h1

    hn = _rms(h1, gmix_ref[...]).astype(BF16)
    w = HG_WIDTH
    new = []
    for j in range(HG_GROUPS):
        new.append(_dot(hn, win_ref[:, j * w:(j + 1) * w]))
        piece = next(late, None)
        if piece is not None:
            piece()
    for piece in late:
        piece()
    c0 = HG_GROUPS * w
    cq = _dot(hn, win_ref[:, c0:c0 + Q_LORA])
    ckv_kr = _dot(hn, win_ref[:, c0 + Q_LORA:IN_WIDTH])
    ckv = ckv_kr[:, 0:KV_LORA]
    kr = jnp.concatenate([ckv_kr[:, KV_LORA:KV_LORA + ROPE_DIM],
                          jnp.zeros((tm, LANES - ROPE_DIM), F32)], axis=1)

    q = _dot(_rms(cq, qan_ref[...]).astype(BF16), wq_ref[...]) * (QK_DIM ** -0.5 * LOG2_E)
    ckvn = _rms(ckv, kvan_ref[...]).astype(BF16)
    k_nope = _dot(ckvn, wk_ref[...])
    v_ref[...] = _dot(ckvn, wv_ref[...]).astype(v_ref.dtype)
    k_rope = _rope(kr, cos_t, sin_lo, sin_hi).astype(kp_ref.dtype)
    for h in range(MLA_HEADS):
        a = h * HEAD_PAD
        qp_ref[:, a:a + NOPE_DIM] = q[:, a:a + NOPE_DIM].astype(qp_ref.dtype)
        qp_ref[:, a + NOPE_DIM:a + HEAD_PAD] = _rope(
            q[:, a + NOPE_DIM:a + HEAD_PAD], cos_t, sin_lo, sin_hi).astype(qp_ref.dtype)
        kp_ref[:, a:a + NOPE_DIM] = k_nope[:, h * NOPE_DIM:(h + 1) * NOPE_DIM].astype(kp_ref.dtype)
        kp_ref[:, a + NOPE_DIM:a + HEAD_PAD] = k_rope

    hq_s[...] = new[0].astype(hq_s.dtype)
    hf_s[...] = new[1]
    hi_s[...] = new[2].astype(hi_s.dtype)
    hg_s[...] = new[3].astype(hg_s.dtype)


def _neg_gap(b, m):
    t = b.shape[0]
    if m >= SUBLANES:
        pieces = []
        for s in range(0, t, 2 * m):
            mid = b[s + m - 1:s + m, :]
            pieces += [mid - b[s:s + m], b[s + m:s + 2 * m] - mid]
        return jnp.concatenate(pieces, axis=0)
    b3 = b.reshape(t // SUBLANES, SUBLANES, LANES)
    sub = lax.broadcasted_iota(jnp.int32, b3.shape, 1)

    def row(i):
        return jnp.broadcast_to(b3[:, i:i + 1, :], b3.shape)

    mid = row(m - 1)
    for s in range(2 * m, SUBLANES, 2 * m):
        mid = jnp.where(sub >= s, row(s + m - 1), mid)
    return -jnp.abs(b3 - mid).reshape(t, LANES)


def _hgrn_pieces(lbl_ref, hq_ref, hf_ref, hi_ref, hg_ref, gn_ref, o_ref, st_ref, r0, t):
    rows = slice(r0, r0 + t)
    half = t // 2
    ctx = {}

    def gates():
        lg = lbl_ref[...]
        e = jnp.exp(lg - jnp.max(lg, axis=0, keepdims=True))
        lb = e[0:1, :] / jnp.sum(e, axis=0, keepdims=True)
        f_raw = hf_ref[rows, :]
        sig = jax.nn.sigmoid(f_raw)
        f = lb + (1.0 - lb) * sig
        g = jnp.log(f)
        ctx["f"] = f
        ctx["kk"] = (1.0 - lb) * (1.0 - sig)

        row = lax.broadcasted_iota(jnp.int32, (t, t), 0)
        col = lax.broadcasted_iota(jnp.int32, (t, t), 1)
        tri = (row >= col).astype(BF16)
        g1 = g.astype(BF16)
        r1 = g - g1.astype(F32)
        g2 = r1.astype(BF16)
        g3 = (r1 - g2.astype(F32)).astype(BF16)
        ctx["b"] = (_dot(tri, g1) + _dot(tri, g2) + _dot(tri, g3)) * LOG2_E

    def levels():
        ri = lax.broadcasted_iota(jnp.int32, (half, half), 0)
        ci = lax.broadcasted_iota(jnp.int32, (half, half), 1)
        xm = jnp.where(ri > ci, ri ^ ci, 0)
        odd_row = (lax.broadcasted_iota(jnp.int32, (t, HG_DIM), 0) & 1) == 1

        sls = [slice(h * HG_DIM, (h + 1) * HG_DIM) for h in range(HG_HEADS)]
        ctx["q"] = [hq_ref[rows, sl] for sl in sls]
        ctx["v"] = [hi_ref[rows, sl] for sl in sls]
        ctx["kb"] = [ctx["kk"][:, sl].astype(BF16) for sl in sls]
        ctx["bh"] = [ctx["b"][:, sl] for sl in sls]

        ps = [[None, None] for _ in sls]
        for i in range(half.bit_length() - 1):
            m = 1 << i
            mask = (xm >= m) & (xm < 2 * m)
            for h, sl in enumerate(sls):
                if m == 1:
                    decay = jnp.where(odd_row, ctx["f"][:, sl], 1.0).astype(BF16)
                else:
                    decay = jnp.exp2(_neg_gap(ctx["bh"][h], m)).astype(BF16)
                q_l, k_l = ctx["q"][h] * decay, ctx["kb"][h] * decay
                for qi, q0 in enumerate((0, half)):
                    s_l = lax.dot_general(q_l[q0:q0 + half], k_l[q0:q0 + half], NT_DIMS,
                                          preferred_element_type=F32)
                    ps[h][qi] = jnp.where(mask, s_l, 0.0 if ps[h][qi] is None else ps[h][qi])
        ctx["diag"] = [[p.astype(BF16) for p in pair] for pair in ps]

    def head(h):
        sl = slice(h * HG_DIM, (h + 1) * HG_DIM)
        q, v, kb, b, diag = ctx["q"][h], ctx["v"][h], ctx["kb"][h], ctx["bh"][h], ctx["diag"][h]
        mid = b[half - 1:half, :]
        cross = lax.dot_general(q[half:] * jnp.exp2(b[half:] - mid).astype(BF16),
                                kb[:half] * jnp.exp2(mid - b[:half]).astype(BF16), NT_DIMS,
                                preferred_element_type=F32).astype(BF16)
        o = jnp.concatenate([_dot(diag[0], v[:half]),
                             _dot(jnp.concatenate([cross, diag[1]], axis=1), v)], axis=0)
        o = o + jnp.sum((q * kb).astype(F32), axis=-1, keepdims=True) * v.astype(F32)

        st = st_ref[h]
        b_last = b[t - 1:t, :]
        o = o + lax.dot_general(q * jnp.exp2(b).astype(BF16), st.astype(BF16), NT_DIMS,
                                preferred_element_type=F32)
        k_end = kb * jnp.exp2(b_last - b).astype(BF16)
        st_ref[h] = st * jnp.exp2(b_last) + lax.dot_general(v, k_end, TN_DIMS, preferred_element_type=F32)

        gate = hg_ref[rows, sl].astype(F32)
        o_ref[rows, sl] = (_rms(o, gn_ref[h:h + 1, :]) * (gate * jax.nn.sigmoid(gate))).astype(o_ref.dtype)

    def heads():
        levels()
        for h in range(HG_HEADS):
            head(h)

    return [gates, heads]


def _attn_kernel(q_ref, k_ref, v_ref, wg_ref, wu_ref, wd_ref, o_ref, wg_bf_ref, wu_bf_ref, wd_bf_ref,
                 vaug_ref, *, tq):
    wg_bf_ref[...] = wg_ref[...].astype(wg_bf_ref.dtype)
    wu_bf_ref[...] = wu_ref[...].astype(wu_bf_ref.dtype)
    wd_bf_ref[...] = wd_ref[...].astype(wd_bf_ref.dtype)

    seq = q_ref.shape[0]
    for h in range(MLA_HEADS):
        vaug_ref[h, :, 0:V_DIM] = v_ref[:, h * V_DIM:(h + 1) * V_DIM]
        vaug_ref[h, :, V_DIM:2 * V_DIM] = jnp.ones((seq, V_DIM), vaug_ref.dtype)
    row = lax.broadcasted_iota(jnp.int32, (tq, tq), 0)
    col = lax.broadcasted_iota(jnp.int32, (tq, tq), 1)

    def scores(job):
        h, i = job
        lo = i * tq
        cols = slice(h * HEAD_PAD, (h + 1) * HEAD_PAD)
        q = q_ref[lo:lo + tq, cols]
        s_d = lax.dot_general(q, k_ref[lo:lo + tq, cols], NT_DIMS, preferred_element_type=F32)
        s_d = jnp.where(row >= col, s_d, -jnp.inf)
        s_p = lax.dot_general(q, k_ref[0:lo, cols], NT_DIMS, preferred_element_type=F32) if i > 0 else None
        return s_d, s_p

    jobs = [(h, i) for i in range(seq // tq) for h in range(MLA_HEADS)]
    pending = [scores(job) for job in jobs[:ATTN_LOOKAHEAD]]
    for n_done, (h, i) in enumerate(jobs):
        lo = i * tq
        s_d, s_p = pending.pop(0)
        if n_done + ATTN_LOOKAHEAD < len(jobs):
            pending.append(scores(jobs[n_done + ATTN_LOOKAHEAD]))
        m = jnp.max(s_d, axis=-1, keepdims=True)
        if i > 0:
            m = jnp.maximum(m, jnp.max(s_p, axis=-1, keepdims=True))
        acc = _dot(jnp.exp2(s_d - m).astype(BF16), vaug_ref[h, lo:lo + tq, :])
        if i > 0:
            acc += _dot(jnp.exp2(s_p - m).astype(BF16), vaug_ref[h, 0:lo, :])
        o_ref[lo:lo + tq, h * V_DIM:(h + 1) * V_DIM] = (
            acc[:, 0:V_DIM] / acc[:, V_DIM:2 * V_DIM]).astype(o_ref.dtype)


def _postmix_kernel(h1_ref, ohg_ref, omla_ref, p_ref, wo_ref, g2_ref, wg_ref, wu_ref, wd_ref,
                    gple_ref, wpg_ref, wpp_ref, gfin_ref, y_ref):
    tp = h1_ref.shape[0]
    parts = [slice(r, r + TOKEN_TILE) for r in range(0, tp, TOKEN_TILE)] if tp % TOKEN_TILE == 0 else [slice(0, tp)]

    def tail_pieces(rows, h3):
        ctx = {}

        def gate():
            ctx["gate"] = jax.nn.sigmoid(_dot(_rms(h3, gple_ref[...]).astype(BF16), wpg_ref[...]))

        def finish():
            h4 = h3 + ctx["gate"] * _dot(p_ref[rows, :].astype(BF16), wpp_ref[...])
            y_ref[rows, :] = _rms(h4, gfin_ref[...])

        return [gate, finish]

    pending = []
    for rows in parts:
        h2 = (h1_ref[rows, :] + _dot(ohg_ref[rows, :], wo_ref[0:HG_WIDTH, :])
              + _dot(omla_ref[rows, :], wo_ref[HG_WIDTH:HG_WIDTH + MLA_HEADS * V_DIM, :]))
        slots = [c for c in POSTMIX_TAIL_AFTER_FFN_CHUNKS if c < D_FF // FFN_CHUNK][:len(pending)]
        between = {c: [piece] for c, piece in zip(slots, pending)}
        h3 = h2 + 0.5 * _swiglu(_rms(h2, g2_ref[...]).astype(BF16), wg_ref, wu_ref, wd_ref, between)
        for piece in pending[len(slots):]:
            piece()
        pending = tail_pieces(rows, h3)
    for piece in pending:
        piece()


def _resident(shape):
    return pl.BlockSpec(shape, lambda *_: (0,) * len(shape), pipeline_mode=pl.Buffered(1))


def _rows(tile, width):
    return pl.BlockSpec((tile, width), lambda i: (i, 0))


def kernel(x, p, positions, ln_ffn1, w1_gate, w1_up, w1_down, ln_mix, w_in, hg_lb_logits, hg_out_norm,
           q_a_norm, w_q_up, kv_a_norm, w_kv_up, w_out, ln_ffn2, w2_gate, w2_up, w2_down, ln_ple,
           w_ple_gate, w_ple_proj, ln_final):
    bsz, seq, _ = x.shape
    assert p.shape[0] == 1 and hg_lb_logits.shape[0] == 2, "single-layer trunk"
    n = bsz * seq
    tm = min(TOKEN_TILE, seq)
    tc = min(HGRN_CHUNK, seq)
    tq = min(ATTN_BLOCK, seq)
    tp = min(POSTMIX_TILE, n)
    assert seq % tm == 0 and tm % tc == 0 and seq % tq == 0 and n % tp == 0

    x2 = x.reshape(n, D_MODEL)
    pos2 = positions.astype(F32).reshape(n // tm, 1, tm)
    half = ROPE_DIM // 2
    inv_freq = ROPE_THETA ** (-jnp.arange(half, dtype=F32) / half)
    invf = jnp.concatenate([inv_freq, inv_freq, jnp.zeros((LANES - ROPE_DIM,), F32)]).reshape(1, LANES)
    wg1, wu1, wd1 = w1_gate[0].astype(BF16), w1_up[0].astype(BF16), w1_down[0].astype(BF16)
    win = w_in[0].astype(BF16)
    wq = jnp.pad(w_q_up[0].astype(BF16).reshape(Q_LORA, MLA_HEADS, QK_DIM),
                 ((0, 0), (0, 0), (0, HEAD_PAD - QK_DIM))).reshape(Q_LORA, MLA_HEADS * HEAD_PAD)
    wkv = w_kv_up[0].astype(BF16).reshape(KV_LORA, MLA_HEADS, NOPE_DIM + V_DIM)
    wk = wkv[:, :, :NOPE_DIM].reshape(KV_LORA, MLA_HEADS * NOPE_DIM)
    wv = wkv[:, :, NOPE_DIM:].reshape(KV_LORA, MLA_HEADS * V_DIM)
    wo = w_out[0].astype(BF16)
    row = lambda a: a.reshape(1, -1)
    layer0 = lambda w: w.reshape(w.shape[1:])

    steps = n // tm
    tile = lambda width: pl.BlockSpec((tm, width), lambda i: (jnp.minimum(i, steps - 1), 0))
    prev_tile = pl.BlockSpec((tm, HG_WIDTH), lambda i: (jnp.maximum(i - 1, 0), 0))
    h1, qp, kp, vv, o_hg = pl.pallas_call(
        functools.partial(_premix_kernel, tiles_per_row=seq // tm, chunk=tc),
        name="premix",
        grid=(steps + 1,),
        in_specs=[
            tile(D_MODEL), pl.BlockSpec((None, 1, tm), lambda i: (jnp.minimum(i, steps - 1), 0, 0)),
            _resident((1, LANES)), _resident((1, D_MODEL)),
            _resident((D_MODEL, D_FF)), _resident((D_MODEL, D_FF)), _resident((D_FF, D_MODEL)),
            _resident((1, D_MODEL)), _resident((D_MODEL, IN_WIDTH)),
            _resident((1, Q_LORA)), _resident((Q_LORA, MLA_HEADS * HEAD_PAD)), _resident((1, KV_LORA)),
            _resident((KV_LORA, MLA_HEADS * NOPE_DIM)), _resident((KV_LORA, MLA_HEADS * V_DIM)),
            _resident((2, HG_WIDTH)), _resident((HG_HEADS, HG_DIM)),
        ],
        out_specs=[tile(D_MODEL), tile(MLA_HEADS * HEAD_PAD), tile(MLA_HEADS * HEAD_PAD),
                   tile(MLA_HEADS * V_DIM), prev_tile],
        out_shape=[
            jax.ShapeDtypeStruct((n, D_MODEL), F32),
            jax.ShapeDtypeStruct((n, MLA_HEADS * HEAD_PAD), BF16),
            jax.ShapeDtypeStruct((n, MLA_HEADS * HEAD_PAD), BF16),
            jax.ShapeDtypeStruct((n, MLA_HEADS * V_DIM), BF16),
            jax.ShapeDtypeStruct((n, HG_WIDTH), BF16),
        ],
        scratch_shapes=[pltpu.VMEM((tm, HG_WIDTH), BF16), pltpu.VMEM((tm, HG_WIDTH), F32),
                        pltpu.VMEM((tm, HG_WIDTH), BF16), pltpu.VMEM((tm, HG_WIDTH), BF16),
                        pltpu.VMEM((HG_HEADS, HG_DIM, HG_DIM), F32)],
        compiler_params=pltpu.CompilerParams(dimension_semantics=("arbitrary",), vmem_limit_bytes=VMEM_LIMIT),
    )(x2, pos2, invf, row(ln_ffn1[0]), wg1, wu1, wd1, row(ln_mix[0]), win,
      row(q_a_norm[0]), wq, row(kv_a_norm[0]), wk, wv, hg_lb_logits, hg_out_norm[0])

    seq_spec = lambda width: pl.BlockSpec((seq, MLA_HEADS * width), lambda b: (b, 0))
    assert D_MODEL % (bsz * BF16_ROWS) == 0 and D_FF % (bsz * BF16_ROWS) == 0
    slab = lambda rows, cols: pl.BlockSpec((rows // bsz, cols), lambda b: (b, 0))
    o_mla, wg2, wu2, wd2 = pl.pallas_call(
        functools.partial(_attn_kernel, tq=tq),
        name="mla_attn",
        grid=(bsz,),
        in_specs=[seq_spec(HEAD_PAD), seq_spec(HEAD_PAD), seq_spec(V_DIM),
                  slab(D_MODEL, D_FF), slab(D_MODEL, D_FF), slab(D_FF, D_MODEL)],
        out_specs=[seq_spec(V_DIM), slab(D_MODEL, D_FF), slab(D_MODEL, D_FF), slab(D_FF, D_MODEL)],
        out_shape=[jax.ShapeDtypeStruct((n, MLA_HEADS * V_DIM), BF16),
                   jax.ShapeDtypeStruct((D_MODEL, D_FF), BF16), jax.ShapeDtypeStruct((D_MODEL, D_FF), BF16),
                   jax.ShapeDtypeStruct((D_FF, D_MODEL), BF16)],
        scratch_shapes=[pltpu.VMEM((MLA_HEADS, seq, 2 * V_DIM), BF16)],
        compiler_params=pltpu.CompilerParams(dimension_semantics=("parallel",), vmem_limit_bytes=VMEM_LIMIT),
    )(qp, kp, vv, layer0(w2_gate), layer0(w2_up), layer0(w2_down))

    y = pl.pallas_call(
        _postmix_kernel,
        name="postmix",
        grid=(n // tp,),
        in_specs=[
            _rows(tp, D_MODEL), _rows(tp, HG_WIDTH), _rows(tp, MLA_HEADS * V_DIM), _rows(tp, PLE_DIM),
            _resident((HG_WIDTH + MLA_HEADS * V_DIM, D_MODEL)), _resident((1, D_MODEL)),
            _resident((D_MODEL, D_FF)), _resident((D_MODEL, D_FF)), _resident((D_FF, D_MODEL)),
            _resident((1, D_MODEL)), _resident((D_MODEL, D_MODEL)), _resident((PLE_DIM, D_MODEL)),
            _resident((1, D_MODEL)),
        ],
        out_specs=_rows(tp, D_MODEL),
        out_shape=jax.ShapeDtypeStruct((n, D_MODEL), F32),
        compiler_params=pltpu.CompilerParams(dimension_semantics=("parallel",), vmem_limit_bytes=VMEM_LIMIT),
    )(h1, o_hg, o_mla, p[0].reshape(n, PLE_DIM), wo, row(ln_ffn2[0]),
      wg2, wu2, wd2, row(ln_ple[0]), w_ple_gate[0].astype(BF16), w_ple_proj[0].astype(BF16), row(ln_final))

    return y.reshape(bsz, seq, D_MODEL)
```

```python
import functools

import jax
import jax.numpy as jnp
from jax import lax
from jax.experimental import pallas as pl
from jax.experimental.pallas import tpu as pltpu

F32 = jnp.float32
BF16 = jnp.bfloat16

D_MODEL = 1024
D_FF = 2816
PLE_DIM = 256
HG_HEADS = 4
HG_DIM = 128
HG_WIDTH = HG_HEADS * HG_DIM
MLA_HEADS = 4
Q_LORA = 256
KV_LORA = 128
NOPE_DIM = 128
ROPE_DIM = 64
V_DIM = 128
QK_DIM = NOPE_DIM + ROPE_DIM
ROPE_THETA = 10000.0
EPS = 1e-6
LOG2_E = 1.4426950408889634

LANES = 128
SUBLANES = 8
BF16_ROWS = 2 * SUBLANES
HEAD_PAD = 2 * LANES
HG_GROUPS = 4
IN_WIDTH = HG_GROUPS * HG_WIDTH + Q_LORA + KV_LORA + ROPE_DIM

FFN_CHUNK = 256
TOKEN_TILE = 512
POSTMIX_TILE = 1024
HGRN_CHUNK = 256
ATTN_BLOCK = 256
ATTN_LOOKAHEAD = 3
POSTMIX_TAIL_AFTER_FFN_CHUNKS = (1, 4)
HGRN_AFTER_FFN_CHUNKS = (0, 0, 1, 2, 3, 4)
VMEM_LIMIT = 56 * 1024 * 1024

NT_DIMS = (((1,), (1,)), ((), ()))
TN_DIMS = (((0,), (0,)), ((), ()))


def _dot(a, b):
    return jnp.dot(a, b, preferred_element_type=F32)


def _rms(x, g):
    return x * lax.rsqrt(jnp.mean(x * x, axis=-1, keepdims=True) + EPS) * g


def _swiglu(xn, wg_ref, wu_ref, wd_ref, between=None):
    acc = None
    for idx, c in enumerate(range(0, D_FF, FFN_CHUNK)):
        g = _dot(xn, wg_ref[:, c:c + FFN_CHUNK])
        u = _dot(xn, wu_ref[:, c:c + FFN_CHUNK])
        a = (g * jax.nn.sigmoid(g) * u).astype(BF16)
        d = _dot(a, wd_ref[c:c + FFN_CHUNK, :])
        acc = d if acc is None else acc + d
        for piece in (between or {}).get(idx, ()):
            piece()
    return acc


def _rope(x, cos_t, sin_lo, sin_hi):
    return (x * cos_t + pltpu.roll(x, LANES - ROPE_DIM // 2, 1) * sin_lo
            + pltpu.roll(x, ROPE_DIM // 2, 1) * sin_hi)


def _premix_kernel(x_ref, pos_ref, invf_ref, g1_ref, wg_ref, wu_ref, wd_ref, gmix_ref, win_ref,
                   qan_ref, wq_ref, kvan_ref, wk_ref, wv_ref, lbl_ref, gn_ref,
                   h1_ref, qp_ref, kp_ref, v_ref, ohg_ref,
                   hq_s, hf_s, hi_s, hg_s, st_ref, *, tiles_per_row, chunk):
    i = pl.program_id(0)
    tm = x_ref.shape[0]

    @pl.when(i == 0)
    def _():
        hq_s[...] = jnp.zeros_like(hq_s)
        hf_s[...] = jnp.zeros_like(hf_s)
        hi_s[...] = jnp.zeros_like(hi_s)
        hg_s[...] = jnp.zeros_like(hg_s)
        st_ref[...] = jnp.zeros_like(st_ref)

    @pl.when(lax.rem(i + tiles_per_row - 1, tiles_per_row) == 0)
    def _():
        st_ref[...] = jnp.zeros_like(st_ref)

    per_chunk = [_hgrn_pieces(lbl_ref, hq_s, hf_s, hi_s, hg_s, gn_ref, ohg_ref, st_ref, r0, chunk)
                 for r0 in range(0, tm, chunk)]
    pieces = [stages[k] for k in range(2) for stages in per_chunk] + [stages[2] for stages in per_chunk]
    early_chunks = [c for c in HGRN_AFTER_FFN_CHUNKS if c < D_FF // FFN_CHUNK][:len(pieces)]
    early = {}
    for c, piece in zip(early_chunks, pieces):
        early.setdefault(c, []).append(piece)
    late = iter(pieces[len(early_chunks):])

    t_row = lax.broadcasted_iota(jnp.int32, (tm, tm), 0)
    t_col = lax.broadcasted_iota(jnp.int32, (tm, tm), 1)
    pos_col = jnp.sum(jnp.where(t_row == t_col, pos_ref[...], 0.0), axis=1, keepdims=True)
    ang = pos_col * invf_ref[...]
    lane = lax.broadcasted_iota(jnp.int32, ang.shape, 1)
    cos_a, sin_a = jnp.cos(ang), jnp.sin(ang)
    half = ROPE_DIM // 2
    cos_t = jnp.where(lane < ROPE_DIM, cos_a, 0.0)
    sin_lo = jnp.where(lane < half, -sin_a, 0.0)
    sin_hi = jnp.where((lane >= half) & (lane < ROPE_DIM), sin_a, 0.0)

    x = x_ref[...]
    xn = _rms(x, g1_ref[...]).astype(BF16)
    h1 = x + 0.5 * _swiglu(xn, wg_ref, wu_ref, wd_ref, early)
    h1_ref[...] = h1

    hn = _rms(h1, gmix_ref[...]).astype(BF16)
    w = HG_WIDTH
    new = []
    for j in range(HG_GROUPS):
        new.append(_dot(hn, win_ref[:, j * w:(j + 1) * w]))
        piece = next(late, None)
        if piece is not None:
            piece()
    for piece in late:
        piece()
    c0 = HG_GROUPS * w
    cq = _dot(hn, win_ref[:, c0:c0 + Q_LORA])
    ckv_kr = _dot(hn, win_ref[:, c0 + Q_LORA:IN_WIDTH])
    ckv = ckv_kr[:, 0:KV_LORA]
    kr = jnp.concatenate([ckv_kr[:, KV_LORA:KV_LORA + ROPE_DIM],
                          jnp.zeros((tm, LANES - ROPE_DIM), F32)], axis=1)

    q = _dot(_rms(cq, qan_ref[...]).astype(BF16), wq_ref[...]) * (QK_DIM ** -0.5 * LOG2_E)
    ckvn = _rms(ckv, kvan_ref[...]).astype(BF16)
    k_nope = _dot(ckvn, wk_ref[...])
    v_ref[...] = _dot(ckvn, wv_ref[...]).astype(v_ref.dtype)
    k_rope = _rope(kr, cos_t, sin_lo, sin_hi).astype(kp_ref.dtype)
    for h in range(MLA_HEADS):
        a = h * HEAD_PAD
        qp_ref[:, a:a + NOPE_DIM] = q[:, a:a + NOPE_DIM].astype(qp_ref.dtype)
        qp_ref[:, a + NOPE_DIM:a + HEAD_PAD] = _rope(
            q[:, a + NOPE_DIM:a + HEAD_PAD], cos_t, sin_lo, sin_hi).astype(qp_ref.dtype)
        kp_ref[:, a:a + NOPE_DIM] = k_nope[:, h * NOPE_DIM:(h + 1) * NOPE_DIM].astype(kp_ref.dtype)
        kp_ref[:, a + NOPE_DIM:a + HEAD_PAD] = k_rope

    hq_s[...] = new[0].astype(hq_s.dtype)
    hf_s[...] = new[1]
    hi_s[...] = new[2].astype(hi_s.dtype)
    hg_s[...] = new[3].astype(hg_s.dtype)


def _neg_gap(b, m):
    t = b.shape[0]
    if m >= SUBLANES:
        pieces = []
        for s in range(0, t, 2 * m):
            mid = b[s + m - 1:s + m, :]
            pieces += [mid - b[s:s + m], b[s + m:s + 2 * m] - mid]
        return jnp.concatenate(pieces, axis=0)
    b3 = b.reshape(t // SUBLANES, SUBLANES, LANES)
    sub = lax.broadcasted_iota(jnp.int32, b3.shape, 1)

    def row(i):
        return jnp.broadcast_to(b3[:, i:i + 1, :], b3.shape)

    mid = row(m - 1)
    for s in range(2 * m, SUBLANES, 2 * m):
        mid = jnp.where(sub >= s, row(s + m - 1), mid)
    return -jnp.abs(b3 - mid).reshape(t, LANES)


def _hgrn_pieces(lbl_ref, hq_ref, hf_ref, hi_ref, hg_ref, gn_ref, o_ref, st_ref, r0, t):
    rows = slice(r0, r0 + t)
    half = t // 2
    ctx = {}

    def gates():
        lg = lbl_ref[...]
        e = jnp.exp(lg - jnp.max(lg, axis=0, keepdims=True))
        lb = e[0:1, :] / jnp.sum(e, axis=0, keepdims=True)
        f_raw = hf_ref[rows, :]
        sig = jax.nn.sigmoid(f_raw)
        f = lb + (1.0 - lb) * sig
        g = jnp.log(f)
        ctx["f"] = f
        ctx["kk"] = (1.0 - lb) * (1.0 - sig)

        row = lax.broadcasted_iota(jnp.int32, (t, t), 0)
        col = lax.broadcasted_iota(jnp.int32, (t, t), 1)
        tri = (row >= col).astype(BF16)
        g1 = g.astype(BF16)
        r1 = g - g1.astype(F32)
        g2 = r1.astype(BF16)
        g3 = (r1 - g2.astype(F32)).astype(BF16)
        ctx["b"] = (_dot(tri, g1) + _dot(tri, g2) + _dot(tri, g3)) * LOG2_E

    def levels():
        ri = lax.broadcasted_iota(jnp.int32, (half, half), 0)
        ci = lax.broadcasted_iota(jnp.int32, (half, half), 1)
        xm = jnp.where(ri > ci, ri ^ ci, 0)
        odd_row = (lax.broadcasted_iota(jnp.int32, (t, HG_DIM), 0) & 1) == 1

        sls = [slice(h * HG_DIM, (h + 1) * HG_DIM) for h in range(HG_HEADS)]
        ctx["q"] = [hq_ref[rows, sl] for sl in sls]
        ctx["v"] = [hi_ref[rows, sl] for sl in sls]
        ctx["kb"] = [ctx["kk"][:, sl].astype(BF16) for sl in sls]
        ctx["bh"] = [ctx["b"][:, sl] for sl in sls]

        ps = [[None, None] for _ in sls]
        for i in range(half.bit_length() - 1):
            m = 1 << i
            mask = (xm >= m) & (xm < 2 * m)
            for h, sl in enumerate(sls):
                if m == 1:
                    decay = jnp.where(odd_row, ctx["f"][:, sl], 1.0).astype(BF16)
                else:
                    decay = jnp.exp2(_neg_gap(ctx["bh"][h], m)).astype(BF16)
                q_l, k_l = ctx["q"][h] * decay, ctx["kb"][h] * decay
                for qi, q0 in enumerate((0, half)):
                    s_l = lax.dot_general(q_l[q0:q0 + half], k_l[q0:q0 + half], NT_DIMS,
                                          preferred_element_type=F32)
                    ps[h][qi] = jnp.where(mask, s_l, 0.0 if ps[h][qi] is None else ps[h][qi])
        ctx["diag"] = [[p.astype(BF16) for p in pair] for pair in ps]

    def head(h):
        sl = slice(h * HG_DIM, (h + 1) * HG_DIM)
        q, v, kb, b, diag = ctx["q"][h], ctx["v"][h], ctx["kb"][h], ctx["bh"][h], ctx["diag"][h]
        mid = b[half - 1:half, :]
        cross = lax.dot_general(q[half:] * jnp.exp2(b[half:] - mid).astype(BF16),
                                kb[:half] * jnp.exp2(mid - b[:half]).astype(BF16), NT_DIMS,
                                preferred_element_type=F32).astype(BF16)
        o = jnp.concatenate([_dot(diag[0], v[:half]),
                             _dot(jnp.concatenate([cross, diag[1]], axis=1), v)], axis=0)
        o = o + jnp.sum((q * kb).astype(F32), axis=-1, keepdims=True) * v.astype(F32)

        st = st_ref[h]
        b_last = b[t - 1:t, :]
        o = o + lax.dot_general(q * jnp.exp2(b).astype(BF16), st.astype(BF16), NT_DIMS,
                                preferred_element_type=F32)
        k_end = kb * jnp.exp2(b_last - b).astype(BF16)
        st_ref[h] = st * jnp.exp2(b_last) + lax.dot_general(v, k_end, TN_DIMS, preferred_element_type=F32)

        gate = hg_ref[rows, sl].astype(F32)
        o_ref[rows, sl] = (_rms(o, gn_ref[h:h + 1, :]) * (gate * jax.nn.sigmoid(gate))).astype(o_ref.dtype)

    def heads():
        for h in range(HG_HEADS):
            head(h)

    return [gates, levels, heads]


def _attn_kernel(q_ref, k_ref, v_ref, wg_ref, wu_ref, wd_ref, o_ref, wg_bf_ref, wu_bf_ref, wd_bf_ref,
                 vaug_ref, *, tq):
    wg_bf_ref[...] = wg_ref[...].astype(wg_bf_ref.dtype)
    wu_bf_ref[...] = wu_ref[...].astype(wu_bf_ref.dtype)
    wd_bf_ref[...] = wd_ref[...].astype(wd_bf_ref.dtype)

    seq = q_ref.shape[0]
    for h in range(MLA_HEADS):
        vaug_ref[h, :, 0:V_DIM] = v_ref[:, h * V_DIM:(h + 1) * V_DIM]
        vaug_ref[h, :, V_DIM:2 * V_DIM] = jnp.ones((seq, V_DIM), vaug_ref.dtype)
    row = lax.broadcasted_iota(jnp.int32, (tq, tq), 0)
    col = lax.broadcasted_iota(jnp.int32, (tq, tq), 1)

    def scores(job):
        h, i = job
        lo = i * tq
        cols = slice(h * HEAD_PAD, (h + 1) * HEAD_PAD)
        q = q_ref[lo:lo + tq, cols]
        s_d = lax.dot_general(q, k_ref[lo:lo + tq, cols], NT_DIMS, preferred_element_type=F32)
        s_d = jnp.where(row >= col, s_d, -jnp.inf)
        s_p = lax.dot_general(q, k_ref[0:lo, cols], NT_DIMS, preferred_element_type=F32) if i > 0 else None
        return s_d, s_p

    jobs = [(h, i) for i in range(seq // tq) for h in range(MLA_HEADS)]
    pending = [scores(job) for job in jobs[:ATTN_LOOKAHEAD]]
    for n_done, (h, i) in enumerate(jobs):
        lo = i * tq
        s_d, s_p = pending.pop(0)
        if n_done + ATTN_LOOKAHEAD < len(jobs):
            pending.append(scores(jobs[n_done + ATTN_LOOKAHEAD]))
        m = jnp.max(s_d, axis=-1, keepdims=True)
        if i > 0:
            m = jnp.maximum(m, jnp.max(s_p, axis=-1, keepdims=True))
        acc = _dot(jnp.exp2(s_d - m).astype(BF16), vaug_ref[h, lo:lo + tq, :])
        if i > 0:
            acc += _dot(jnp.exp2(s_p - m).astype(BF16), vaug_ref[h, 0:lo, :])
        o_ref[lo:lo + tq, h * V_DIM:(h + 1) * V_DIM] = (
            acc[:, 0:V_DIM] / acc[:, V_DIM:2 * V_DIM]).astype(o_ref.dtype)


def _postmix_kernel(h1_ref, ohg_ref, omla_ref, p_ref, wo_ref, g2_ref, wg_ref, wu_ref, wd_ref,
                    gple_ref, wpg_ref, wpp_ref, gfin_ref, y_ref):
    tp = h1_ref.shape[0]
    parts = [slice(r, r + TOKEN_TILE) for r in range(0, tp, TOKEN_TILE)] if tp % TOKEN_TILE == 0 else [slice(0, tp)]

    def tail_pieces(rows, h3):
        ctx = {}

        def gate():
            ctx["gate"] = jax.nn.sigmoid(_dot(_rms(h3, gple_ref[...]).astype(BF16), wpg_ref[...]))

        def finish():
            h4 = h3 + ctx["gate"] * _dot(p_ref[rows, :].astype(BF16), wpp_ref[...])
            y_ref[rows, :] = _rms(h4, gfin_ref[...])

        return [gate, finish]

    pending = []
    for rows in parts:
        h2 = (h1_ref[rows, :] + _dot(ohg_ref[rows, :], wo_ref[0:HG_WIDTH, :])
              + _dot(omla_ref[rows, :], wo_ref[HG_WIDTH:HG_WIDTH + MLA_HEADS * V_DIM, :]))
        slots = [c for c in POSTMIX_TAIL_AFTER_FFN_CHUNKS if c < D_FF // FFN_CHUNK][:len(pending)]
        between = {c: [piece] for c, piece in zip(slots, pending)}
        h3 = h2 + 0.5 * _swiglu(_rms(h2, g2_ref[...]).astype(BF16), wg_ref, wu_ref, wd_ref, between)
        for piece in pending[len(slots):]:
            piece()
        pending = tail_pieces(rows, h3)
    for piece in pending:
        piece()


def _resident(shape):
    return pl.BlockSpec(shape, lambda *_: (0,) * len(shape), pipeline_mode=pl.Buffered(1))


def _rows(tile, width):
    return pl.BlockSpec((tile, width), lambda i: (i, 0))


def kernel(x, p, positions, ln_ffn1, w1_gate, w1_up, w1_down, ln_mix, w_in, hg_lb_logits, hg_out_norm,
           q_a_norm, w_q_up, kv_a_norm, w_kv_up, w_out, ln_ffn2, w2_gate, w2_up, w2_down, ln_ple,
           w_ple_gate, w_ple_proj, ln_final):
    bsz, seq, _ = x.shape
    assert p.shape[0] == 1 and hg_lb_logits.shape[0] == 2, "single-layer trunk"
    n = bsz * seq
    tm = min(TOKEN_TILE, seq)
    tc = min(HGRN_CHUNK, seq)
    tq = min(ATTN_BLOCK, seq)
    tp = min(POSTMIX_TILE, n)
    assert seq % tm == 0 and tm % tc == 0 and seq % tq == 0 and n % tp == 0

    x2 = x.reshape(n, D_MODEL)
    pos2 = positions.astype(F32).reshape(n // tm, 1, tm)
    half = ROPE_DIM // 2
    inv_freq = ROPE_THETA ** (-jnp.arange(half, dtype=F32) / half)
    invf = jnp.concatenate([inv_freq, inv_freq, jnp.zeros((LANES - ROPE_DIM,), F32)]).reshape(1, LANES)
    wg1, wu1, wd1 = w1_gate[0].astype(BF16), w1_up[0].astype(BF16), w1_down[0].astype(BF16)
    win = w_in[0].astype(BF16)
    wq = jnp.pad(w_q_up[0].astype(BF16).reshape(Q_LORA, MLA_HEADS, QK_DIM),
                 ((0, 0), (0, 0), (0, HEAD_PAD - QK_DIM))).reshape(Q_LORA, MLA_HEADS * HEAD_PAD)
    wkv = w_kv_up[0].astype(BF16).reshape(KV_LORA, MLA_HEADS, NOPE_DIM + V_DIM)
    wk = wkv[:, :, :NOPE_DIM].reshape(KV_LORA, MLA_HEADS * NOPE_DIM)
    wv = wkv[:, :, NOPE_DIM:].reshape(KV_LORA, MLA_HEADS * V_DIM)
    wo = w_out[0].astype(BF16)
    row = lambda a: a.reshape(1, -1)
    layer0 = lambda w: w.reshape(w.shape[1:])

    steps = n // tm
    tile = lambda width: pl.BlockSpec((tm, width), lambda i: (jnp.minimum(i, steps - 1), 0))
    prev_tile = pl.BlockSpec((tm, HG_WIDTH), lambda i: (jnp.maximum(i - 1, 0), 0))
    h1, qp, kp, vv, o_hg = pl.pallas_call(
        functools.partial(_premix_kernel, tiles_per_row=seq // tm, chunk=tc),
        name="premix",
        grid=(steps + 1,),
        in_specs=[
            tile(D_MODEL), pl.BlockSpec((None, 1, tm), lambda i: (jnp.minimum(i, steps - 1), 0, 0)),
            _resident((1, LANES)), _resident((1, D_MODEL)),
            _resident((D_MODEL, D_FF)), _resident((D_MODEL, D_FF)), _resident((D_FF, D_MODEL)),
            _resident((1, D_MODEL)), _resident((D_MODEL, IN_WIDTH)),
            _resident((1, Q_LORA)), _resident((Q_LORA, MLA_HEADS * HEAD_PAD)), _resident((1, KV_LORA)),
            _resident((KV_LORA, MLA_HEADS * NOPE_DIM)), _resident((KV_LORA, MLA_HEADS * V_DIM)),
            _resident((2, HG_WIDTH)), _resident((HG_HEADS, HG_DIM)),
        ],
        out_specs=[tile(D_MODEL), tile(MLA_HEADS * HEAD_PAD), tile(MLA_HEADS * HEAD_PAD),
                   tile(MLA_HEADS * V_DIM), prev_tile],
        out_shape=[
            jax.ShapeDtypeStruct((n, D_MODEL), F32),
            jax.ShapeDtypeStruct((n, MLA_HEADS * HEAD_PAD), BF16),
            jax.ShapeDtypeStruct((n, MLA_HEADS * HEAD_PAD), BF16),
            jax.ShapeDtypeStruct((n, MLA_HEADS * V_DIM), BF16),
            jax.ShapeDtypeStruct((n, HG_WIDTH), BF16),
        ],
        scratch_shapes=[pltpu.VMEM((tm, HG_WIDTH), BF16), pltpu.VMEM((tm, HG_WIDTH), F32),
                        pltpu.VMEM((tm, HG_WIDTH), BF16), pltpu.VMEM((tm, HG_WIDTH), BF16),
                        pltpu.VMEM((HG_HEADS, HG_DIM, HG_DIM), F32)],
        compiler_params=pltpu.CompilerParams(dimension_semantics=("arbitrary",), vmem_limit_bytes=VMEM_LIMIT),
    )(x2, pos2, invf, row(ln_ffn1[0]), wg1, wu1, wd1, row(ln_mix[0]), win,
      row(q_a_norm[0]), wq, row(kv_a_norm[0]), wk, wv, hg_lb_logits, hg_out_norm[0])

    seq_spec = lambda width: pl.BlockSpec((seq, MLA_HEADS * width), lambda b: (b, 0))
    assert D_MODEL % (bsz * BF16_ROWS) == 0 and D_FF % (bsz * BF16_ROWS) == 0
    slab = lambda rows, cols: pl.BlockSpec((rows // bsz, cols), lambda b: (b, 0))
    o_mla, wg2, wu2, wd2 = pl.pallas_call(
        functools.partial(_attn_kernel, tq=tq),
        name="mla_attn",
        grid=(bsz,),
        in_specs=[seq_spec(HEAD_PAD), seq_spec(HEAD_PAD), seq_spec(V_DIM),
                  slab(D_MODEL, D_FF), slab(D_MODEL, D_FF), slab(D_FF, D_MODEL)],
        out_specs=[seq_spec(V_DIM), slab(D_MODEL, D_FF), slab(D_MODEL, D_FF), slab(D_FF, D_MODEL)],
        out_shape=[jax.ShapeDtypeStruct((n, MLA_HEADS * V_DIM), BF16),
                   jax.ShapeDtypeStruct((D_MODEL, D_FF), BF16), jax.ShapeDtypeStruct((D_MODEL, D_FF), BF16),
                   jax.ShapeDtypeStruct((D_FF, D_MODEL), BF16)],
        scratch_shapes=[pltpu.VMEM((MLA_HEADS, seq, 2 * V_DIM), BF16)],
        compiler_params=pltpu.CompilerParams(dimension_semantics=("parallel",), vmem_limit_bytes=VMEM_LIMIT),
    )(qp, kp, vv, layer0(w2_gate), layer0(w2_up), layer0(w2_down))

    y = pl.pallas_call(
        _postmix_kernel,
        name="postmix",
        grid=(n // tp,),
        in_specs=[
            _rows(tp, D_MODEL), _rows(tp, HG_WIDTH), _rows(tp, MLA_HEADS * V_DIM), _rows(tp, PLE_DIM),
            _resident((HG_WIDTH + MLA_HEADS * V_DIM, D_MODEL)), _resident((1, D_MODEL)),
            _resident((D_MODEL, D_FF)), _resident((D_MODEL, D_FF)), _resident((D_FF, D_MODEL)),
            _resident((1, D_MODEL)), _resident((D_MODEL, D_MODEL)), _resident((PLE_DIM, D_MODEL)),
            _resident((1, D_MODEL)),
        ],
        out_specs=_rows(tp, D_MODEL),
        out_shape=jax.ShapeDtypeStruct((n, D_MODEL), F32),
        compiler_params=pltpu.CompilerParams(dimension_semantics=("parallel",), vmem_limit_bytes=VMEM_LIMIT),
    )(h1, o_hg, o_mla, p[0].reshape(n, PLE_DIM), wo, row(ln_ffn2[0]),
      wg2, wu2, wd2, row(ln_ple[0]), w_ple_gate[0].astype(BF16), w_ple_proj[0].astype(BF16), row(ln_final))

    return y.reshape(bsz, seq, D_MODEL)
```

```python
import functools

import jax
import jax.numpy as jnp
from jax import lax
from jax.experimental import pallas as pl
from jax.experimental.pallas import tpu as pltpu

F32 = jnp.float32
BF16 = jnp.bfloat16

D_MODEL = 1024
D_FF = 2816
PLE_DIM = 256
HG_HEADS = 4
HG_DIM = 128
HG_WIDTH = HG_HEADS * HG_DIM
MLA_HEADS = 4
Q_LORA = 256
KV_LORA = 128
NOPE_DIM = 128
ROPE_DIM = 64
V_DIM = 128
QK_DIM = NOPE_DIM + ROPE_DIM
ROPE_THETA = 10000.0
EPS = 1e-6
LOG2_E = 1.4426950408889634

LANES = 128
SUBLANES = 8
BF16_ROWS = 2 * SUBLANES
HEAD_PAD = 2 * LANES
HG_GROUPS = 4
IN_WIDTH = HG_GROUPS * HG_WIDTH + Q_LORA + KV_LORA + ROPE_DIM

FFN_CHUNK = 256
TOKEN_TILE = 512
POSTMIX_TILE = 1024
HGRN_CHUNK = 256
ATTN_BLOCK = 256
ATTN_LOOKAHEAD = 3
POSTMIX_TAIL_AFTER_FFN_CHUNKS = (1, 4)
HGRN_AFTER_FFN_CHUNKS = (0, 2, 5, 7)
VMEM_LIMIT = 56 * 1024 * 1024

NT_DIMS = (((1,), (1,)), ((), ()))
TN_DIMS = (((0,), (0,)), ((), ()))


def _dot(a, b):
    return jnp.dot(a, b, preferred_element_type=F32)


def _rms(x, g):
    return x * lax.rsqrt(jnp.mean(x * x, axis=-1, keepdims=True) + EPS) * g


def _swiglu(xn, wg_ref, wu_ref, wd_ref, between=None):
    acc = None
    for idx, c in enumerate(range(0, D_FF, FFN_CHUNK)):
        g = _dot(xn, wg_ref[:, c:c + FFN_CHUNK])
        u = _dot(xn, wu_ref[:, c:c + FFN_CHUNK])
        a = (g * jax.nn.sigmoid(g) * u).astype(BF16)
        d = _dot(a, wd_ref[c:c + FFN_CHUNK, :])
        acc = d if acc is None else acc + d
        for piece in (between or {}).get(idx, ()):
            piece()
    return acc


def _rope(x, cos_t, sin_lo, sin_hi):
    return (x * cos_t + pltpu.roll(x, LANES - ROPE_DIM // 2, 1) * sin_lo
            + pltpu.roll(x, ROPE_DIM // 2, 1) * sin_hi)


def _premix_kernel(x_ref, pos_ref, invf_ref, g1_ref, wg_ref, wu_ref, wd_ref, gmix_ref, win_ref,
                   qan_ref, wq_ref, kvan_ref, wk_ref, wv_ref, lbl_ref, gn_ref,
                   h1_ref, qp_ref, kp_ref, v_ref, ohg_ref,
                   hq_s, hf_s, hi_s, hg_s, st_ref, *, tiles_per_row, chunk):
    i = pl.program_id(0)
    tm = x_ref.shape[0]

    @pl.when(i == 0)
    def _():
        hq_s[...] = jnp.zeros_like(hq_s)
        hf_s[...] = jnp.zeros_like(hf_s)
        hi_s[...] = jnp.zeros_like(hi_s)
        hg_s[...] = jnp.zeros_like(hg_s)
        st_ref[...] = jnp.zeros_like(st_ref)

    @pl.when(lax.rem(i + tiles_per_row - 1, tiles_per_row) == 0)
    def _():
        st_ref[...] = jnp.zeros_like(st_ref)

    pieces = []
    for r0 in range(0, tm, chunk):
        pieces += _hgrn_pieces(lbl_ref, hq_s, hf_s, hi_s, hg_s, gn_ref, ohg_ref, st_ref, r0, chunk)
    early_chunks = [c for c in HGRN_AFTER_FFN_CHUNKS if c < D_FF // FFN_CHUNK][:len(pieces)]
    early = {}
    for c, piece in zip(early_chunks, pieces):
        early.setdefault(c, []).append(piece)
    late = iter(pieces[len(early_chunks):])

    t_row = lax.broadcasted_iota(jnp.int32, (tm, tm), 0)
    t_col = lax.broadcasted_iota(jnp.int32, (tm, tm), 1)
    pos_col = jnp.sum(jnp.where(t_row == t_col, pos_ref[...], 0.0), axis=1, keepdims=True)
    ang = pos_col * invf_ref[...]
    lane = lax.broadcasted_iota(jnp.int32, ang.shape, 1)
    cos_a, sin_a = jnp.cos(ang), jnp.sin(ang)
    half = ROPE_DIM // 2
    cos_t = jnp.where(lane < ROPE_DIM, cos_a, 0.0)
    sin_lo = jnp.where(lane < half, -sin_a, 0.0)
    sin_hi = jnp.where((lane >= half) & (lane < ROPE_DIM), sin_a, 0.0)

    x = x_ref[...]
    xn = _rms(x, g1_ref[...]).astype(BF16)
    h1 = x + 0.5 * _swiglu(xn, wg_ref, wu_ref, wd_ref, early)
    h1_ref[...] = h1

    hn = _rms(h1, gmix_ref[...]).astype(BF16)
    w = HG_WIDTH
    new = []
    for j in range(HG_GROUPS):
        new.append(_dot(hn, win_ref[:, j * w:(j + 1) * w]))
        piece = next(late, None)
        if piece is not None:
            piece()
    for piece in late:
        piece()
    c0 = HG_GROUPS * w
    cq = _dot(hn, win_ref[:, c0:c0 + Q_LORA])
    ckv_kr = _dot(hn, win_ref[:, c0 + Q_LORA:IN_WIDTH])
    ckv = ckv_kr[:, 0:KV_LORA]
    kr = jnp.concatenate([ckv_kr[:, KV_LORA:KV_LORA + ROPE_DIM],
                          jnp.zeros((tm, LANES - ROPE_DIM), F32)], axis=1)

    q = _dot(_rms(cq, qan_ref[...]).astype(BF16), wq_ref[...]) * (QK_DIM ** -0.5 * LOG2_E)
    ckvn = _rms(ckv, kvan_ref[...]).astype(BF16)
    k_nope = _dot(ckvn, wk_ref[...])
    v_ref[...] = _dot(ckvn, wv_ref[...]).astype(v_ref.dtype)
    k_rope = _rope(kr, cos_t, sin_lo, sin_hi).astype(kp_ref.dtype)
    for h in range(MLA_HEADS):
        a = h * HEAD_PAD
        qp_ref[:, a:a + NOPE_DIM] = q[:, a:a + NOPE_DIM].astype(qp_ref.dtype)
        qp_ref[:, a + NOPE_DIM:a + HEAD_PAD] = _rope(
            q[:, a + NOPE_DIM:a + HEAD_PAD], cos_t, sin_lo, sin_hi).astype(qp_ref.dtype)
        kp_ref[:, a:a + NOPE_DIM] = k_nope[:, h * NOPE_DIM:(h + 1) * NOPE_DIM].astype(kp_ref.dtype)
        kp_ref[:, a + NOPE_DIM:a + HEAD_PAD] = k_rope

    hq_s[...] = new[0].astype(hq_s.dtype)
    hf_s[...] = new[1]
    hi_s[...] = new[2].astype(hi_s.dtype)
    hg_s[...] = new[3].astype(hg_s.dtype)


def _neg_gap(b, m):
    t = b.shape[0]
    if m >= SUBLANES:
        pieces = []
        for s in range(0, t, 2 * m):
            mid = b[s + m - 1:s + m, :]
            pieces += [mid - b[s:s + m], b[s + m:s + 2 * m] - mid]
        return jnp.concatenate(pieces, axis=0)
    b3 = b.reshape(t // SUBLANES, SUBLANES, LANES)
    sub = lax.broadcasted_iota(jnp.int32, b3.shape, 1)

    def row(i):
        return jnp.broadcast_to(b3[:, i:i + 1, :], b3.shape)

    mid = row(m - 1)
    for s in range(2 * m, SUBLANES, 2 * m):
        mid = jnp.where(sub >= s, row(s + m - 1), mid)
    return -jnp.abs(b3 - mid).reshape(t, LANES)


def _hgrn_pieces(lbl_ref, hq_ref, hf_ref, hi_ref, hg_ref, gn_ref, o_ref, st_ref, r0, t):
    rows = slice(r0, r0 + t)
    half = t // 2
    ctx = {}

    def gates():
        lg = lbl_ref[...]
        e = jnp.exp(lg - jnp.max(lg, axis=0, keepdims=True))
        lb = e[0:1, :] / jnp.sum(e, axis=0, keepdims=True)
        f_raw = hf_ref[rows, :]
        sig = jax.nn.sigmoid(f_raw)
        f = lb + (1.0 - lb) * sig
        g = jnp.log(f)
        ctx["f"] = f
        ctx["kk"] = (1.0 - lb) * (1.0 - sig)

        row = lax.broadcasted_iota(jnp.int32, (t, t), 0)
        col = lax.broadcasted_iota(jnp.int32, (t, t), 1)
        tri = (row >= col).astype(BF16)
        g1 = g.astype(BF16)
        r1 = g - g1.astype(F32)
        g2 = r1.astype(BF16)
        g3 = (r1 - g2.astype(F32)).astype(BF16)
        ctx["b"] = (_dot(tri, g1) + _dot(tri, g2) + _dot(tri, g3)) * LOG2_E

    def levels():
        ri = lax.broadcasted_iota(jnp.int32, (half, half), 0)
        ci = lax.broadcasted_iota(jnp.int32, (half, half), 1)
        xm = jnp.where(ri > ci, ri ^ ci, 0)
        odd_row = (lax.broadcasted_iota(jnp.int32, (t, HG_DIM), 0) & 1) == 1

        sls = [slice(h * HG_DIM, (h + 1) * HG_DIM) for h in range(HG_HEADS)]
        ctx["q"] = [hq_ref[rows, sl] for sl in sls]
        ctx["v"] = [hi_ref[rows, sl] for sl in sls]
        ctx["kb"] = [ctx["kk"][:, sl].astype(BF16) for sl in sls]
        ctx["bh"] = [ctx["b"][:, sl] for sl in sls]

        ps = [[None, None] for _ in sls]
        for i in range(half.bit_length() - 1):
            m = 1 << i
            mask = (xm >= m) & (xm < 2 * m)
            for h, sl in enumerate(sls):
                if m == 1:
                    decay = jnp.where(odd_row, ctx["f"][:, sl], 1.0).astype(BF16)
                else:
                    decay = jnp.exp2(_neg_gap(ctx["bh"][h], m)).astype(BF16)
                q_l, k_l = ctx["q"][h] * decay, ctx["kb"][h] * decay
                for qi, q0 in enumerate((0, half)):
                    s_l = lax.dot_general(q_l[q0:q0 + half], k_l[q0:q0 + half], NT_DIMS,
                                          preferred_element_type=F32)
                    ps[h][qi] = jnp.where(mask, s_l, 0.0 if ps[h][qi] is None else ps[h][qi])
        ctx["diag"] = [[p.astype(BF16) for p in pair] for pair in ps]

    def head(h):
        sl = slice(h * HG_DIM, (h + 1) * HG_DIM)
        q, v, kb, b, diag = ctx["q"][h], ctx["v"][h], ctx["kb"][h], ctx["bh"][h], ctx["diag"][h]
        mid = b[half - 1:half, :]
        cross = lax.dot_general(q[half:] * jnp.exp2(b[half:] - mid).astype(BF16),
                                kb[:half] * jnp.exp2(mid - b[:half]).astype(BF16), NT_DIMS,
                                preferred_element_type=F32).astype(BF16)
        o = jnp.concatenate([_dot(diag[0], v[:half]),
                             _dot(jnp.concatenate([cross, diag[1]], axis=1), v)], axis=0)
        o = o + jnp.sum((q * kb).astype(F32), axis=-1, keepdims=True) * v.astype(F32)

        st = st_ref[h]
        b_last = b[t - 1:t, :]
        o = o + lax.dot_general(q * jnp.exp2(b).astype(BF16), st.astype(BF16), NT_DIMS,
                                preferred_element_type=F32)
        k_end = kb * jnp.exp2(b_last - b).astype(BF16)
        st_ref[h] = st * jnp.exp2(b_last) + lax.dot_general(v, k_end, TN_DIMS, preferred_element_type=F32)

        gate = hg_ref[rows, sl].astype(F32)
        o_ref[rows, sl] = (_rms(o, gn_ref[h:h + 1, :]) * (gate * jax.nn.sigmoid(gate))).astype(o_ref.dtype)

    def heads():
        levels()
        for h in range(HG_HEADS):
            head(h)

    return [gates, heads]


def _attn_kernel(q_ref, k_ref, v_ref, wg_ref, wu_ref, wd_ref, o_ref, wg_bf_ref, wu_bf_ref, wd_bf_ref,
                 vaug_ref, *, tq):
    wg_bf_ref[...] = wg_ref[...].astype(wg_bf_ref.dtype)
    wu_bf_ref[...] = wu_ref[...].astype(wu_bf_ref.dtype)
    wd_bf_ref[...] = wd_ref[...].astype(wd_bf_ref.dtype)

    seq = q_ref.shape[0]
    for h in range(MLA_HEADS):
        vaug_ref[h, :, 0:V_DIM] = v_ref[:, h * V_DIM:(h + 1) * V_DIM]
        vaug_ref[h, :, V_DIM:2 * V_DIM] = jnp.ones((seq, V_DIM), vaug_ref.dtype)
    row = lax.broadcasted_iota(jnp.int32, (tq, tq), 0)
    col = lax.broadcasted_iota(jnp.int32, (tq, tq), 1)

    def scores(job):
        h, i = job
        lo = i * tq
        cols = slice(h * HEAD_PAD, (h + 1) * HEAD_PAD)
        q = q_ref[lo:lo + tq, cols]
        s_d = lax.dot_general(q, k_ref[lo:lo + tq, cols], NT_DIMS, preferred_element_type=F32)
        s_d = jnp.where(row >= col, s_d, -jnp.inf)
        s_p = lax.dot_general(q, k_ref[0:lo, cols], NT_DIMS, preferred_element_type=F32) if i > 0 else None
        return s_d, s_p

    jobs = [(h, i) for i in range(seq // tq) for h in range(MLA_HEADS)]
    pending = [scores(job) for job in jobs[:ATTN_LOOKAHEAD]]
    for n_done, (h, i) in enumerate(jobs):
        lo = i * tq
        s_d, s_p = pending.pop(0)
        if n_done + ATTN_LOOKAHEAD < len(jobs):
            pending.append(scores(jobs[n_done + ATTN_LOOKAHEAD]))
        m = jnp.max(s_d, axis=-1, keepdims=True)
        if i > 0:
            m = jnp.maximum(m, jnp.max(s_p, axis=-1, keepdims=True))
        acc = _dot(jnp.exp2(s_d - m).astype(BF16), vaug_ref[h, lo:lo + tq, :])
        if i > 0:
            acc += _dot(jnp.exp2(s_p - m).astype(BF16), vaug_ref[h, 0:lo, :])
        o_ref[lo:lo + tq, h * V_DIM:(h + 1) * V_DIM] = (
            acc[:, 0:V_DIM] / acc[:, V_DIM:2 * V_DIM]).astype(o_ref.dtype)


def _postmix_kernel(h1_ref, ohg_ref, omla_ref, p_ref, wo_ref, g2_ref, wg_ref, wu_ref, wd_ref,
                    gple_ref, wpg_ref, wpp_ref, gfin_ref, y_ref):
    tp = h1_ref.shape[0]
    parts = [slice(r, r + TOKEN_TILE) for r in range(0, tp, TOKEN_TILE)] if tp % TOKEN_TILE == 0 else [slice(0, tp)]

    def tail_pieces(rows, h3):
        ctx = {}

        def gate():
            ctx["gate"] = jax.nn.sigmoid(_dot(_rms(h3, gple_ref[...]).astype(BF16), wpg_ref[...]))

        def finish():
            h4 = h3 + ctx["gate"] * _dot(p_ref[rows, :].astype(BF16), wpp_ref[...])
            y_ref[rows, :] = _rms(h4, gfin_ref[...])

        return [gate, finish]

    pending = []
    for rows in parts:
        h2 = (h1_ref[rows, :] + _dot(ohg_ref[rows, :], wo_ref[0:HG_WIDTH, :])
              + _dot(omla_ref[rows, :], wo_ref[HG_WIDTH:HG_WIDTH + MLA_HEADS * V_DIM, :]))
        slots = [c for c in POSTMIX_TAIL_AFTER_FFN_CHUNKS if c < D_FF // FFN_CHUNK][:len(pending)]
        between = {c: [piece] for c, piece in zip(slots, pending)}
        h3 = h2 + 0.5 * _swiglu(_rms(h2, g2_ref[...]).astype(BF16), wg_ref, wu_ref, wd_ref, between)
        for piece in pending[len(slots):]:
            piece()
        pending = tail_pieces(rows, h3)
    for piece in pending:
        piece()


def _resident(shape):
    return pl.BlockSpec(shape, lambda *_: (0,) * len(shape), pipeline_mode=pl.Buffered(1))


def _rows(tile, width):
    return pl.BlockSpec((tile, width), lambda i: (i, 0))


def kernel(x, p, positions, ln_ffn1, w1_gate, w1_up, w1_down, ln_mix, w_in, hg_lb_logits, hg_out_norm,
           q_a_norm, w_q_up, kv_a_norm, w_kv_up, w_out, ln_ffn2, w2_gate, w2_up, w2_down, ln_ple,
           w_ple_gate, w_ple_proj, ln_final):
    bsz, seq, _ = x.shape
    assert p.shape[0] == 1 and hg_lb_logits.shape[0] == 2, "single-layer trunk"
    n = bsz * seq
    tm = min(TOKEN_TILE, seq)
    tc = min(HGRN_CHUNK, seq)
    tq = min(ATTN_BLOCK, seq)
    tp = min(POSTMIX_TILE, n)
    assert seq % tm == 0 and tm % tc == 0 and seq % tq == 0 and n % tp == 0

    x2 = x.reshape(n, D_MODEL)
    pos2 = positions.astype(F32).reshape(n // tm, 1, tm)
    half = ROPE_DIM // 2
    inv_freq = ROPE_THETA ** (-jnp.arange(half, dtype=F32) / half)
    invf = jnp.concatenate([inv_freq, inv_freq, jnp.zeros((LANES - ROPE_DIM,), F32)]).reshape(1, LANES)
    wg1, wu1, wd1 = w1_gate[0].astype(BF16), w1_up[0].astype(BF16), w1_down[0].astype(BF16)
    win = w_in[0].astype(BF16)
    wq = jnp.pad(w_q_up[0].astype(BF16).reshape(Q_LORA, MLA_HEADS, QK_DIM),
                 ((0, 0), (0, 0), (0, HEAD_PAD - QK_DIM))).reshape(Q_LORA, MLA_HEADS * HEAD_PAD)
    wkv = w_kv_up[0].astype(BF16).reshape(KV_LORA, MLA_HEADS, NOPE_DIM + V_DIM)
    wk = wkv[:, :, :NOPE_DIM].reshape(KV_LORA, MLA_HEADS * NOPE_DIM)
    wv = wkv[:, :, NOPE_DIM:].reshape(KV_LORA, MLA_HEADS * V_DIM)
    wo = w_out[0].astype(BF16)
    row = lambda a: a.reshape(1, -1)
    layer0 = lambda w: w.reshape(w.shape[1:])

    steps = n // tm
    tile = lambda width: pl.BlockSpec((tm, width), lambda i: (jnp.minimum(i, steps - 1), 0))
    prev_tile = pl.BlockSpec((tm, HG_WIDTH), lambda i: (jnp.maximum(i - 1, 0), 0))
    h1, qp, kp, vv, o_hg = pl.pallas_call(
        functools.partial(_premix_kernel, tiles_per_row=seq // tm, chunk=tc),
        name="premix",
        grid=(steps + 1,),
        in_specs=[
            tile(D_MODEL), pl.BlockSpec((None, 1, tm), lambda i: (jnp.minimum(i, steps - 1), 0, 0)),
            _resident((1, LANES)), _resident((1, D_MODEL)),
            _resident((D_MODEL, D_FF)), _resident((D_MODEL, D_FF)), _resident((D_FF, D_MODEL)),
            _resident((1, D_MODEL)), _resident((D_MODEL, IN_WIDTH)),
            _resident((1, Q_LORA)), _resident((Q_LORA, MLA_HEADS * HEAD_PAD)), _resident((1, KV_LORA)),
            _resident((KV_LORA, MLA_HEADS * NOPE_DIM)), _resident((KV_LORA, MLA_HEADS * V_DIM)),
            _resident((2, HG_WIDTH)), _resident((HG_HEADS, HG_DIM)),
        ],
        out_specs=[tile(D_MODEL), tile(MLA_HEADS * HEAD_PAD), tile(MLA_HEADS * HEAD_PAD),
                   tile(MLA_HEADS * V_DIM), prev_tile],
        out_shape=[
            jax.ShapeDtypeStruct((n, D_MODEL), F32),
            jax.ShapeDtypeStruct((n, MLA_HEADS * HEAD_PAD), BF16),
            jax.ShapeDtypeStruct((n, MLA_HEADS * HEAD_PAD), BF16),
            jax.ShapeDtypeStruct((n, MLA_HEADS * V_DIM), BF16),
            jax.ShapeDtypeStruct((n, HG_WIDTH), BF16),
        ],
        scratch_shapes=[pltpu.VMEM((tm, HG_WIDTH), BF16), pltpu.VMEM((tm, HG_WIDTH), F32),
                        pltpu.VMEM((tm, HG_WIDTH), BF16), pltpu.VMEM((tm, HG_WIDTH), BF16),
                        pltpu.VMEM((HG_HEADS, HG_DIM, HG_DIM), F32)],
        compiler_params=pltpu.CompilerParams(dimension_semantics=("arbitrary",), vmem_limit_bytes=VMEM_LIMIT),
    )(x2, pos2, invf, row(ln_ffn1[0]), wg1, wu1, wd1, row(ln_mix[0]), win,
      row(q_a_norm[0]), wq, row(kv_a_norm[0]), wk, wv, hg_lb_logits, hg_out_norm[0])

    seq_spec = lambda width: pl.BlockSpec((seq, MLA_HEADS * width), lambda b: (b, 0))
    assert D_MODEL % (bsz * BF16_ROWS) == 0 and D_FF % (bsz * BF16_ROWS) == 0
    slab = lambda rows, cols: pl.BlockSpec((rows // bsz, cols), lambda b: (b, 0))
    o_mla, wg2, wu2, wd2 = pl.pallas_call(
        functools.partial(_attn_kernel, tq=tq),
        name="mla_attn",
        grid=(bsz,),
        in_specs=[seq_spec(HEAD_PAD), seq_spec(HEAD_PAD), seq_spec(V_DIM),
                  slab(D_MODEL, D_FF), slab(D_MODEL, D_FF), slab(D_FF, D_MODEL)],
        out_specs=[seq_spec(V_DIM), slab(D_MODEL, D_FF), slab(D_MODEL, D_FF), slab(D_FF, D_MODEL)],
        out_shape=[jax.ShapeDtypeStruct((n, MLA_HEADS * V_DIM), BF16),
                   jax.ShapeDtypeStruct((D_MODEL, D_FF), BF16), jax.ShapeDtypeStruct((D_MODEL, D_FF), BF16),
                   jax.ShapeDtypeStruct((D_FF, D_MODEL), BF16)],
        scratch_shapes=[pltpu.VMEM((MLA_HEADS, seq, 2 * V_DIM), BF16)],
        compiler_params=pltpu.CompilerParams(dimension_semantics=("parallel",), vmem_limit_bytes=VMEM_LIMIT),
    )(qp, kp, vv, layer0(w2_gate), layer0(w2_up), layer0(w2_down))

    y = pl.pallas_call(
        _postmix_kernel,
        name="postmix",
        grid=(n // tp,),
        in_specs=[
            _rows(tp, D_MODEL), _rows(tp, HG_WIDTH), _rows(tp, MLA_HEADS * V_DIM), _rows(tp, PLE_DIM),
            _resident((HG_WIDTH + MLA_HEADS * V_DIM, D_MODEL)), _resident((1, D_MODEL)),
            _resident((D_MODEL, D_FF)), _resident((D_MODEL, D_FF)), _resident((D_FF, D_MODEL)),
            _resident((1, D_MODEL)), _resident((D_MODEL, D_MODEL)), _resident((PLE_DIM, D_MODEL)),
            _resident((1, D_MODEL)),
        ],
        out_specs=_rows(tp, D_MODEL),
        out_shape=jax.ShapeDtypeStruct((n, D_MODEL), F32),
        compiler_params=pltpu.CompilerParams(dimension_semantics=("parallel",), vmem_limit_bytes=VMEM_LIMIT),
    )(h1, o_hg, o_mla, p[0].reshape(n, PLE_DIM), wo, row(ln_ffn2[0]),
      wg2, wu2, wd2, row(ln_ple[0]), w_ple_gate[0].astype(BF16), w_ple_proj[0].astype(BF16), row(ln_final))

    return y.reshape(bsz, seq, D_MODEL)
```

```python
import functools

import jax
import jax.numpy as jnp
from jax import lax
from jax.experimental import pallas as pl
from jax.experimental.pallas import tpu as pltpu

F32 = jnp.float32
BF16 = jnp.bfloat16

D_MODEL = 1024
D_FF = 2816
PLE_DIM = 256
HG_HEADS = 4
HG_DIM = 128
HG_WIDTH = HG_HEADS * HG_DIM
MLA_HEADS = 4
Q_LORA = 256
KV_LORA = 128
NOPE_DIM = 128
ROPE_DIM = 64
V_DIM = 128
QK_DIM = NOPE_DIM + ROPE_DIM
ROPE_THETA = 10000.0
EPS = 1e-6
LOG2_E = 1.4426950408889634

LANES = 128
SUBLANES = 8
BF16_ROWS = 2 * SUBLANES
HEAD_PAD = 2 * LANES
HG_GROUPS = 4
IN_WIDTH = HG_GROUPS * HG_WIDTH + Q_LORA + KV_LORA + ROPE_DIM

FFN_CHUNK = 256
TOKEN_TILE = 512
POSTMIX_TILE = 1024
HGRN_CHUNK = 256
ATTN_BLOCK = 256
ATTN_LOOKAHEAD = 3
POSTMIX_TAIL_AFTER_FFN_CHUNKS = (1, 4)
HGRN_AFTER_FFN_CHUNKS = (0, 3, 6, 9)
VMEM_LIMIT = 56 * 1024 * 1024

NT_DIMS = (((1,), (1,)), ((), ()))
TN_DIMS = (((0,), (0,)), ((), ()))


def _dot(a, b):
    return jnp.dot(a, b, preferred_element_type=F32)


def _rms(x, g):
    return x * lax.rsqrt(jnp.mean(x * x, axis=-1, keepdims=True) + EPS) * g


def _swiglu(xn, wg_ref, wu_ref, wd_ref, between=None):
    acc = None
    for idx, c in enumerate(range(0, D_FF, FFN_CHUNK)):
        g = _dot(xn, wg_ref[:, c:c + FFN_CHUNK])
        u = _dot(xn, wu_ref[:, c:c + FFN_CHUNK])
        a = (g * jax.nn.sigmoid(g) * u).astype(BF16)
        d = _dot(a, wd_ref[c:c + FFN_CHUNK, :])
        acc = d if acc is None else acc + d
        for piece in (between or {}).get(idx, ()):
            piece()
    return acc


def _rope(x, cos_t, sin_lo, sin_hi):
    return (x * cos_t + pltpu.roll(x, LANES - ROPE_DIM // 2, 1) * sin_lo
            + pltpu.roll(x, ROPE_DIM // 2, 1) * sin_hi)


def _premix_kernel(x_ref, pos_ref, invf_ref, g1_ref, wg_ref, wu_ref, wd_ref, gmix_ref, win_ref,
                   qan_ref, wq_ref, kvan_ref, wk_ref, wv_ref, lbl_ref, gn_ref,
                   h1_ref, qp_ref, kp_ref, v_ref, ohg_ref,
                   hq_s, hf_s, hi_s, hg_s, st_ref, *, tiles_per_row, chunk):
    i = pl.program_id(0)
    tm = x_ref.shape[0]

    @pl.when(i == 0)
    def _():
        hq_s[...] = jnp.zeros_like(hq_s)
        hf_s[...] = jnp.zeros_like(hf_s)
        hi_s[...] = jnp.zeros_like(hi_s)
        hg_s[...] = jnp.zeros_like(hg_s)
        st_ref[...] = jnp.zeros_like(st_ref)

    @pl.when(lax.rem(i + tiles_per_row - 1, tiles_per_row) == 0)
    def _():
        st_ref[...] = jnp.zeros_like(st_ref)

    pieces = []
    for r0 in range(0, tm, chunk):
        pieces += _hgrn_pieces(lbl_ref, hq_s, hf_s, hi_s, hg_s, gn_ref, ohg_ref, st_ref, r0, chunk)
    early_chunks = [c for c in HGRN_AFTER_FFN_CHUNKS if c < D_FF // FFN_CHUNK][:len(pieces)]
    early = {}
    for c, piece in zip(early_chunks, pieces):
        early.setdefault(c, []).append(piece)
    late = iter(pieces[len(early_chunks):])

    t_row = lax.broadcasted_iota(jnp.int32, (tm, tm), 0)
    t_col = lax.broadcasted_iota(jnp.int32, (tm, tm), 1)
    pos_col = jnp.sum(jnp.where(t_row == t_col, pos_ref[...], 0.0), axis=1, keepdims=True)
    ang = pos_col * invf_ref[...]
    lane = lax.broadcasted_iota(jnp.int32, ang.shape, 1)
    cos_a, sin_a = jnp.cos(ang), jnp.sin(ang)
    half = ROPE_DIM // 2
    cos_t = jnp.where(lane < ROPE_DIM, cos_a, 0.0)
    sin_lo = jnp.where(lane < half, -sin_a, 0.0)
    sin_hi = jnp.where((lane >= half) & (lane < ROPE_DIM), sin_a, 0.0)

    x = x_ref[...]
    xn = _rms(x, g1_ref[...]).astype(BF16)
    h1 = x + 0.5 * _swiglu(xn, wg_ref, wu_ref, wd_ref, early)
    h1_ref[...] = h1

    hn = _rms(h1, gmix_ref[...]).astype(BF16)
    w = HG_WIDTH
    new = []
    for j in range(HG_GROUPS):
        new.append(_dot(hn, win_ref[:, j * w:(j + 1) * w]))
        piece = next(late, None)
        if piece is not None:
            piece()
    for piece in late:
        piece()
    c0 = HG_GROUPS * w
    cq = _dot(hn, win_ref[:, c0:c0 + Q_LORA])
    ckv_kr = _dot(hn, win_ref[:, c0 + Q_LORA:IN_WIDTH])
    ckv = ckv_kr[:, 0:KV_LORA]
    kr = jnp.concatenate([ckv_kr[:, KV_LORA:KV_LORA + ROPE_DIM],
                          jnp.zeros((tm, LANES - ROPE_DIM), F32)], axis=1)

    q = _dot(_rms(cq, qan_ref[...]).astype(BF16), wq_ref[...]) * (QK_DIM ** -0.5 * LOG2_E)
    ckvn = _rms(ckv, kvan_ref[...]).astype(BF16)
    k_nope = _dot(ckvn, wk_ref[...])
    v_ref[...] = _dot(ckvn, wv_ref[...]).astype(v_ref.dtype)
    k_rope = _rope(kr, cos_t, sin_lo, sin_hi).astype(kp_ref.dtype)
    for h in range(MLA_HEADS):
        a = h * HEAD_PAD
        qp_ref[:, a:a + NOPE_DIM] = q[:, a:a + NOPE_DIM].astype(qp_ref.dtype)
        qp_ref[:, a + NOPE_DIM:a + HEAD_PAD] = _rope(
            q[:, a + NOPE_DIM:a + HEAD_PAD], cos_t, sin_lo, sin_hi).astype(qp_ref.dtype)
        kp_ref[:, a:a + NOPE_DIM] = k_nope[:, h * NOPE_DIM:(h + 1) * NOPE_DIM].astype(kp_ref.dtype)
        kp_ref[:, a + NOPE_DIM:a + HEAD_PAD] = k_rope

    hq_s[...] = new[0].astype(hq_s.dtype)
    hf_s[...] = new[1]
    hi_s[...] = new[2].astype(hi_s.dtype)
    hg_s[...] = new[3].astype(hg_s.dtype)


def _neg_gap(b, m):
    t = b.shape[0]
    if m >= SUBLANES:
        pieces = []
        for s in range(0, t, 2 * m):
            mid = b[s + m - 1:s + m, :]
            pieces += [mid - b[s:s + m], b[s + m:s + 2 * m] - mid]
        return jnp.concatenate(pieces, axis=0)
    b3 = b.reshape(t // SUBLANES, SUBLANES, LANES)
    sub = lax.broadcasted_iota(jnp.int32, b3.shape, 1)

    def row(i):
        return jnp.broadcast_to(b3[:, i:i + 1, :], b3.shape)

    mid = row(m - 1)
    for s in range(2 * m, SUBLANES, 2 * m):
        mid = jnp.where(sub >= s, row(s + m - 1), mid)
    return -jnp.abs(b3 - mid).reshape(t, LANES)


def _hgrn_pieces(lbl_ref, hq_ref, hf_ref, hi_ref, hg_ref, gn_ref, o_ref, st_ref, r0, t):
    rows = slice(r0, r0 + t)
    half = t // 2
    ctx = {}

    def gates():
        lg = lbl_ref[...]
        e = jnp.exp(lg - jnp.max(lg, axis=0, keepdims=True))
        lb = e[0:1, :] / jnp.sum(e, axis=0, keepdims=True)
        f_raw = hf_ref[rows, :]
        sig = jax.nn.sigmoid(f_raw)
        f = lb + (1.0 - lb) * sig
        g = jnp.log(f)
        ctx["f"] = f
        ctx["kk"] = (1.0 - lb) * (1.0 - sig)

        row = lax.broadcasted_iota(jnp.int32, (t, t), 0)
        col = lax.broadcasted_iota(jnp.int32, (t, t), 1)
        tri = (row >= col).astype(BF16)
        g1 = g.astype(BF16)
        r1 = g - g1.astype(F32)
        g2 = r1.astype(BF16)
        g3 = (r1 - g2.astype(F32)).astype(BF16)
        ctx["b"] = (_dot(tri, g1) + _dot(tri, g2) + _dot(tri, g3)) * LOG2_E

    def levels():
        ri = lax.broadcasted_iota(jnp.int32, (half, half), 0)
        ci = lax.broadcasted_iota(jnp.int32, (half, half), 1)
        xm = jnp.where(ri > ci, ri ^ ci, 0)
        odd_row = (lax.broadcasted_iota(jnp.int32, (t, HG_DIM), 0) & 1) == 1

        sls = [slice(h * HG_DIM, (h + 1) * HG_DIM) for h in range(HG_HEADS)]
        ctx["q"] = [hq_ref[rows, sl] for sl in sls]
        ctx["v"] = [hi_ref[rows, sl] for sl in sls]
        ctx["kb"] = [ctx["kk"][:, sl].astype(BF16) for sl in sls]
        ctx["bh"] = [ctx["b"][:, sl] for sl in sls]

        ps = [[None, None] for _ in sls]
        for i in range(half.bit_length() - 1):
            m = 1 << i
            mask = (xm >= m) & (xm < 2 * m)
            for h, sl in enumerate(sls):
                if m == 1:
                    decay = jnp.where(odd_row, ctx["f"][:, sl], 1.0).astype(BF16)
                else:
                    decay = jnp.exp2(_neg_gap(ctx["bh"][h], m)).astype(BF16)
                q_l, k_l = ctx["q"][h] * decay, ctx["kb"][h] * decay
                for qi, q0 in enumerate((0, half)):
                    s_l = lax.dot_general(q_l[q0:q0 + half], k_l[q0:q0 + half], NT_DIMS,
                                          preferred_element_type=F32)
                    ps[h][qi] = jnp.where(mask, s_l, 0.0 if ps[h][qi] is None else ps[h][qi])
        ctx["diag"] = [[p.astype(BF16) for p in pair] for pair in ps]

    def head(h):
        sl = slice(h * HG_DIM, (h + 1) * HG_DIM)
        q, v, kb, b, diag = ctx["q"][h], ctx["v"][h], ctx["kb"][h], ctx["bh"][h], ctx["diag"][h]
        mid = b[half - 1:half, :]
        cross = lax.dot_general(q[half:] * jnp.exp2(b[half:] - mid).astype(BF16),
                                kb[:half] * jnp.exp2(mid - b[:half]).astype(BF16), NT_DIMS,
                                preferred_element_type=F32).astype(BF16)
        o = jnp.concatenate([_dot(diag[0], v[:half]),
                             _dot(jnp.concatenate([cross, diag[1]], axis=1), v)], axis=0)
        o = o + jnp.sum((q * kb).astype(F32), axis=-1, keepdims=True) * v.astype(F32)

        st = st_ref[h]
        b_last = b[t - 1:t, :]
        o = o + lax.dot_general(q * jnp.exp2(b).astype(BF16), st.astype(BF16), NT_DIMS,
                                preferred_element_type=F32)
        k_end = kb * jnp.exp2(b_last - b).astype(BF16)
        st_ref[h] = st * jnp.exp2(b_last) + lax.dot_general(v, k_end, TN_DIMS, preferred_element_type=F32)

        gate = hg_ref[rows, sl].astype(F32)
        o_ref[rows, sl] = (_rms(o, gn_ref[h:h + 1, :]) * (gate * jax.nn.sigmoid(gate))).astype(o_ref.dtype)

    def heads():
        levels()
        for h in range(HG_HEADS):
            head(h)

    return [gates, heads]


def _attn_kernel(q_ref, k_ref, v_ref, wg_ref, wu_ref, wd_ref, o_ref, wg_bf_ref, wu_bf_ref, wd_bf_ref,
                 vaug_ref, *, tq):
    wg_bf_ref[...] = wg_ref[...].astype(wg_bf_ref.dtype)
    wu_bf_ref[...] = wu_ref[...].astype(wu_bf_ref.dtype)
    wd_bf_ref[...] = wd_ref[...].astype(wd_bf_ref.dtype)

    seq = q_ref.shape[0]
    for h in range(MLA_HEADS):
        vaug_ref[h, :, 0:V_DIM] = v_ref[:, h * V_DIM:(h + 1) * V_DIM]
        vaug_ref[h, :, V_DIM:2 * V_DIM] = jnp.ones((seq, V_DIM), vaug_ref.dtype)
    row = lax.broadcasted_iota(jnp.int32, (tq, tq), 0)
    col = lax.broadcasted_iota(jnp.int32, (tq, tq), 1)

    def scores(job):
        h, i = job
        lo = i * tq
        cols = slice(h * HEAD_PAD, (h + 1) * HEAD_PAD)
        q = q_ref[lo:lo + tq, cols]
        s_d = lax.dot_general(q, k_ref[lo:lo + tq, cols], NT_DIMS, preferred_element_type=F32)
        s_d = jnp.where(row >= col, s_d, -jnp.inf)
        s_p = lax.dot_general(q, k_ref[0:lo, cols], NT_DIMS, preferred_element_type=F32) if i > 0 else None
        return s_d, s_p

    jobs = [(h, i) for i in range(seq // tq) for h in range(MLA_HEADS)]
    pending = [scores(job) for job in jobs[:ATTN_LOOKAHEAD]]
    for n_done, (h, i) in enumerate(jobs):
        lo = i * tq
        s_d, s_p = pending.pop(0)
        if n_done + ATTN_LOOKAHEAD < len(jobs):
            pending.append(scores(jobs[n_done + ATTN_LOOKAHEAD]))
        m = jnp.max(s_d, axis=-1, keepdims=True)
        if i > 0:
            m = jnp.maximum(m, jnp.max(s_p, axis=-1, keepdims=True))
        acc = _dot(jnp.exp2(s_d - m).astype(BF16), vaug_ref[h, lo:lo + tq, :])
        if i > 0:
            acc += _dot(jnp.exp2(s_p - m).astype(BF16), vaug_ref[h, 0:lo, :])
        o_ref[lo:lo + tq, h * V_DIM:(h + 1) * V_DIM] = (
            acc[:, 0:V_DIM] / acc[:, V_DIM:2 * V_DIM]).astype(o_ref.dtype)


def _postmix_kernel(h1_ref, ohg_ref, omla_ref, p_ref, wo_ref, g2_ref, wg_ref, wu_ref, wd_ref,
                    gple_ref, wpg_ref, wpp_ref, gfin_ref, y_ref):
    tp = h1_ref.shape[0]
    parts = [slice(r, r + TOKEN_TILE) for r in range(0, tp, TOKEN_TILE)] if tp % TOKEN_TILE == 0 else [slice(0, tp)]

    def tail_pieces(rows, h3):
        ctx = {}

        def gate():
            ctx["gate"] = jax.nn.sigmoid(_dot(_rms(h3, gple_ref[...]).astype(BF16), wpg_ref[...]))

        def finish():
            h4 = h3 + ctx["gate"] * _dot(p_ref[rows, :].astype(BF16), wpp_ref[...])
            y_ref[rows, :] = _rms(h4, gfin_ref[...])

        return [gate, finish]

    pending = []
    for rows in parts:
        h2 = (h1_ref[rows, :] + _dot(ohg_ref[rows, :], wo_ref[0:HG_WIDTH, :])
              + _dot(omla_ref[rows, :], wo_ref[HG_WIDTH:HG_WIDTH + MLA_HEADS * V_DIM, :]))
        slots = [c for c in POSTMIX_TAIL_AFTER_FFN_CHUNKS if c < D_FF // FFN_CHUNK][:len(pending)]
        between = {c: [piece] for c, piece in zip(slots, pending)}
        h3 = h2 + 0.5 * _swiglu(_rms(h2, g2_ref[...]).astype(BF16), wg_ref, wu_ref, wd_ref, between)
        for piece in pending[len(slots):]:
            piece()
        pending = tail_pieces(rows, h3)
    for piece in pending:
        piece()


def _resident(shape):
    return pl.BlockSpec(shape, lambda *_: (0,) * len(shape), pipeline_mode=pl.Buffered(1))


def _rows(tile, width):
    return pl.BlockSpec((tile, width), lambda i: (i, 0))


def kernel(x, p, positions, ln_ffn1, w1_gate, w1_up, w1_down, ln_mix, w_in, hg_lb_logits, hg_out_norm,
           q_a_norm, w_q_up, kv_a_norm, w_kv_up, w_out, ln_ffn2, w2_gate, w2_up, w2_down, ln_ple,
           w_ple_gate, w_ple_proj, ln_final):
    bsz, seq, _ = x.shape
    assert p.shape[0] == 1 and hg_lb_logits.shape[0] == 2, "single-layer trunk"
    n = bsz * seq
    tm = min(TOKEN_TILE, seq)
    tc = min(HGRN_CHUNK, seq)
    tq = min(ATTN_BLOCK, seq)
    tp = min(POSTMIX_TILE, n)
    assert seq % tm == 0 and tm % tc == 0 and seq % tq == 0 and n % tp == 0

    x2 = x.reshape(n, D_MODEL)
    pos2 = positions.astype(F32).reshape(n // tm, 1, tm)
    half = ROPE_DIM // 2
    inv_freq = ROPE_THETA ** (-jnp.arange(half, dtype=F32) / half)
    invf = jnp.concatenate([inv_freq, inv_freq, jnp.zeros((LANES - ROPE_DIM,), F32)]).reshape(1, LANES)
    wg1, wu1, wd1 = w1_gate[0].astype(BF16), w1_up[0].astype(BF16), w1_down[0].astype(BF16)
    win = w_in[0].astype(BF16)
    wq = jnp.pad(w_q_up[0].astype(BF16).reshape(Q_LORA, MLA_HEADS, QK_DIM),
                 ((0, 0), (0, 0), (0, HEAD_PAD - QK_DIM))).reshape(Q_LORA, MLA_HEADS * HEAD_PAD)
    wkv = w_kv_up[0].astype(BF16).reshape(KV_LORA, MLA_HEADS, NOPE_DIM + V_DIM)
    wk = wkv[:, :, :NOPE_DIM].reshape(KV_LORA, MLA_HEADS * NOPE_DIM)
    wv = wkv[:, :, NOPE_DIM:].reshape(KV_LORA, MLA_HEADS * V_DIM)
    wo = w_out[0].astype(BF16)
    row = lambda a: a.reshape(1, -1)
    layer0 = lambda w: w.reshape(w.shape[1:])

    steps = n // tm
    tile = lambda width: pl.BlockSpec((tm, width), lambda i: (jnp.minimum(i, steps - 1), 0))
    prev_tile = pl.BlockSpec((tm, HG_WIDTH), lambda i: (jnp.maximum(i - 1, 0), 0))
    h1, qp, kp, vv, o_hg = pl.pallas_call(
        functools.partial(_premix_kernel, tiles_per_row=seq // tm, chunk=tc),
        name="premix",
        grid=(steps + 1,),
        in_specs=[
            tile(D_MODEL), pl.BlockSpec((None, 1, tm), lambda i: (jnp.minimum(i, steps - 1), 0, 0)),
            _resident((1, LANES)), _resident((1, D_MODEL)),
            _resident((D_MODEL, D_FF)), _resident((D_MODEL, D_FF)), _resident((D_FF, D_MODEL)),
            _resident((1, D_MODEL)), _resident((D_MODEL, IN_WIDTH)),
            _resident((1, Q_LORA)), _resident((Q_LORA, MLA_HEADS * HEAD_PAD)), _resident((1, KV_LORA)),
            _resident((KV_LORA, MLA_HEADS * NOPE_DIM)), _resident((KV_LORA, MLA_HEADS * V_DIM)),
            _resident((2, HG_WIDTH)), _resident((HG_HEADS, HG_DIM)),
        ],
        out_specs=[tile(D_MODEL), tile(MLA_HEADS * HEAD_PAD), tile(MLA_HEADS * HEAD_PAD),
                   tile(MLA_HEADS * V_DIM), prev_tile],
        out_shape=[
            jax.ShapeDtypeStruct((n, D_MODEL), F32),
            jax.ShapeDtypeStruct((n, MLA_HEADS * HEAD_PAD), BF16),
            jax.ShapeDtypeStruct((n, MLA_HEADS * HEAD_PAD), BF16),
            jax.ShapeDtypeStruct((n, MLA_HEADS * V_DIM), BF16),
            jax.ShapeDtypeStruct((n, HG_WIDTH), BF16),
        ],
        scratch_shapes=[pltpu.VMEM((tm, HG_WIDTH), BF16), pltpu.VMEM((tm, HG_WIDTH), F32),
                        pltpu.VMEM((tm, HG_WIDTH), BF16), pltpu.VMEM((tm, HG_WIDTH), BF16),
                        pltpu.VMEM((HG_HEADS, HG_DIM, HG_DIM), F32)],
        compiler_params=pltpu.CompilerParams(dimension_semantics=("arbitrary",), vmem_limit_bytes=VMEM_LIMIT),
    )(x2, pos2, invf, row(ln_ffn1[0]), wg1, wu1, wd1, row(ln_mix[0]), win,
      row(q_a_norm[0]), wq, row(kv_a_norm[0]), wk, wv, hg_lb_logits, hg_out_norm[0])

    seq_spec = lambda width: pl.BlockSpec((seq, MLA_HEADS * width), lambda b: (b, 0))
    assert D_MODEL % (bsz * BF16_ROWS) == 0 and D_FF % (bsz * BF16_ROWS) == 0
    slab = lambda rows, cols: pl.BlockSpec((rows // bsz, cols), lambda b: (b, 0))
    o_mla, wg2, wu2, wd2 = pl.pallas_call(
        functools.partial(_attn_kernel, tq=tq),
        name="mla_attn",
        grid=(bsz,),
        in_specs=[seq_spec(HEAD_PAD), seq_spec(HEAD_PAD), seq_spec(V_DIM),
                  slab(D_MODEL, D_FF), slab(D_MODEL, D_FF), slab(D_FF, D_MODEL)],
        out_specs=[seq_spec(V_DIM), slab(D_MODEL, D_FF), slab(D_MODEL, D_FF), slab(D_FF, D_MODEL)],
        out_shape=[jax.ShapeDtypeStruct((n, MLA_HEADS * V_DIM), BF16),
                   jax.ShapeDtypeStruct((D_MODEL, D_FF), BF16), jax.ShapeDtypeStruct((D_MODEL, D_FF), BF16),
                   jax.ShapeDtypeStruct((D_FF, D_MODEL), BF16)],
        scratch_shapes=[pltpu.VMEM((MLA_HEADS, seq, 2 * V_DIM), BF16)],
        compiler_params=pltpu.CompilerParams(dimension_semantics=("parallel",), vmem_limit_bytes=VMEM_LIMIT),
    )(qp, kp, vv, layer0(w2_gate), layer0(w2_up), layer0(w2_down))

    y = pl.pallas_call(
        _postmix_kernel,
        name="postmix",
        grid=(n // tp,),
        in_specs=[
            _rows(tp, D_MODEL), _rows(tp, HG_WIDTH), _rows(tp, MLA_HEADS * V_DIM), _rows(tp, PLE_DIM),
            _resident((HG_WIDTH + MLA_HEADS * V_DIM, D_MODEL)), _resident((1, D_MODEL)),
            _resident((D_MODEL, D_FF)), _resident((D_MODEL, D_FF)), _resident((D_FF, D_MODEL)),
            _resident((1, D_MODEL)), _resident((D_MODEL, D_MODEL)), _resident((PLE_DIM, D_MODEL)),
            _resident((1, D_MODEL)),
        ],
        out_specs=_rows(tp, D_MODEL),
        out_shape=jax.ShapeDtypeStruct((n, D_MODEL), F32),
        compiler_params=pltpu.CompilerParams(dimension_semantics=("parallel",), vmem_limit_bytes=VMEM_LIMIT),
    )(h1, o_hg, o_mla, p[0].reshape(n, PLE_DIM), wo, row(ln_ffn2[0]),
      wg2, wu2, wd2, row(ln_ple[0]), w_ple_gate[0].astype(BF16), w_ple_proj[0].astype(BF16), row(ln_final))

    return y.reshape(bsz, seq, D_MODEL)
```

```python
import functools

import jax
import jax.numpy as jnp
from jax import lax
from jax.experimental import pallas as pl
from jax.experimental.pallas import tpu as pltpu

F32 = jnp.float32
BF16 = jnp.bfloat16

D_MODEL = 1024
D_FF = 2816
PLE_DIM = 256
HG_HEADS = 4
HG_DIM = 128
HG_WIDTH = HG_HEADS * HG_DIM
MLA_HEADS = 4
Q_LORA = 256
KV_LORA = 128
NOPE_DIM = 128
ROPE_DIM = 64
V_DIM = 128
QK_DIM = NOPE_DIM + ROPE_DIM
ROPE_THETA = 10000.0
EPS = 1e-6
LOG2_E = 1.4426950408889634

LANES = 128
SUBLANES = 8
BF16_ROWS = 2 * SUBLANES
HEAD_PAD = 2 * LANES
HG_GROUPS = 4
IN_WIDTH = HG_GROUPS * HG_WIDTH + Q_LORA + KV_LORA + ROPE_DIM

FFN_CHUNK = 256
TOKEN_TILE = 512
POSTMIX_TILE = 1024
HGRN_CHUNK = 256
ATTN_BLOCK = 256
ATTN_LOOKAHEAD = 3
POSTMIX_TAIL_AFTER_FFN_CHUNKS = (1, 4)
HGRN_AFTER_FFN_CHUNKS = (1, 4, 7, 10)
VMEM_LIMIT = 56 * 1024 * 1024

NT_DIMS = (((1,), (1,)), ((), ()))
TN_DIMS = (((0,), (0,)), ((), ()))


def _dot(a, b):
    return jnp.dot(a, b, preferred_element_type=F32)


def _rms(x, g):
    return x * lax.rsqrt(jnp.mean(x * x, axis=-1, keepdims=True) + EPS) * g


def _swiglu(xn, wg_ref, wu_ref, wd_ref, between=None):
    acc = None
    for idx, c in enumerate(range(0, D_FF, FFN_CHUNK)):
        g = _dot(xn, wg_ref[:, c:c + FFN_CHUNK])
        u = _dot(xn, wu_ref[:, c:c + FFN_CHUNK])
        a = (g * jax.nn.sigmoid(g) * u).astype(BF16)
        d = _dot(a, wd_ref[c:c + FFN_CHUNK, :])
        acc = d if acc is None else acc + d
        for piece in (between or {}).get(idx, ()):
            piece()
    return acc


def _rope(x, cos_t, sin_lo, sin_hi):
    return (x * cos_t + pltpu.roll(x, LANES - ROPE_DIM // 2, 1) * sin_lo
            + pltpu.roll(x, ROPE_DIM // 2, 1) * sin_hi)


def _premix_kernel(x_ref, pos_ref, invf_ref, g1_ref, wg_ref, wu_ref, wd_ref, gmix_ref, win_ref,
                   qan_ref, wq_ref, kvan_ref, wk_ref, wv_ref, lbl_ref, gn_ref,
                   h1_ref, qp_ref, kp_ref, v_ref, ohg_ref,
                   hq_s, hf_s, hi_s, hg_s, st_ref, *, tiles_per_row, chunk):
    i = pl.program_id(0)
    tm = x_ref.shape[0]

    @pl.when(i == 0)
    def _():
        hq_s[...] = jnp.zeros_like(hq_s)
        hf_s[...] = jnp.zeros_like(hf_s)
        hi_s[...] = jnp.zeros_like(hi_s)
        hg_s[...] = jnp.zeros_like(hg_s)
        st_ref[...] = jnp.zeros_like(st_ref)

    @pl.when(lax.rem(i + tiles_per_row - 1, tiles_per_row) == 0)
    def _():
        st_ref[...] = jnp.zeros_like(st_ref)

    pieces = []
    for r0 in range(0, tm, chunk):
        pieces += _hgrn_pieces(lbl_ref, hq_s, hf_s, hi_s, hg_s, gn_ref, ohg_ref, st_ref, r0, chunk)
    early_chunks = [c for c in HGRN_AFTER_FFN_CHUNKS if c < D_FF // FFN_CHUNK][:len(pieces)]
    early = {}
    for c, piece in zip(early_chunks, pieces):
        early.setdefault(c, []).append(piece)
    late = iter(pieces[len(early_chunks):])

    t_row = lax.broadcasted_iota(jnp.int32, (tm, tm), 0)
    t_col = lax.broadcasted_iota(jnp.int32, (tm, tm), 1)
    pos_col = jnp.sum(jnp.where(t_row == t_col, pos_ref[...], 0.0), axis=1, keepdims=True)
    ang = pos_col * invf_ref[...]
    lane = lax.broadcasted_iota(jnp.int32, ang.shape, 1)
    cos_a, sin_a = jnp.cos(ang), jnp.sin(ang)
    half = ROPE_DIM // 2
    cos_t = jnp.where(lane < ROPE_DIM, cos_a, 0.0)
    sin_lo = jnp.where(lane < half, -sin_a, 0.0)
    sin_hi = jnp.where((lane >= half) & (lane < ROPE_DIM), sin_a, 0.0)

    x = x_ref[...]
    xn = _rms(x, g1_ref[...]).astype(BF16)
    h1 = x + 0.5 * _swiglu(xn, wg_ref, wu_ref, wd_ref, early)
    h1_ref[...] = h1

    hn = _rms(h1, gmix_ref[...]).astype(BF16)
    w = HG_WIDTH
    new = []
    for j in range(HG_GROUPS):
        new.append(_dot(hn, win_ref[:, j * w:(j + 1) * w]))
        piece = next(late, None)
        if piece is not None:
            piece()
    for piece in late:
        piece()
    c0 = HG_GROUPS * w
    cq = _dot(hn, win_ref[:, c0:c0 + Q_LORA])
    ckv_kr = _dot(hn, win_ref[:, c0 + Q_LORA:IN_WIDTH])
    ckv = ckv_kr[:, 0:KV_LORA]
    kr = jnp.concatenate([ckv_kr[:, KV_LORA:KV_LORA + ROPE_DIM],
                          jnp.zeros((tm, LANES - ROPE_DIM), F32)], axis=1)

    q = _dot(_rms(cq, qan_ref[...]).astype(BF16), wq_ref[...]) * (QK_DIM ** -0.5 * LOG2_E)
    ckvn = _rms(ckv, kvan_ref[...]).astype(BF16)
    k_nope = _dot(ckvn, wk_ref[...])
    v_ref[...] = _dot(ckvn, wv_ref[...]).astype(v_ref.dtype)
    k_rope = _rope(kr, cos_t, sin_lo, sin_hi).astype(kp_ref.dtype)
    for h in range(MLA_HEADS):
        a = h * HEAD_PAD
        qp_ref[:, a:a + NOPE_DIM] = q[:, a:a + NOPE_DIM].astype(qp_ref.dtype)
        qp_ref[:, a + NOPE_DIM:a + HEAD_PAD] = _rope(
            q[:, a + NOPE_DIM:a + HEAD_PAD], cos_t, sin_lo, sin_hi).astype(qp_ref.dtype)
        kp_ref[:, a:a + NOPE_DIM] = k_nope[:, h * NOPE_DIM:(h + 1) * NOPE_DIM].astype(kp_ref.dtype)
        kp_ref[:, a + NOPE_DIM:a + HEAD_PAD] = k_rope

    hq_s[...] = new[0].astype(hq_s.dtype)
    hf_s[...] = new[1]
    hi_s[...] = new[2].astype(hi_s.dtype)
    hg_s[...] = new[3].astype(hg_s.dtype)


def _neg_gap(b, m):
    t = b.shape[0]
    if m >= SUBLANES:
        pieces = []
        for s in range(0, t, 2 * m):
            mid = b[s + m - 1:s + m, :]
            pieces += [mid - b[s:s + m], b[s + m:s + 2 * m] - mid]
        return jnp.concatenate(pieces, axis=0)
    b3 = b.reshape(t // SUBLANES, SUBLANES, LANES)
    sub = lax.broadcasted_iota(jnp.int32, b3.shape, 1)

    def row(i):
        return jnp.broadcast_to(b3[:, i:i + 1, :], b3.shape)

    mid = row(m - 1)
    for s in range(2 * m, SUBLANES, 2 * m):
        mid = jnp.where(sub >= s, row(s + m - 1), mid)
    return -jnp.abs(b3 - mid).reshape(t, LANES)


def _hgrn_pieces(lbl_ref, hq_ref, hf_ref, hi_ref, hg_ref, gn_ref, o_ref, st_ref, r0, t):
    rows = slice(r0, r0 + t)
    half = t // 2
    ctx = {}

    def gates():
        lg = lbl_ref[...]
        e = jnp.exp(lg - jnp.max(lg, axis=0, keepdims=True))
        lb = e[0:1, :] / jnp.sum(e, axis=0, keepdims=True)
        f_raw = hf_ref[rows, :]
        sig = jax.nn.sigmoid(f_raw)
        f = lb + (1.0 - lb) * sig
        g = jnp.log(f)
        ctx["f"] = f
        ctx["kk"] = (1.0 - lb) * (1.0 - sig)

        row = lax.broadcasted_iota(jnp.int32, (t, t), 0)
        col = lax.broadcasted_iota(jnp.int32, (t, t), 1)
        tri = (row >= col).astype(BF16)
        g1 = g.astype(BF16)
        r1 = g - g1.astype(F32)
        g2 = r1.astype(BF16)
        g3 = (r1 - g2.astype(F32)).astype(BF16)
        ctx["b"] = (_dot(tri, g1) + _dot(tri, g2) + _dot(tri, g3)) * LOG2_E

    def levels():
        ri = lax.broadcasted_iota(jnp.int32, (half, half), 0)
        ci = lax.broadcasted_iota(jnp.int32, (half, half), 1)
        xm = jnp.where(ri > ci, ri ^ ci, 0)
        odd_row = (lax.broadcasted_iota(jnp.int32, (t, HG_DIM), 0) & 1) == 1

        sls = [slice(h * HG_DIM, (h + 1) * HG_DIM) for h in range(HG_HEADS)]
        ctx["q"] = [hq_ref[rows, sl] for sl in sls]
        ctx["v"] = [hi_ref[rows, sl] for sl in sls]
        ctx["kb"] = [ctx["kk"][:, sl].astype(BF16) for sl in sls]
        ctx["bh"] = [ctx["b"][:, sl] for sl in sls]

        ps = [[None, None] for _ in sls]
        for i in range(half.bit_length() - 1):
            m = 1 << i
            mask = (xm >= m) & (xm < 2 * m)
            for h, sl in enumerate(sls):
                if m == 1:
                    decay = jnp.where(odd_row, ctx["f"][:, sl], 1.0).astype(BF16)
                else:
                    decay = jnp.exp2(_neg_gap(ctx["bh"][h], m)).astype(BF16)
                q_l, k_l = ctx["q"][h] * decay, ctx["kb"][h] * decay
                for qi, q0 in enumerate((0, half)):
                    s_l = lax.dot_general(q_l[q0:q0 + half], k_l[q0:q0 + half], NT_DIMS,
                                          preferred_element_type=F32)
                    ps[h][qi] = jnp.where(mask, s_l, 0.0 if ps[h][qi] is None else ps[h][qi])
        ctx["diag"] = [[p.astype(BF16) for p in pair] for pair in ps]

    def head(h):
        sl = slice(h * HG_DIM, (h + 1) * HG_DIM)
        q, v, kb, b, diag = ctx["q"][h], ctx["v"][h], ctx["kb"][h], ctx["bh"][h], ctx["diag"][h]
        mid = b[half - 1:half, :]
        cross = lax.dot_general(q[half:] * jnp.exp2(b[half:] - mid).astype(BF16),
                                kb[:half] * jnp.exp2(mid - b[:half]).astype(BF16), NT_DIMS,
                                preferred_element_type=F32).astype(BF16)
        o = jnp.concatenate([_dot(diag[0], v[:half]),
                             _dot(jnp.concatenate([cross, diag[1]], axis=1), v)], axis=0)
        o = o + jnp.sum((q * kb).astype(F32), axis=-1, keepdims=True) * v.astype(F32)

        st = st_ref[h]
        b_last = b[t - 1:t, :]
        o = o + lax.dot_general(q * jnp.exp2(b).astype(BF16), st.astype(BF16), NT_DIMS,
                                preferred_element_type=F32)
        k_end = kb * jnp.exp2(b_last - b).astype(BF16)
        st_ref[h] = st * jnp.exp2(b_last) + lax.dot_general(v, k_end, TN_DIMS, preferred_element_type=F32)

        gate = hg_ref[rows, sl].astype(F32)
        o_ref[rows, sl] = (_rms(o, gn_ref[h:h + 1, :]) * (gate * jax.nn.sigmoid(gate))).astype(o_ref.dtype)

    def heads():
        levels()
        for h in range(HG_HEADS):
            head(h)

    return [gates, heads]


def _attn_kernel(q_ref, k_ref, v_ref, wg_ref, wu_ref, wd_ref, o_ref, wg_bf_ref, wu_bf_ref, wd_bf_ref,
                 vaug_ref, *, tq):
    wg_bf_ref[...] = wg_ref[...].astype(wg_bf_ref.dtype)
    wu_bf_ref[...] = wu_ref[...].astype(wu_bf_ref.dtype)
    wd_bf_ref[...] = wd_ref[...].astype(wd_bf_ref.dtype)

    seq = q_ref.shape[0]
    for h in range(MLA_HEADS):
        vaug_ref[h, :, 0:V_DIM] = v_ref[:, h * V_DIM:(h + 1) * V_DIM]
        vaug_ref[h, :, V_DIM:2 * V_DIM] = jnp.ones((seq, V_DIM), vaug_ref.dtype)
    row = lax.broadcasted_iota(jnp.int32, (tq, tq), 0)
    col = lax.broadcasted_iota(jnp.int32, (tq, tq), 1)

    def scores(job):
        h, i = job
        lo = i * tq
        cols = slice(h * HEAD_PAD, (h + 1) * HEAD_PAD)
        q = q_ref[lo:lo + tq, cols]
        s_d = lax.dot_general(q, k_ref[lo:lo + tq, cols], NT_DIMS, preferred_element_type=F32)
        s_d = jnp.where(row >= col, s_d, -jnp.inf)
        s_p = lax.dot_general(q, k_ref[0:lo, cols], NT_DIMS, preferred_element_type=F32) if i > 0 else None
        return s_d, s_p

    jobs = [(h, i) for i in range(seq // tq) for h in range(MLA_HEADS)]
    pending = [scores(job) for job in jobs[:ATTN_LOOKAHEAD]]
    for n_done, (h, i) in enumerate(jobs):
        lo = i * tq
        s_d, s_p = pending.pop(0)
        if n_done + ATTN_LOOKAHEAD < len(jobs):
            pending.append(scores(jobs[n_done + ATTN_LOOKAHEAD]))
        m = jnp.max(s_d, axis=-1, keepdims=True)
        if i > 0:
            m = jnp.maximum(m, jnp.max(s_p, axis=-1, keepdims=True))
        acc = _dot(jnp.exp2(s_d - m).astype(BF16), vaug_ref[h, lo:lo + tq, :])
        if i > 0:
            acc += _dot(jnp.exp2(s_p - m).astype(BF16), vaug_ref[h, 0:lo, :])
        o_ref[lo:lo + tq, h * V_DIM:(h + 1) * V_DIM] = (
            acc[:, 0:V_DIM] / acc[:, V_DIM:2 * V_DIM]).astype(o_ref.dtype)


def _postmix_kernel(h1_ref, ohg_ref, omla_ref, p_ref, wo_ref, g2_ref, wg_ref, wu_ref, wd_ref,
                    gple_ref, wpg_ref, wpp_ref, gfin_ref, y_ref):
    tp = h1_ref.shape[0]
    parts = [slice(r, r + TOKEN_TILE) for r in range(0, tp, TOKEN_TILE)] if tp % TOKEN_TILE == 0 else [slice(0, tp)]

    def tail_pieces(rows, h3):
        ctx = {}

        def gate():
            ctx["gate"] = jax.nn.sigmoid(_dot(_rms(h3, gple_ref[...]).astype(BF16), wpg_ref[...]))

        def finish():
            h4 = h3 + ctx["gate"] * _dot(p_ref[rows, :].astype(BF16), wpp_ref[...])
            y_ref[rows, :] = _rms(h4, gfin_ref[...])

        return [gate, finish]

    pending = []
    for rows in parts:
        h2 = (h1_ref[rows, :] + _dot(ohg_ref[rows, :], wo_ref[0:HG_WIDTH, :])
              + _dot(omla_ref[rows, :], wo_ref[HG_WIDTH:HG_WIDTH + MLA_HEADS * V_DIM, :]))
        slots = [c for c in POSTMIX_TAIL_AFTER_FFN_CHUNKS if c < D_FF // FFN_CHUNK][:len(pending)]
        between = {c: [piece] for c, piece in zip(slots, pending)}
        h3 = h2 + 0.5 * _swiglu(_rms(h2, g2_ref[...]).astype(BF16), wg_ref, wu_ref, wd_ref, between)
        for piece in pending[len(slots):]:
            piece()
        pending = tail_pieces(rows, h3)
    for piece in pending:
        piece()


def _resident(shape):
    return pl.BlockSpec(shape, lambda *_: (0,) * len(shape), pipeline_mode=pl.Buffered(1))


def _rows(tile, width):
    return pl.BlockSpec((tile, width), lambda i: (i, 0))


def kernel(x, p, positions, ln_ffn1, w1_gate, w1_up, w1_down, ln_mix, w_in, hg_lb_logits, hg_out_norm,
           q_a_norm, w_q_up, kv_a_norm, w_kv_up, w_out, ln_ffn2, w2_gate, w2_up, w2_down, ln_ple,
           w_ple_gate, w_ple_proj, ln_final):
    bsz, seq, _ = x.shape
    assert p.shape[0] == 1 and hg_lb_logits.shape[0] == 2, "single-layer trunk"
    n = bsz * seq
    tm = min(TOKEN_TILE, seq)
    tc = min(HGRN_CHUNK, seq)
    tq = min(ATTN_BLOCK, seq)
    tp = min(POSTMIX_TILE, n)
    assert seq % tm == 0 and tm % tc == 0 and seq % tq == 0 and n % tp == 0

    x2 = x.reshape(n, D_MODEL)
    pos2 = positions.astype(F32).reshape(n // tm, 1, tm)
    half = ROPE_DIM // 2
    inv_freq = ROPE_THETA ** (-jnp.arange(half, dtype=F32) / half)
    invf = jnp.concatenate([inv_freq, inv_freq, jnp.zeros((LANES - ROPE_DIM,), F32)]).reshape(1, LANES)
    wg1, wu1, wd1 = w1_gate[0].astype(BF16), w1_up[0].astype(BF16), w1_down[0].astype(BF16)
    win = w_in[0].astype(BF16)
    wq = jnp.pad(w_q_up[0].astype(BF16).reshape(Q_LORA, MLA_HEADS, QK_DIM),
                 ((0, 0), (0, 0), (0, HEAD_PAD - QK_DIM))).reshape(Q_LORA, MLA_HEADS * HEAD_PAD)
    wkv = w_kv_up[0].astype(BF16).reshape(KV_LORA, MLA_HEADS, NOPE_DIM + V_DIM)
    wk = wkv[:, :, :NOPE_DIM].reshape(KV_LORA, MLA_HEADS * NOPE_DIM)
    wv = wkv[:, :, NOPE_DIM:].reshape(KV_LORA, MLA_HEADS * V_DIM)
    wo = w_out[0].astype(BF16)
    row = lambda a: a.reshape(1, -1)
    layer0 = lambda w: w.reshape(w.shape[1:])

    steps = n // tm
    tile = lambda width: pl.BlockSpec((tm, width), lambda i: (jnp.minimum(i, steps - 1), 0))
    prev_tile = pl.BlockSpec((tm, HG_WIDTH), lambda i: (jnp.maximum(i - 1, 0), 0))
    h1, qp, kp, vv, o_hg = pl.pallas_call(
        functools.partial(_premix_kernel, tiles_per_row=seq // tm, chunk=tc),
        name="premix",
        grid=(steps + 1,),
        in_specs=[
            tile(D_MODEL), pl.BlockSpec((None, 1, tm), lambda i: (jnp.minimum(i, steps - 1), 0, 0)),
            _resident((1, LANES)), _resident((1, D_MODEL)),
            _resident((D_MODEL, D_FF)), _resident((D_MODEL, D_FF)), _resident((D_FF, D_MODEL)),
            _resident((1, D_MODEL)), _resident((D_MODEL, IN_WIDTH)),
            _resident((1, Q_LORA)), _resident((Q_LORA, MLA_HEADS * HEAD_PAD)), _resident((1, KV_LORA)),
            _resident((KV_LORA, MLA_HEADS * NOPE_DIM)), _resident((KV_LORA, MLA_HEADS * V_DIM)),
            _resident((2, HG_WIDTH)), _resident((HG_HEADS, HG_DIM)),
        ],
        out_specs=[tile(D_MODEL), tile(MLA_HEADS * HEAD_PAD), tile(MLA_HEADS * HEAD_PAD),
                   tile(MLA_HEADS * V_DIM), prev_tile],
        out_shape=[
            jax.ShapeDtypeStruct((n, D_MODEL), F32),
            jax.ShapeDtypeStruct((n, MLA_HEADS * HEAD_PAD), BF16),
            jax.ShapeDtypeStruct((n, MLA_HEADS * HEAD_PAD), BF16),
            jax.ShapeDtypeStruct((n, MLA_HEADS * V_DIM), BF16),
            jax.ShapeDtypeStruct((n, HG_WIDTH), BF16),
        ],
        scratch_shapes=[pltpu.VMEM((tm, HG_WIDTH), BF16), pltpu.VMEM((tm, HG_WIDTH), F32),
                        pltpu.VMEM((tm, HG_WIDTH), BF16), pltpu.VMEM((tm, HG_WIDTH), BF16),
                        pltpu.VMEM((HG_HEADS, HG_DIM, HG_DIM), F32)],
        compiler_params=pltpu.CompilerParams(dimension_semantics=("arbitrary",), vmem_limit_bytes=VMEM_LIMIT),
    )(x2, pos2, invf, row(ln_ffn1[0]), wg1, wu1, wd1, row(ln_mix[0]), win,
      row(q_a_norm[0]), wq, row(kv_a_norm[0]), wk, wv, hg_lb_logits, hg_out_norm[0])

    seq_spec = lambda width: pl.BlockSpec((seq, MLA_HEADS * width), lambda b: (b, 0))
    assert D_MODEL % (bsz * BF16_ROWS) == 0 and D_FF % (bsz * BF16_ROWS) == 0
    slab = lambda rows, cols: pl.BlockSpec((rows // bsz, cols), lambda b: (b, 0))
    o_mla, wg2, wu2, wd2 = pl.pallas_call(
        functools.partial(_attn_kernel, tq=tq),
        name="mla_attn",
        grid=(bsz,),
        in_specs=[seq_spec(HEAD_PAD), seq_spec(HEAD_PAD), seq_spec(V_DIM),
                  slab(D_MODEL, D_FF), slab(D_MODEL, D_FF), slab(D_FF, D_MODEL)],
        out_specs=[seq_spec(V_DIM), slab(D_MODEL, D_FF), slab(D_MODEL, D_FF), slab(D_FF, D_MODEL)],
        out_shape=[jax.ShapeDtypeStruct((n, MLA_HEADS * V_DIM), BF16),
                   jax.ShapeDtypeStruct((D_MODEL, D_FF), BF16), jax.ShapeDtypeStruct((D_MODEL, D_FF), BF16),
                   jax.ShapeDtypeStruct((D_FF, D_MODEL), BF16)],
        scratch_shapes=[pltpu.VMEM((MLA_HEADS, seq, 2 * V_DIM), BF16)],
        compiler_params=pltpu.CompilerParams(dimension_semantics=("parallel",), vmem_limit_bytes=VMEM_LIMIT),
    )(qp, kp, vv, layer0(w2_gate), layer0(w2_up), layer0(w2_down))

    y = pl.pallas_call(
        _postmix_kernel,
        name="postmix",
        grid=(n // tp,),
        in_specs=[
            _rows(tp, D_MODEL), _rows(tp, HG_WIDTH), _rows(tp, MLA_HEADS * V_DIM), _rows(tp, PLE_DIM),
            _resident((HG_WIDTH + MLA_HEADS * V_DIM, D_MODEL)), _resident((1, D_MODEL)),
            _resident((D_MODEL, D_FF)), _resident((D_MODEL, D_FF)), _resident((D_FF, D_MODEL)),
            _resident((1, D_MODEL)), _resident((D_MODEL, D_MODEL)), _resident((PLE_DIM, D_MODEL)),
            _resident((1, D_MODEL)),
        ],
        out_specs=_rows(tp, D_MODEL),
        out_shape=jax.ShapeDtypeStruct((n, D_MODEL), F32),
        compiler_params=pltpu.CompilerParams(dimension_semantics=("parallel",), vmem_limit_bytes=VMEM_LIMIT),
    )(h1, o_hg, o_mla, p[0].reshape(n, PLE_DIM), wo, row(ln_ffn2[0]),
      wg2, wu2, wd2, row(ln_ple[0]), w_ple_gate[0].astype(BF16), w_ple_proj[0].astype(BF16), row(ln_final))

    return y.reshape(bsz, seq, D_MODEL)
```

```python
import functools

import jax
import jax.numpy as jnp
from jax import lax
from jax.experimental import pallas as pl
from jax.experimental.pallas import tpu as pltpu

F32 = jnp.float32
BF16 = jnp.bfloat16

D_MODEL = 1024
D_FF = 2816
PLE_DIM = 256
HG_HEADS = 4
HG_DIM = 128
HG_WIDTH = HG_HEADS * HG_DIM
MLA_HEADS = 4
Q_LORA = 256
KV_LORA = 128
NOPE_DIM = 128
ROPE_DIM = 64
V_DIM = 128
QK_DIM = NOPE_DIM + ROPE_DIM
ROPE_THETA = 10000.0
EPS = 1e-6
LOG2_E = 1.4426950408889634

LANES = 128
SUBLANES = 8
BF16_ROWS = 2 * SUBLANES
HEAD_PAD = 2 * LANES
HG_GROUPS = 4
IN_WIDTH = HG_GROUPS * HG_WIDTH + Q_LORA + KV_LORA + ROPE_DIM

FFN_CHUNK = 256
TOKEN_TILE = 512
POSTMIX_TILE = 1024
HGRN_CHUNK = 256
ATTN_BLOCK = 256
ATTN_LOOKAHEAD = 5
POSTMIX_TAIL_AFTER_FFN_CHUNKS = (3, 7)
HGRN_AFTER_FFN_CHUNKS = (0, 3, 6, 9)
VMEM_LIMIT = 56 * 1024 * 1024

NT_DIMS = (((1,), (1,)), ((), ()))
TN_DIMS = (((0,), (0,)), ((), ()))


def _dot(a, b):
    return jnp.dot(a, b, preferred_element_type=F32)


def _rms(x, g):
    return x * lax.rsqrt(jnp.mean(x * x, axis=-1, keepdims=True) + EPS) * g


def _swiglu(xn, wg_ref, wu_ref, wd_ref, between=None):
    acc = None
    for idx, c in enumerate(range(0, D_FF, FFN_CHUNK)):
        g = _dot(xn, wg_ref[:, c:c + FFN_CHUNK])
        u = _dot(xn, wu_ref[:, c:c + FFN_CHUNK])
        a = (g * jax.nn.sigmoid(g) * u).astype(BF16)
        d = _dot(a, wd_ref[c:c + FFN_CHUNK, :])
        acc = d if acc is None else acc + d
        for piece in (between or {}).get(idx, ()):
            piece()
    return acc


def _rope(x, cos_t, sin_lo, sin_hi):
    return (x * cos_t + pltpu.roll(x, LANES - ROPE_DIM // 2, 1) * sin_lo
            + pltpu.roll(x, ROPE_DIM // 2, 1) * sin_hi)


def _premix_kernel(x_ref, pos_ref, invf_ref, g1_ref, wg_ref, wu_ref, wd_ref, gmix_ref, win_ref,
                   qan_ref, wq_ref, kvan_ref, wk_ref, wv_ref, lbl_ref, gn_ref,
                   h1_ref, qp_ref, kp_ref, v_ref, ohg_ref,
                   hq_s, hf_s, hi_s, hg_s, st_ref, *, tiles_per_row, chunk):
    i = pl.program_id(0)
    tm = x_ref.shape[0]

    @pl.when(i == 0)
    def _():
        hq_s[...] = jnp.zeros_like(hq_s)
        hf_s[...] = jnp.zeros_like(hf_s)
        hi_s[...] = jnp.zeros_like(hi_s)
        hg_s[...] = jnp.zeros_like(hg_s)
        st_ref[...] = jnp.zeros_like(st_ref)

    @pl.when(lax.rem(i + tiles_per_row - 1, tiles_per_row) == 0)
    def _():
        st_ref[...] = jnp.zeros_like(st_ref)

    pieces = []
    for r0 in range(0, tm, chunk):
        pieces += _hgrn_pieces(lbl_ref, hq_s, hf_s, hi_s, hg_s, gn_ref, ohg_ref, st_ref, r0, chunk)
    early_chunks = [c for c in HGRN_AFTER_FFN_CHUNKS if c < D_FF // FFN_CHUNK][:len(pieces)]
    early = {}
    for c, piece in zip(early_chunks, pieces):
        early.setdefault(c, []).append(piece)
    late = iter(pieces[len(early_chunks):])

    t_row = lax.broadcasted_iota(jnp.int32, (tm, tm), 0)
    t_col = lax.broadcasted_iota(jnp.int32, (tm, tm), 1)
    pos_col = jnp.sum(jnp.where(t_row == t_col, pos_ref[...], 0.0), axis=1, keepdims=True)
    ang = pos_col * invf_ref[...]
    lane = lax.broadcasted_iota(jnp.int32, ang.shape, 1)
    cos_a, sin_a = jnp.cos(ang), jnp.sin(ang)
    half = ROPE_DIM // 2
    cos_t = jnp.where(lane < ROPE_DIM, cos_a, 0.0)
    sin_lo = jnp.where(lane < half, -sin_a, 0.0)
    sin_hi = jnp.where((lane >= half) & (lane < ROPE_DIM), sin_a, 0.0)

    x = x_ref[...]
    xn = _rms(x, g1_ref[...]).astype(BF16)
    h1 = x + 0.5 * _swiglu(xn, wg_ref, wu_ref, wd_ref, early)
    h1_ref[...] = h1

    hn = _rms(h1, gmix_ref[...]).astype(BF16)
    w = HG_WIDTH
    new = []
    for j in range(HG_GROUPS):
        new.append(_dot(hn, win_ref[:, j * w:(j + 1) * w]))
        piece = next(late, None)
        if piece is not None:
            piece()
    for piece in late:
        piece()
    c0 = HG_GROUPS * w
    cq = _dot(hn, win_ref[:, c0:c0 + Q_LORA])
    ckv_kr = _dot(hn, win_ref[:, c0 + Q_LORA:IN_WIDTH])
    ckv = ckv_kr[:, 0:KV_LORA]
    kr = jnp.concatenate([ckv_kr[:, KV_LORA:KV_LORA + ROPE_DIM],
                          jnp.zeros((tm, LANES - ROPE_DIM), F32)], axis=1)

    q = _dot(_rms(cq, qan_ref[...]).astype(BF16), wq_ref[...]) * (QK_DIM ** -0.5 * LOG2_E)
    ckvn = _rms(ckv, kvan_ref[...]).astype(BF16)
    k_nope = _dot(ckvn, wk_ref[...])
    v_ref[...] = _dot(ckvn, wv_ref[...]).astype(v_ref.dtype)
    k_rope = _rope(kr, cos_t, sin_lo, sin_hi).astype(kp_ref.dtype)
    for h in range(MLA_HEADS):
        a = h * HEAD_PAD
        qp_ref[:, a:a + NOPE_DIM] = q[:, a:a + NOPE_DIM].astype(qp_ref.dtype)
        qp_ref[:, a + NOPE_DIM:a + HEAD_PAD] = _rope(
            q[:, a + NOPE_DIM:a + HEAD_PAD], cos_t, sin_lo, sin_hi).astype(qp_ref.dtype)
        kp_ref[:, a:a + NOPE_DIM] = k_nope[:, h * NOPE_DIM:(h + 1) * NOPE_DIM].astype(kp_ref.dtype)
        kp_ref[:, a + NOPE_DIM:a + HEAD_PAD] = k_rope

    hq_s[...] = new[0].astype(hq_s.dtype)
    hf_s[...] = new[1]
    hi_s[...] = new[2].astype(hi_s.dtype)
    hg_s[...] = new[3].astype(hg_s.dtype)


def _neg_gap(b, m):
    t = b.shape[0]
    if m >= SUBLANES:
        pieces = []
        for s in range(0, t, 2 * m):
            mid = b[s + m - 1:s + m, :]
            pieces += [mid - b[s:s + m], b[s + m:s + 2 * m] - mid]
        return jnp.concatenate(pieces, axis=0)
    b3 = b.reshape(t // SUBLANES, SUBLANES, LANES)
    sub = lax.broadcasted_iota(jnp.int32, b3.shape, 1)

    def row(i):
        return jnp.broadcast_to(b3[:, i:i + 1, :], b3.shape)

    mid = row(m - 1)
    for s in range(2 * m, SUBLANES, 2 * m):
        mid = jnp.where(sub >= s, row(s + m - 1), mid)
    return -jnp.abs(b3 - mid).reshape(t, LANES)


def _hgrn_pieces(lbl_ref, hq_ref, hf_ref, hi_ref, hg_ref, gn_ref, o_ref, st_ref, r0, t):
    rows = slice(r0, r0 + t)
    half = t // 2
    ctx = {}

    def gates():
        lg = lbl_ref[...]
        e = jnp.exp(lg - jnp.max(lg, axis=0, keepdims=True))
        lb = e[0:1, :] / jnp.sum(e, axis=0, keepdims=True)
        f_raw = hf_ref[rows, :]
        sig = jax.nn.sigmoid(f_raw)
        f = lb + (1.0 - lb) * sig
        g = jnp.log(f)
        ctx["f"] = f
        ctx["kk"] = (1.0 - lb) * (1.0 - sig)

        row = lax.broadcasted_iota(jnp.int32, (t, t), 0)
        col = lax.broadcasted_iota(jnp.int32, (t, t), 1)
        tri = (row >= col).astype(BF16)
        g1 = g.astype(BF16)
        r1 = g - g1.astype(F32)
        g2 = r1.astype(BF16)
        g3 = (r1 - g2.astype(F32)).astype(BF16)
        ctx["b"] = (_dot(tri, g1) + _dot(tri, g2) + _dot(tri, g3)) * LOG2_E

    def levels():
        ri = lax.broadcasted_iota(jnp.int32, (half, half), 0)
        ci = lax.broadcasted_iota(jnp.int32, (half, half), 1)
        xm = jnp.where(ri > ci, ri ^ ci, 0)
        odd_row = (lax.broadcasted_iota(jnp.int32, (t, HG_DIM), 0) & 1) == 1

        sls = [slice(h * HG_DIM, (h + 1) * HG_DIM) for h in range(HG_HEADS)]
        ctx["q"] = [hq_ref[rows, sl] for sl in sls]
        ctx["v"] = [hi_ref[rows, sl] for sl in sls]
        ctx["kb"] = [ctx["kk"][:, sl].astype(BF16) for sl in sls]
        ctx["bh"] = [ctx["b"][:, sl] for sl in sls]

        ps = [[None, None] for _ in sls]
        for i in range(half.bit_length() - 1):
            m = 1 << i
            mask = (xm >= m) & (xm < 2 * m)
            for h, sl in enumerate(sls):
                if m == 1:
                    decay = jnp.where(odd_row, ctx["f"][:, sl], 1.0).astype(BF16)
                else:
                    decay = jnp.exp2(_neg_gap(ctx["bh"][h], m)).astype(BF16)
                q_l, k_l = ctx["q"][h] * decay, ctx["kb"][h] * decay
                for qi, q0 in enumerate((0, half)):
                    s_l = lax.dot_general(q_l[q0:q0 + half], k_l[q0:q0 + half], NT_DIMS,
                                          preferred_element_type=F32)
                    ps[h][qi] = jnp.where(mask, s_l, 0.0 if ps[h][qi] is None else ps[h][qi])
        ctx["diag"] = [[p.astype(BF16) for p in pair] for pair in ps]

    def head(h):
        sl = slice(h * HG_DIM, (h + 1) * HG_DIM)
        q, v, kb, b, diag = ctx["q"][h], ctx["v"][h], ctx["kb"][h], ctx["bh"][h], ctx["diag"][h]
        mid = b[half - 1:half, :]
        cross = lax.dot_general(q[half:] * jnp.exp2(b[half:] - mid).astype(BF16),
                                kb[:half] * jnp.exp2(mid - b[:half]).astype(BF16), NT_DIMS,
                                preferred_element_type=F32).astype(BF16)
        o = jnp.concatenate([_dot(diag[0], v[:half]),
                             _dot(jnp.concatenate([cross, diag[1]], axis=1), v)], axis=0)
        o = o + jnp.sum((q * kb).astype(F32), axis=-1, keepdims=True) * v.astype(F32)

        st = st_ref[h]
        b_last = b[t - 1:t, :]
        o = o + lax.dot_general(q * jnp.exp2(b).astype(BF16), st.astype(BF16), NT_DIMS,
                                preferred_element_type=F32)
        k_end = kb * jnp.exp2(b_last - b).astype(BF16)
        st_ref[h] = st * jnp.exp2(b_last) + lax.dot_general(v, k_end, TN_DIMS, preferred_element_type=F32)

        gate = hg_ref[rows, sl].astype(F32)
        o_ref[rows, sl] = (_rms(o, gn_ref[h:h + 1, :]) * (gate * jax.nn.sigmoid(gate))).astype(o_ref.dtype)

    def heads():
        levels()
        for h in range(HG_HEADS):
            head(h)

    return [gates, heads]


def _attn_kernel(q_ref, k_ref, v_ref, wg_ref, wu_ref, wd_ref, o_ref, wg_bf_ref, wu_bf_ref, wd_bf_ref,
                 vaug_ref, *, tq):
    wg_bf_ref[...] = wg_ref[...].astype(wg_bf_ref.dtype)
    wu_bf_ref[...] = wu_ref[...].astype(wu_bf_ref.dtype)
    wd_bf_ref[...] = wd_ref[...].astype(wd_bf_ref.dtype)

    seq = q_ref.shape[0]
    for h in range(MLA_HEADS):
        vaug_ref[h, :, 0:V_DIM] = v_ref[:, h * V_DIM:(h + 1) * V_DIM]
        vaug_ref[h, :, V_DIM:2 * V_DIM] = jnp.ones((seq, V_DIM), vaug_ref.dtype)
    row = lax.broadcasted_iota(jnp.int32, (tq, tq), 0)
    col = lax.broadcasted_iota(jnp.int32, (tq, tq), 1)

    def scores(job):
        h, i = job
        lo = i * tq
        cols = slice(h * HEAD_PAD, (h + 1) * HEAD_PAD)
        q = q_ref[lo:lo + tq, cols]
        s_d = lax.dot_general(q, k_ref[lo:lo + tq, cols], NT_DIMS, preferred_element_type=F32)
        s_d = jnp.where(row >= col, s_d, -jnp.inf)
        s_p = lax.dot_general(q, k_ref[0:lo, cols], NT_DIMS, preferred_element_type=F32) if i > 0 else None
        return s_d, s_p

    jobs = [(h, i) for i in range(seq // tq) for h in range(MLA_HEADS)]
    pending = [scores(job) for job in jobs[:ATTN_LOOKAHEAD]]
    for n_done, (h, i) in enumerate(jobs):
        lo = i * tq
        s_d, s_p = pending.pop(0)
        if n_done + ATTN_LOOKAHEAD < len(jobs):
            pending.append(scores(jobs[n_done + ATTN_LOOKAHEAD]))
        m = jnp.max(s_d, axis=-1, keepdims=True)
        if i > 0:
            m = jnp.maximum(m, jnp.max(s_p, axis=-1, keepdims=True))
        acc = _dot(jnp.exp2(s_d - m).astype(BF16), vaug_ref[h, lo:lo + tq, :])
        if i > 0:
            acc += _dot(jnp.exp2(s_p - m).astype(BF16), vaug_ref[h, 0:lo, :])
        o_ref[lo:lo + tq, h * V_DIM:(h + 1) * V_DIM] = (
            acc[:, 0:V_DIM] / acc[:, V_DIM:2 * V_DIM]).astype(o_ref.dtype)


def _postmix_kernel(h1_ref, ohg_ref, omla_ref, p_ref, wo_ref, g2_ref, wg_ref, wu_ref, wd_ref,
                    gple_ref, wpg_ref, wpp_ref, gfin_ref, y_ref):
    tp = h1_ref.shape[0]
    parts = [slice(r, r + TOKEN_TILE) for r in range(0, tp, TOKEN_TILE)] if tp % TOKEN_TILE == 0 else [slice(0, tp)]

    def tail_pieces(rows, h3):
        ctx = {}

        def gate():
            ctx["gate"] = jax.nn.sigmoid(_dot(_rms(h3, gple_ref[...]).astype(BF16), wpg_ref[...]))

        def finish():
            h4 = h3 + ctx["gate"] * _dot(p_ref[rows, :].astype(BF16), wpp_ref[...])
            y_ref[rows, :] = _rms(h4, gfin_ref[...])

        return [gate, finish]

    pending = []
    for rows in parts:
        h2 = (h1_ref[rows, :] + _dot(ohg_ref[rows, :], wo_ref[0:HG_WIDTH, :])
              + _dot(omla_ref[rows, :], wo_ref[HG_WIDTH:HG_WIDTH + MLA_HEADS * V_DIM, :]))
        slots = [c for c in POSTMIX_TAIL_AFTER_FFN_CHUNKS if c < D_FF // FFN_CHUNK][:len(pending)]
        between = {c: [piece] for c, piece in zip(slots, pending)}
        h3 = h2 + 0.5 * _swiglu(_rms(h2, g2_ref[...]).astype(BF16), wg_ref, wu_ref, wd_ref, between)
        for piece in pending[len(slots):]:
            piece()
        pending = tail_pieces(rows, h3)
    for piece in pending:
        piece()


def _resident(shape):
    return pl.BlockSpec(shape, lambda *_: (0,) * len(shape), pipeline_mode=pl.Buffered(1))


def _rows(tile, width):
    return pl.BlockSpec((tile, width), lambda i: (i, 0))


def kernel(x, p, positions, ln_ffn1, w1_gate, w1_up, w1_down, ln_mix, w_in, hg_lb_logits, hg_out_norm,
           q_a_norm, w_q_up, kv_a_norm, w_kv_up, w_out, ln_ffn2, w2_gate, w2_up, w2_down, ln_ple,
           w_ple_gate, w_ple_proj, ln_final):
    bsz, seq, _ = x.shape
    assert p.shape[0] == 1 and hg_lb_logits.shape[0] == 2, "single-layer trunk"
    n = bsz * seq
    tm = min(TOKEN_TILE, seq)
    tc = min(HGRN_CHUNK, seq)
    tq = min(ATTN_BLOCK, seq)
    tp = min(POSTMIX_TILE, n)
    assert seq % tm == 0 and tm % tc == 0 and seq % tq == 0 and n % tp == 0

    x2 = x.reshape(n, D_MODEL)
    pos2 = positions.astype(F32).reshape(n // tm, 1, tm)
    half = ROPE_DIM // 2
    inv_freq = ROPE_THETA ** (-jnp.arange(half, dtype=F32) / half)
    invf = jnp.concatenate([inv_freq, inv_freq, jnp.zeros((LANES - ROPE_DIM,), F32)]).reshape(1, LANES)
    wg1, wu1, wd1 = w1_gate[0].astype(BF16), w1_up[0].astype(BF16), w1_down[0].astype(BF16)
    win = w_in[0].astype(BF16)
    wq = jnp.pad(w_q_up[0].astype(BF16).reshape(Q_LORA, MLA_HEADS, QK_DIM),
                 ((0, 0), (0, 0), (0, HEAD_PAD - QK_DIM))).reshape(Q_LORA, MLA_HEADS * HEAD_PAD)
    wkv = w_kv_up[0].astype(BF16).reshape(KV_LORA, MLA_HEADS, NOPE_DIM + V_DIM)
    wk = wkv[:, :, :NOPE_DIM].reshape(KV_LORA, MLA_HEADS * NOPE_DIM)
    wv = wkv[:, :, NOPE_DIM:].reshape(KV_LORA, MLA_HEADS * V_DIM)
    wo = w_out[0].astype(BF16)
    row = lambda a: a.reshape(1, -1)
    layer0 = lambda w: w.reshape(w.shape[1:])

    steps = n // tm
    tile = lambda width: pl.BlockSpec((tm, width), lambda i: (jnp.minimum(i, steps - 1), 0))
    prev_tile = pl.BlockSpec((tm, HG_WIDTH), lambda i: (jnp.maximum(i - 1, 0), 0))
    h1, qp, kp, vv, o_hg = pl.pallas_call(
        functools.partial(_premix_kernel, tiles_per_row=seq // tm, chunk=tc),
        name="premix",
        grid=(steps + 1,),
        in_specs=[
            tile(D_MODEL), pl.BlockSpec((None, 1, tm), lambda i: (jnp.minimum(i, steps - 1), 0, 0)),
            _resident((1, LANES)), _resident((1, D_MODEL)),
            _resident((D_MODEL, D_FF)), _resident((D_MODEL, D_FF)), _resident((D_FF, D_MODEL)),
            _resident((1, D_MODEL)), _resident((D_MODEL, IN_WIDTH)),
            _resident((1, Q_LORA)), _resident((Q_LORA, MLA_HEADS * HEAD_PAD)), _resident((1, KV_LORA)),
            _resident((KV_LORA, MLA_HEADS * NOPE_DIM)), _resident((KV_LORA, MLA_HEADS * V_DIM)),
            _resident((2, HG_WIDTH)), _resident((HG_HEADS, HG_DIM)),
        ],
        out_specs=[tile(D_MODEL), tile(MLA_HEADS * HEAD_PAD), tile(MLA_HEADS * HEAD_PAD),
                   tile(MLA_HEADS * V_DIM), prev_tile],
        out_shape=[
            jax.ShapeDtypeStruct((n, D_MODEL), F32),
            jax.ShapeDtypeStruct((n, MLA_HEADS * HEAD_PAD), BF16),
            jax.ShapeDtypeStruct((n, MLA_HEADS * HEAD_PAD), BF16),
            jax.ShapeDtypeStruct((n, MLA_HEADS * V_DIM), BF16),
            jax.ShapeDtypeStruct((n, HG_WIDTH), BF16),
        ],
        scratch_shapes=[pltpu.VMEM((tm, HG_WIDTH), BF16), pltpu.VMEM((tm, HG_WIDTH), F32),
                        pltpu.VMEM((tm, HG_WIDTH), BF16), pltpu.VMEM((tm, HG_WIDTH), BF16),
                        pltpu.VMEM((HG_HEADS, HG_DIM, HG_DIM), F32)],
        compiler_params=pltpu.CompilerParams(dimension_semantics=("arbitrary",), vmem_limit_bytes=VMEM_LIMIT),
    )(x2, pos2, invf, row(ln_ffn1[0]), wg1, wu1, wd1, row(ln_mix[0]), win,
      row(q_a_norm[0]), wq, row(kv_a_norm[0]), wk, wv, hg_lb_logits, hg_out_norm[0])

    seq_spec = lambda width: pl.BlockSpec((seq, MLA_HEADS * width), lambda b: (b, 0))
    assert D_MODEL % (bsz * BF16_ROWS) == 0 and D_FF % (bsz * BF16_ROWS) == 0
    slab = lambda rows, cols: pl.BlockSpec((rows // bsz, cols), lambda b: (b, 0))
    o_mla, wg2, wu2, wd2 = pl.pallas_call(
        functools.partial(_attn_kernel, tq=tq),
        name="mla_attn",
        grid=(bsz,),
        in_specs=[seq_spec(HEAD_PAD), seq_spec(HEAD_PAD), seq_spec(V_DIM),
                  slab(D_MODEL, D_FF), slab(D_MODEL, D_FF), slab(D_FF, D_MODEL)],
        out_specs=[seq_spec(V_DIM), slab(D_MODEL, D_FF), slab(D_MODEL, D_FF), slab(D_FF, D_MODEL)],
        out_shape=[jax.ShapeDtypeStruct((n, MLA_HEADS * V_DIM), BF16),
                   jax.ShapeDtypeStruct((D_MODEL, D_FF), BF16), jax.ShapeDtypeStruct((D_MODEL, D_FF), BF16),
                   jax.ShapeDtypeStruct((D_FF, D_MODEL), BF16)],
        scratch_shapes=[pltpu.VMEM((MLA_HEADS, seq, 2 * V_DIM), BF16)],
        compiler_params=pltpu.CompilerParams(dimension_semantics=("parallel",), vmem_limit_bytes=VMEM_LIMIT),
    )(qp, kp, vv, layer0(w2_gate), layer0(w2_up), layer0(w2_down))

    y = pl.pallas_call(
        _postmix_kernel,
        name="postmix",
        grid=(n // tp,),
        in_specs=[
            _rows(tp, D_MODEL), _rows(tp, HG_WIDTH), _rows(tp, MLA_HEADS * V_DIM), _rows(tp, PLE_DIM),
            _resident((HG_WIDTH + MLA_HEADS * V_DIM, D_MODEL)), _resident((1, D_MODEL)),
            _resident((D_MODEL, D_FF)), _resident((D_MODEL, D_FF)), _resident((D_FF, D_MODEL)),
            _resident((1, D_MODEL)), _resident((D_MODEL, D_MODEL)), _resident((PLE_DIM, D_MODEL)),
            _resident((1, D_MODEL)),
        ],
        out_specs=_rows(tp, D_MODEL),
        out_shape=jax.ShapeDtypeStruct((n, D_MODEL), F32),
        compiler_params=pltpu.CompilerParams(dimension_semantics=("parallel",), vmem_limit_bytes=VMEM_LIMIT),
    )(h1, o_hg, o_mla, p[0].reshape(n, PLE_DIM), wo, row(ln_ffn2[0]),
      wg2, wu2, wd2, row(ln_ple[0]), w_ple_gate[0].astype(BF16), w_ple_proj[0].astype(BF16), row(ln_final))

    return y.reshape(bsz, seq, D_MODEL)
```

```python
import functools

import jax
import jax.numpy as jnp
from jax import lax
from jax.experimental import pallas as pl
from jax.experimental.pallas import tpu as pltpu

F32 = jnp.float32
BF16 = jnp.bfloat16

D_MODEL = 1024
D_FF = 2816
PLE_DIM = 256
HG_HEADS = 4
HG_DIM = 128
HG_WIDTH = HG_HEADS * HG_DIM
MLA_HEADS = 4
Q_LORA = 256
KV_LORA = 128
NOPE_DIM = 128
ROPE_DIM = 64
V_DIM = 128
QK_DIM = NOPE_DIM + ROPE_DIM
ROPE_THETA = 10000.0
EPS = 1e-6
LOG2_E = 1.4426950408889634

LANES = 128
SUBLANES = 8
BF16_ROWS = 2 * SUBLANES
HEAD_PAD = 2 * LANES
HG_GROUPS = 4
IN_WIDTH = HG_GROUPS * HG_WIDTH + Q_LORA + KV_LORA + ROPE_DIM

FFN_CHUNK = 256
TOKEN_TILE = 512
POSTMIX_TILE = 1024
HGRN_CHUNK = 256
ATTN_BLOCK = 256
ATTN_LOOKAHEAD = 5
ROPE_AFTER_FFN_CHUNK = 8
POSTMIX_TAIL_AFTER_FFN_CHUNKS = (3, 7)
HGRN_AFTER_FFN_CHUNKS = (0, 3, 6, 9)
VMEM_LIMIT = 56 * 1024 * 1024

NT_DIMS = (((1,), (1,)), ((), ()))
TN_DIMS = (((0,), (0,)), ((), ()))


def _dot(a, b):
    return jnp.dot(a, b, preferred_element_type=F32)


def _rms(x, g):
    return x * lax.rsqrt(jnp.mean(x * x, axis=-1, keepdims=True) + EPS) * g


def _swiglu(xn, wg_ref, wu_ref, wd_ref, between=None):
    acc = None
    for idx, c in enumerate(range(0, D_FF, FFN_CHUNK)):
        g = _dot(xn, wg_ref[:, c:c + FFN_CHUNK])
        u = _dot(xn, wu_ref[:, c:c + FFN_CHUNK])
        a = (g * jax.nn.sigmoid(g) * u).astype(BF16)
        d = _dot(a, wd_ref[c:c + FFN_CHUNK, :])
        acc = d if acc is None else acc + d
        for piece in (between or {}).get(idx, ()):
            piece()
    return acc


def _rope(x, cos_t, sin_lo, sin_hi):
    return (x * cos_t + pltpu.roll(x, LANES - ROPE_DIM // 2, 1) * sin_lo
            + pltpu.roll(x, ROPE_DIM // 2, 1) * sin_hi)


def _premix_kernel(x_ref, pos_ref, invf_ref, g1_ref, wg_ref, wu_ref, wd_ref, gmix_ref, win_ref,
                   qan_ref, wq_ref, kvan_ref, wk_ref, wv_ref, lbl_ref, gn_ref,
                   h1_ref, qp_ref, kp_ref, v_ref, ohg_ref,
                   hq_s, hf_s, hi_s, hg_s, st_ref, *, tiles_per_row, chunk):
    i = pl.program_id(0)
    tm = x_ref.shape[0]

    @pl.when(i == 0)
    def _():
        hq_s[...] = jnp.zeros_like(hq_s)
        hf_s[...] = jnp.zeros_like(hf_s)
        hi_s[...] = jnp.zeros_like(hi_s)
        hg_s[...] = jnp.zeros_like(hg_s)
        st_ref[...] = jnp.zeros_like(st_ref)

    @pl.when(lax.rem(i + tiles_per_row - 1, tiles_per_row) == 0)
    def _():
        st_ref[...] = jnp.zeros_like(st_ref)

    pieces = []
    for r0 in range(0, tm, chunk):
        pieces += _hgrn_pieces(lbl_ref, hq_s, hf_s, hi_s, hg_s, gn_ref, ohg_ref, st_ref, r0, chunk)
    early_chunks = [c for c in HGRN_AFTER_FFN_CHUNKS if c < D_FF // FFN_CHUNK][:len(pieces)]
    early = {}
    for c, piece in zip(early_chunks, pieces):
        early.setdefault(c, []).append(piece)
    late = iter(pieces[len(early_chunks):])

    rope = {}

    def rope_tables():
        t_row = lax.broadcasted_iota(jnp.int32, (tm, tm), 0)
        t_col = lax.broadcasted_iota(jnp.int32, (tm, tm), 1)
        pos_col = jnp.sum(jnp.where(t_row == t_col, pos_ref[...], 0.0), axis=1, keepdims=True)
        ang = pos_col * invf_ref[...]
        lane = lax.broadcasted_iota(jnp.int32, ang.shape, 1)
        cos_a, sin_a = jnp.cos(ang), jnp.sin(ang)
        half = ROPE_DIM // 2
        rope["cos"] = jnp.where(lane < ROPE_DIM, cos_a, 0.0)
        rope["lo"] = jnp.where(lane < half, -sin_a, 0.0)
        rope["hi"] = jnp.where((lane >= half) & (lane < ROPE_DIM), sin_a, 0.0)

    early.setdefault(ROPE_AFTER_FFN_CHUNK, []).append(rope_tables)

    x = x_ref[...]
    xn = _rms(x, g1_ref[...]).astype(BF16)
    h1 = x + 0.5 * _swiglu(xn, wg_ref, wu_ref, wd_ref, early)
    h1_ref[...] = h1

    hn = _rms(h1, gmix_ref[...]).astype(BF16)
    w = HG_WIDTH
    new = []
    for j in range(HG_GROUPS):
        new.append(_dot(hn, win_ref[:, j * w:(j + 1) * w]))
        piece = next(late, None)
        if piece is not None:
            piece()
    for piece in late:
        piece()
    c0 = HG_GROUPS * w
    cq = _dot(hn, win_ref[:, c0:c0 + Q_LORA])
    ckv_kr = _dot(hn, win_ref[:, c0 + Q_LORA:IN_WIDTH])
    ckv = ckv_kr[:, 0:KV_LORA]
    kr = jnp.concatenate([ckv_kr[:, KV_LORA:KV_LORA + ROPE_DIM],
                          jnp.zeros((tm, LANES - ROPE_DIM), F32)], axis=1)

    q = _dot(_rms(cq, qan_ref[...]).astype(BF16), wq_ref[...]) * (QK_DIM ** -0.5 * LOG2_E)
    ckvn = _rms(ckv, kvan_ref[...]).astype(BF16)
    k_nope = _dot(ckvn, wk_ref[...])
    v_ref[...] = _dot(ckvn, wv_ref[...]).astype(v_ref.dtype)
    cos_t, sin_lo, sin_hi = rope["cos"], rope["lo"], rope["hi"]
    k_rope = _rope(kr, cos_t, sin_lo, sin_hi).astype(kp_ref.dtype)
    for h in range(MLA_HEADS):
        a = h * HEAD_PAD
        qp_ref[:, a:a + NOPE_DIM] = q[:, a:a + NOPE_DIM].astype(qp_ref.dtype)
        qp_ref[:, a + NOPE_DIM:a + HEAD_PAD] = _rope(
            q[:, a + NOPE_DIM:a + HEAD_PAD], cos_t, sin_lo, sin_hi).astype(qp_ref.dtype)
        kp_ref[:, a:a + NOPE_DIM] = k_nope[:, h * NOPE_DIM:(h + 1) * NOPE_DIM].astype(kp_ref.dtype)
        kp_ref[:, a + NOPE_DIM:a + HEAD_PAD] = k_rope

    hq_s[...] = new[0].astype(hq_s.dtype)
    hf_s[...] = new[1]
    hi_s[...] = new[2].astype(hi_s.dtype)
    hg_s[...] = new[3].astype(hg_s.dtype)


def _neg_gap(b, m):
    t = b.shape[0]
    if m >= SUBLANES:
        pieces = []
        for s in range(0, t, 2 * m):
            mid = b[s + m - 1:s + m, :]
            pieces += [mid - b[s:s + m], b[s + m:s + 2 * m] - mid]
        return jnp.concatenate(pieces, axis=0)
    b3 = b.reshape(t // SUBLANES, SUBLANES, LANES)
    sub = lax.broadcasted_iota(jnp.int32, b3.shape, 1)

    def row(i):
        return jnp.broadcast_to(b3[:, i:i + 1, :], b3.shape)

    mid = row(m - 1)
    for s in range(2 * m, SUBLANES, 2 * m):
        mid = jnp.where(sub >= s, row(s + m - 1), mid)
    return -jnp.abs(b3 - mid).reshape(t, LANES)


def _hgrn_pieces(lbl_ref, hq_ref, hf_ref, hi_ref, hg_ref, gn_ref, o_ref, st_ref, r0, t):
    rows = slice(r0, r0 + t)
    half = t // 2
    ctx = {}

    def gates():
        lg = lbl_ref[...]
        e = jnp.exp(lg - jnp.max(lg, axis=0, keepdims=True))
        lb = e[0:1, :] / jnp.sum(e, axis=0, keepdims=True)
        f_raw = hf_ref[rows, :]
        sig = jax.nn.sigmoid(f_raw)
        f = lb + (1.0 - lb) * sig
        g = jnp.log(f)
        ctx["f"] = f
        ctx["kk"] = (1.0 - lb) * (1.0 - sig)

        row = lax.broadcasted_iota(jnp.int32, (t, t), 0)
        col = lax.broadcasted_iota(jnp.int32, (t, t), 1)
        tri = (row >= col).astype(BF16)
        g1 = g.astype(BF16)
        r1 = g - g1.astype(F32)
        g2 = r1.astype(BF16)
        g3 = (r1 - g2.astype(F32)).astype(BF16)
        ctx["b"] = (_dot(tri, g1) + _dot(tri, g2) + _dot(tri, g3)) * LOG2_E

    def levels():
        ri = lax.broadcasted_iota(jnp.int32, (half, half), 0)
        ci = lax.broadcasted_iota(jnp.int32, (half, half), 1)
        xm = jnp.where(ri > ci, ri ^ ci, 0)
        odd_row = (lax.broadcasted_iota(jnp.int32, (t, HG_DIM), 0) & 1) == 1

        sls = [slice(h * HG_DIM, (h + 1) * HG_DIM) for h in range(HG_HEADS)]
        ctx["q"] = [hq_ref[rows, sl] for sl in sls]
        ctx["v"] = [hi_ref[rows, sl] for sl in sls]
        ctx["kb"] = [ctx["kk"][:, sl].astype(BF16) for sl in sls]
        ctx["bh"] = [ctx["b"][:, sl] for sl in sls]

        ps = [[None, None] for _ in sls]
        for i in range(half.bit_length() - 1):
            m = 1 << i
            mask = (xm >= m) & (xm < 2 * m)
            for h, sl in enumerate(sls):
                if m == 1:
                    decay = jnp.where(odd_row, ctx["f"][:, sl], 1.0).astype(BF16)
                else:
                    decay = jnp.exp2(_neg_gap(ctx["bh"][h], m)).astype(BF16)
                q_l, k_l = ctx["q"][h] * decay, ctx["kb"][h] * decay
                for qi, q0 in enumerate((0, half)):
                    s_l = lax.dot_general(q_l[q0:q0 + half], k_l[q0:q0 + half], NT_DIMS,
                                          preferred_element_type=F32)
                    ps[h][qi] = jnp.where(mask, s_l, 0.0 if ps[h][qi] is None else ps[h][qi])
        ctx["diag"] = [[p.astype(BF16) for p in pair] for pair in ps]

    def head(h):
        sl = slice(h * HG_DIM, (h + 1) * HG_DIM)
        q, v, kb, b, diag = ctx["q"][h], ctx["v"][h], ctx["kb"][h], ctx["bh"][h], ctx["diag"][h]
        mid = b[half - 1:half, :]
        cross = lax.dot_general(q[half:] * jnp.exp2(b[half:] - mid).astype(BF16),
                                kb[:half] * jnp.exp2(mid - b[:half]).astype(BF16), NT_DIMS,
                                preferred_element_type=F32).astype(BF16)
        o = jnp.concatenate([_dot(diag[0], v[:half]),
                             _dot(jnp.concatenate([cross, diag[1]], axis=1), v)], axis=0)
        o = o + jnp.sum((q * kb).astype(F32), axis=-1, keepdims=True) * v.astype(F32)

        st = st_ref[h]
        b_last = b[t - 1:t, :]
        o = o + lax.dot_general(q * jnp.exp2(b).astype(BF16), st.astype(BF16), NT_DIMS,
                                preferred_element_type=F32)
        k_end = kb * jnp.exp2(b_last - b).astype(BF16)
        st_ref[h] = st * jnp.exp2(b_last) + lax.dot_general(v, k_end, TN_DIMS, preferred_element_type=F32)

        gate = hg_ref[rows, sl].astype(F32)
        o_ref[rows, sl] = (_rms(o, gn_ref[h:h + 1, :]) * (gate * jax.nn.sigmoid(gate))).astype(o_ref.dtype)

    def heads():
        levels()
        for h in range(HG_HEADS):
            head(h)

    return [gates, heads]


def _attn_kernel(q_ref, k_ref, v_ref, wg_ref, wu_ref, wd_ref, o_ref, wg_bf_ref, wu_bf_ref, wd_bf_ref,
                 vaug_ref, *, tq):
    wg_bf_ref[...] = wg_ref[...].astype(wg_bf_ref.dtype)
    wu_bf_ref[...] = wu_ref[...].astype(wu_bf_ref.dtype)
    wd_bf_ref[...] = wd_ref[...].astype(wd_bf_ref.dtype)

    seq = q_ref.shape[0]
    for h in range(MLA_HEADS):
        vaug_ref[h, :, 0:V_DIM] = v_ref[:, h * V_DIM:(h + 1) * V_DIM]
        vaug_ref[h, :, V_DIM:2 * V_DIM] = jnp.ones((seq, V_DIM), vaug_ref.dtype)
    row = lax.broadcasted_iota(jnp.int32, (tq, tq), 0)
    col = lax.broadcasted_iota(jnp.int32, (tq, tq), 1)

    def scores(job):
        h, i = job
        lo = i * tq
        cols = slice(h * HEAD_PAD, (h + 1) * HEAD_PAD)
        q = q_ref[lo:lo + tq, cols]
        s_d = lax.dot_general(q, k_ref[lo:lo + tq, cols], NT_DIMS, preferred_element_type=F32)
        s_d = jnp.where(row >= col, s_d, -jnp.inf)
        s_p = lax.dot_general(q, k_ref[0:lo, cols], NT_DIMS, preferred_element_type=F32) if i > 0 else None
        return s_d, s_p

    jobs = [(h, i) for i in range(seq // tq) for h in range(MLA_HEADS)]
    pending = [scores(job) for job in jobs[:ATTN_LOOKAHEAD]]
    for n_done, (h, i) in enumerate(jobs):
        lo = i * tq
        s_d, s_p = pending.pop(0)
        if n_done + ATTN_LOOKAHEAD < len(jobs):
            pending.append(scores(jobs[n_done + ATTN_LOOKAHEAD]))
        m = jnp.max(s_d, axis=-1, keepdims=True)
        if i > 0:
            m = jnp.maximum(m, jnp.max(s_p, axis=-1, keepdims=True))
        acc = _dot(jnp.exp2(s_d - m).astype(BF16), vaug_ref[h, lo:lo + tq, :])
        if i > 0:
            acc += _dot(jnp.exp2(s_p - m).astype(BF16), vaug_ref[h, 0:lo, :])
        o_ref[lo:lo + tq, h * V_DIM:(h + 1) * V_DIM] = (
            acc[:, 0:V_DIM] / acc[:, V_DIM:2 * V_DIM]).astype(o_ref.dtype)


def _postmix_kernel(h1_ref, ohg_ref, omla_ref, p_ref, wo_ref, g2_ref, wg_ref, wu_ref, wd_ref,
                    gple_ref, wpg_ref, wpp_ref, gfin_ref, y_ref):
    tp = h1_ref.shape[0]
    parts = [slice(r, r + TOKEN_TILE) for r in range(0, tp, TOKEN_TILE)] if tp % TOKEN_TILE == 0 else [slice(0, tp)]

    def tail_pieces(rows, h3):
        ctx = {}

        def gate():
            ctx["gate"] = jax.nn.sigmoid(_dot(_rms(h3, gple_ref[...]).astype(BF16), wpg_ref[...]))

        def finish():
            h4 = h3 + ctx["gate"] * _dot(p_ref[rows, :].astype(BF16), wpp_ref[...])
            y_ref[rows, :] = _rms(h4, gfin_ref[...])

        return [gate, finish]

    pending = []
    for rows in parts:
        h2 = (h1_ref[rows, :] + _dot(ohg_ref[rows, :], wo_ref[0:HG_WIDTH, :])
              + _dot(omla_ref[rows, :], wo_ref[HG_WIDTH:HG_WIDTH + MLA_HEADS * V_DIM, :]))
        slots = [c for c in POSTMIX_TAIL_AFTER_FFN_CHUNKS if c < D_FF // FFN_CHUNK][:len(pending)]
        between = {c: [piece] for c, piece in zip(slots, pending)}
        h3 = h2 + 0.5 * _swiglu(_rms(h2, g2_ref[...]).astype(BF16), wg_ref, wu_ref, wd_ref, between)
        for piece in pending[len(slots):]:
            piece()
        pending = tail_pieces(rows, h3)
    for piece in pending:
        piece()


def _resident(shape):
    return pl.BlockSpec(shape, lambda *_: (0,) * len(shape), pipeline_mode=pl.Buffered(1))


def _rows(tile, width):
    return pl.BlockSpec((tile, width), lambda i: (i, 0))


def kernel(x, p, positions, ln_ffn1, w1_gate, w1_up, w1_down, ln_mix, w_in, hg_lb_logits, hg_out_norm,
           q_a_norm, w_q_up, kv_a_norm, w_kv_up, w_out, ln_ffn2, w2_gate, w2_up, w2_down, ln_ple,
           w_ple_gate, w_ple_proj, ln_final):
    bsz, seq, _ = x.shape
    assert p.shape[0] == 1 and hg_lb_logits.shape[0] == 2, "single-layer trunk"
    n = bsz * seq
    tm = min(TOKEN_TILE, seq)
    tc = min(HGRN_CHUNK, seq)
    tq = min(ATTN_BLOCK, seq)
    tp = min(POSTMIX_TILE, n)
    assert seq % tm == 0 and tm % tc == 0 and seq % tq == 0 and n % tp == 0

    x2 = x.reshape(n, D_MODEL)
    pos2 = positions.astype(F32).reshape(n // tm, 1, tm)
    half = ROPE_DIM // 2
    inv_freq = ROPE_THETA ** (-jnp.arange(half, dtype=F32) / half)
    invf = jnp.concatenate([inv_freq, inv_freq, jnp.zeros((LANES - ROPE_DIM,), F32)]).reshape(1, LANES)
    wg1, wu1, wd1 = w1_gate[0].astype(BF16), w1_up[0].astype(BF16), w1_down[0].astype(BF16)
    win = w_in[0].astype(BF16)
    wq = jnp.pad(w_q_up[0].astype(BF16).reshape(Q_LORA, MLA_HEADS, QK_DIM),
                 ((0, 0), (0, 0), (0, HEAD_PAD - QK_DIM))).reshape(Q_LORA, MLA_HEADS * HEAD_PAD)
    wkv = w_kv_up[0].astype(BF16).reshape(KV_LORA, MLA_HEADS, NOPE_DIM + V_DIM)
    wk = wkv[:, :, :NOPE_DIM].reshape(KV_LORA, MLA_HEADS * NOPE_DIM)
    wv = wkv[:, :, NOPE_DIM:].reshape(KV_LORA, MLA_HEADS * V_DIM)
    wo = w_out[0].astype(BF16)
    row = lambda a: a.reshape(1, -1)
    layer0 = lambda w: w.reshape(w.shape[1:])

    steps = n // tm
    tile = lambda width: pl.BlockSpec((tm, width), lambda i: (jnp.minimum(i, steps - 1), 0))
    prev_tile = pl.BlockSpec((tm, HG_WIDTH), lambda i: (jnp.maximum(i - 1, 0), 0))
    h1, qp, kp, vv, o_hg = pl.pallas_call(
        functools.partial(_premix_kernel, tiles_per_row=seq // tm, chunk=tc),
        name="premix",
        grid=(steps + 1,),
        in_specs=[
            tile(D_MODEL), pl.BlockSpec((None, 1, tm), lambda i: (jnp.minimum(i, steps - 1), 0, 0)),
            _resident((1, LANES)), _resident((1, D_MODEL)),
            _resident((D_MODEL, D_FF)), _resident((D_MODEL, D_FF)), _resident((D_FF, D_MODEL)),
            _resident((1, D_MODEL)), _resident((D_MODEL, IN_WIDTH)),
            _resident((1, Q_LORA)), _resident((Q_LORA, MLA_HEADS * HEAD_PAD)), _resident((1, KV_LORA)),
            _resident((KV_LORA, MLA_HEADS * NOPE_DIM)), _resident((KV_LORA, MLA_HEADS * V_DIM)),
            _resident((2, HG_WIDTH)), _resident((HG_HEADS, HG_DIM)),
        ],
        out_specs=[tile(D_MODEL), tile(MLA_HEADS * HEAD_PAD), tile(MLA_HEADS * HEAD_PAD),
                   tile(MLA_HEADS * V_DIM), prev_tile],
        out_shape=[
            jax.ShapeDtypeStruct((n, D_MODEL), F32),
            jax.ShapeDtypeStruct((n, MLA_HEADS * HEAD_PAD), BF16),
            jax.ShapeDtypeStruct((n, MLA_HEADS * HEAD_PAD), BF16),
            jax.ShapeDtypeStruct((n, MLA_HEADS * V_DIM), BF16),
            jax.ShapeDtypeStruct((n, HG_WIDTH), BF16),
        ],
        scratch_shapes=[pltpu.VMEM((tm, HG_WIDTH), BF16), pltpu.VMEM((tm, HG_WIDTH), F32),
                        pltpu.VMEM((tm, HG_WIDTH), BF16), pltpu.VMEM((tm, HG_WIDTH), BF16),
                        pltpu.VMEM((HG_HEADS, HG_DIM, HG_DIM), F32)],
        compiler_params=pltpu.CompilerParams(dimension_semantics=("arbitrary",), vmem_limit_bytes=VMEM_LIMIT),
    )(x2, pos2, invf, row(ln_ffn1[0]), wg1, wu1, wd1, row(ln_mix[0]), win,
      row(q_a_norm[0]), wq, row(kv_a_norm[0]), wk, wv, hg_lb_logits, hg_out_norm[0])

    seq_spec = lambda width: pl.BlockSpec((seq, MLA_HEADS * width), lambda b: (b, 0))
    assert D_MODEL % (bsz * BF16_ROWS) == 0 and D_FF % (bsz * BF16_ROWS) == 0
    slab = lambda rows, cols: pl.BlockSpec((rows // bsz, cols), lambda b: (b, 0))
    o_mla, wg2, wu2, wd2 = pl.pallas_call(
        functools.partial(_attn_kernel, tq=tq),
        name="mla_attn",
        grid=(bsz,),
        in_specs=[seq_spec(HEAD_PAD), seq_spec(HEAD_PAD), seq_spec(V_DIM),
                  slab(D_MODEL, D_FF), slab(D_MODEL, D_FF), slab(D_FF, D_MODEL)],
        out_specs=[seq_spec(V_DIM), slab(D_MODEL, D_FF), slab(D_MODEL, D_FF), slab(D_FF, D_MODEL)],
        out_shape=[jax.ShapeDtypeStruct((n, MLA_HEADS * V_DIM), BF16),
                   jax.ShapeDtypeStruct((D_MODEL, D_FF), BF16), jax.ShapeDtypeStruct((D_MODEL, D_FF), BF16),
                   jax.ShapeDtypeStruct((D_FF, D_MODEL), BF16)],
        scratch_shapes=[pltpu.VMEM((MLA_HEADS, seq, 2 * V_DIM), BF16)],
        compiler_params=pltpu.CompilerParams(dimension_semantics=("parallel",), vmem_limit_bytes=VMEM_LIMIT),
    )(qp, kp, vv, layer0(w2_gate), layer0(w2_up), layer0(w2_down))

    y = pl.pallas_call(
        _postmix_kernel,
        name="postmix",
        grid=(n // tp,),
        in_specs=[
            _rows(tp, D_MODEL), _rows(tp, HG_WIDTH), _rows(tp, MLA_HEADS * V_DIM), _rows(tp, PLE_DIM),
            _resident((HG_WIDTH + MLA_HEADS * V_DIM, D_MODEL)), _resident((1, D_MODEL)),
            _resident((D_MODEL, D_FF)), _resident((D_MODEL, D_FF)), _resident((D_FF, D_MODEL)),
            _resident((1, D_MODEL)), _resident((D_MODEL, D_MODEL)), _resident((PLE_DIM, D_MODEL)),
            _resident((1, D_MODEL)),
        ],
        out_specs=_rows(tp, D_MODEL),
        out_shape=jax.ShapeDtypeStruct((n, D_MODEL), F32),
        compiler_params=pltpu.CompilerParams(dimension_semantics=("parallel",), vmem_limit_bytes=VMEM_LIMIT),
    )(h1, o_hg, o_mla, p[0].reshape(n, PLE_DIM), wo, row(ln_ffn2[0]),
      wg2, wu2, wd2, row(ln_ple[0]), w_ple_gate[0].astype(BF16), w_ple_proj[0].astype(BF16), row(ln_final))

    return y.reshape(bsz, seq, D_MODEL)
```

```python
import functools

import jax
import jax.numpy as jnp
from jax import lax
from jax.experimental import pallas as pl
from jax.experimental.pallas import tpu as pltpu

F32 = jnp.float32
BF16 = jnp.bfloat16

D_MODEL = 1024
D_FF = 2816
PLE_DIM = 256
HG_HEADS = 4
HG_DIM = 128
HG_WIDTH = HG_HEADS * HG_DIM
MLA_HEADS = 4
Q_LORA = 256
KV_LORA = 128
NOPE_DIM = 128
ROPE_DIM = 64
V_DIM = 128
QK_DIM = NOPE_DIM + ROPE_DIM
ROPE_THETA = 10000.0
EPS = 1e-6
LOG2_E = 1.4426950408889634

LANES = 128
SUBLANES = 8
BF16_ROWS = 2 * SUBLANES
HEAD_PAD = 2 * LANES
HG_GROUPS = 4
IN_WIDTH = HG_GROUPS * HG_WIDTH + Q_LORA + KV_LORA + ROPE_DIM

FFN_CHUNK = 256
TOKEN_TILE = 512
POSTMIX_TILE = 1024
HGRN_CHUNK = 256
ATTN_BLOCK = 256
ATTN_LOOKAHEAD = 3
POSTMIX_TAIL_AFTER_FFN_CHUNKS = (1, 4)
HGRN_AFTER_FFN_CHUNKS = (0, 2, 4, 6, 8, 9)
VMEM_LIMIT = 56 * 1024 * 1024

NT_DIMS = (((1,), (1,)), ((), ()))
TN_DIMS = (((0,), (0,)), ((), ()))


def _dot(a, b):
    return jnp.dot(a, b, preferred_element_type=F32)


def _rms(x, g):
    return x * lax.rsqrt(jnp.mean(x * x, axis=-1, keepdims=True) + EPS) * g


def _swiglu(xn, wg_ref, wu_ref, wd_ref, between=None):
    acc = None
    for idx, c in enumerate(range(0, D_FF, FFN_CHUNK)):
        g = _dot(xn, wg_ref[:, c:c + FFN_CHUNK])
        u = _dot(xn, wu_ref[:, c:c + FFN_CHUNK])
        a = (g * jax.nn.sigmoid(g) * u).astype(BF16)
        d = _dot(a, wd_ref[c:c + FFN_CHUNK, :])
        acc = d if acc is None else acc + d
        for piece in (between or {}).get(idx, ()):
            piece()
    return acc


def _rope(x, cos_t, sin_lo, sin_hi):
    return (x * cos_t + pltpu.roll(x, LANES - ROPE_DIM // 2, 1) * sin_lo
            + pltpu.roll(x, ROPE_DIM // 2, 1) * sin_hi)


def _premix_kernel(x_ref, pos_ref, invf_ref, g1_ref, wg_ref, wu_ref, wd_ref, gmix_ref, win_ref,
                   qan_ref, wq_ref, kvan_ref, wk_ref, wv_ref, lbl_ref, gn_ref,
                   h1_ref, qp_ref, kp_ref, v_ref, ohg_ref,
                   hq_s, hf_s, hi_s, hg_s, st_ref, *, tiles_per_row, chunk):
    i = pl.program_id(0)
    tm = x_ref.shape[0]

    @pl.when(i == 0)
    def _():
        hq_s[...] = jnp.zeros_like(hq_s)
        hf_s[...] = jnp.zeros_like(hf_s)
        hi_s[...] = jnp.zeros_like(hi_s)
        hg_s[...] = jnp.zeros_like(hg_s)
        st_ref[...] = jnp.zeros_like(st_ref)

    @pl.when(lax.rem(i + tiles_per_row - 1, tiles_per_row) == 0)
    def _():
        st_ref[...] = jnp.zeros_like(st_ref)

    pieces = []
    for r0 in range(0, tm, chunk):
        pieces += _hgrn_pieces(lbl_ref, hq_s, hf_s, hi_s, hg_s, gn_ref, ohg_ref, st_ref, r0, chunk)
    early_chunks = [c for c in HGRN_AFTER_FFN_CHUNKS if c < D_FF // FFN_CHUNK][:len(pieces)]
    early = {}
    for c, piece in zip(early_chunks, pieces):
        early.setdefault(c, []).append(piece)
    late = iter(pieces[len(early_chunks):])

    t_row = lax.broadcasted_iota(jnp.int32, (tm, tm), 0)
    t_col = lax.broadcasted_iota(jnp.int32, (tm, tm), 1)
    pos_col = jnp.sum(jnp.where(t_row == t_col, pos_ref[...], 0.0), axis=1, keepdims=True)
    ang = pos_col * invf_ref[...]
    lane = lax.broadcasted_iota(jnp.int32, ang.shape, 1)
    cos_a, sin_a = jnp.cos(ang), jnp.sin(ang)
    half = ROPE_DIM // 2
    cos_t = jnp.where(lane < ROPE_DIM, cos_a, 0.0)
    sin_lo = jnp.where(lane < half, -sin_a, 0.0)
    sin_hi = jnp.where((lane >= half) & (lane < ROPE_DIM), sin_a, 0.0)

    x = x_ref[...]
    xn = _rms(x, g1_ref[...]).astype(BF16)
    h1 = x + 0.5 * _swiglu(xn, wg_ref, wu_ref, wd_ref, early)
    h1_ref[...] = h1

    hn = _rms(h1, gmix_ref[...]).astype(BF16)
    w = HG_WIDTH
    new = []
    for j in range(HG_GROUPS):
        new.append(_dot(hn, win_ref[:, j * w:(j + 1) * w]))
        piece = next(late, None)
        if piece is not None:
            piece()
    for piece in late:
        piece()
    c0 = HG_GROUPS * w
    cq = _dot(hn, win_ref[:, c0:c0 + Q_LORA])
    ckv_kr = _dot(hn, win_ref[:, c0 + Q_LORA:IN_WIDTH])
    ckv = ckv_kr[:, 0:KV_LORA]
    kr = jnp.concatenate([ckv_kr[:, KV_LORA:KV_LORA + ROPE_DIM],
                          jnp.zeros((tm, LANES - ROPE_DIM), F32)], axis=1)

    q = _dot(_rms(cq, qan_ref[...]).astype(BF16), wq_ref[...]) * (QK_DIM ** -0.5 * LOG2_E)
    ckvn = _rms(ckv, kvan_ref[...]).astype(BF16)
    k_nope = _dot(ckvn, wk_ref[...])
    v_ref[...] = _dot(ckvn, wv_ref[...]).astype(v_ref.dtype)
    k_rope = _rope(kr, cos_t, sin_lo, sin_hi).astype(kp_ref.dtype)
    for h in range(MLA_HEADS):
        a = h * HEAD_PAD
        qp_ref[:, a:a + NOPE_DIM] = q[:, a:a + NOPE_DIM].astype(qp_ref.dtype)
        qp_ref[:, a + NOPE_DIM:a + HEAD_PAD] = _rope(
            q[:, a + NOPE_DIM:a + HEAD_PAD], cos_t, sin_lo, sin_hi).astype(qp_ref.dtype)
        kp_ref[:, a:a + NOPE_DIM] = k_nope[:, h * NOPE_DIM:(h + 1) * NOPE_DIM].astype(kp_ref.dtype)
        kp_ref[:, a + NOPE_DIM:a + HEAD_PAD] = k_rope

    hq_s[...] = new[0].astype(hq_s.dtype)
    hf_s[...] = new[1]
    hi_s[...] = new[2].astype(hi_s.dtype)
    hg_s[...] = new[3].astype(hg_s.dtype)


def _neg_gap(b, m):
    t = b.shape[0]
    if m >= SUBLANES:
        pieces = []
        for s in range(0, t, 2 * m):
            mid = b[s + m - 1:s + m, :]
            pieces += [mid - b[s:s + m], b[s + m:s + 2 * m] - mid]
        return jnp.concatenate(pieces, axis=0)
    b3 = b.reshape(t // SUBLANES, SUBLANES, LANES)
    sub = lax.broadcasted_iota(jnp.int32, b3.shape, 1)

    def row(i):
        return jnp.broadcast_to(b3[:, i:i + 1, :], b3.shape)

    mid = row(m - 1)
    for s in range(2 * m, SUBLANES, 2 * m):
        mid = jnp.where(sub >= s, row(s + m - 1), mid)
    return -jnp.abs(b3 - mid).reshape(t, LANES)


def _hgrn_pieces(lbl_ref, hq_ref, hf_ref, hi_ref, hg_ref, gn_ref, o_ref, st_ref, r0, t):
    rows = slice(r0, r0 + t)
    half = t // 2
    ctx = {}

    def gates():
        lg = lbl_ref[...]
        e = jnp.exp(lg - jnp.max(lg, axis=0, keepdims=True))
        lb = e[0:1, :] / jnp.sum(e, axis=0, keepdims=True)
        f_raw = hf_ref[rows, :]
        sig = jax.nn.sigmoid(f_raw)
        f = lb + (1.0 - lb) * sig
        g = jnp.log(f)
        ctx["f"] = f
        ctx["kk"] = (1.0 - lb) * (1.0 - sig)

        row = lax.broadcasted_iota(jnp.int32, (t, t), 0)
        col = lax.broadcasted_iota(jnp.int32, (t, t), 1)
        tri = (row >= col).astype(BF16)
        g1 = g.astype(BF16)
        r1 = g - g1.astype(F32)
        g2 = r1.astype(BF16)
        g3 = (r1 - g2.astype(F32)).astype(BF16)
        ctx["b"] = (_dot(tri, g1) + _dot(tri, g2) + _dot(tri, g3)) * LOG2_E

    def levels():
        ri = lax.broadcasted_iota(jnp.int32, (half, half), 0)
        ci = lax.broadcasted_iota(jnp.int32, (half, half), 1)
        xm = jnp.where(ri > ci, ri ^ ci, 0)
        odd_row = (lax.broadcasted_iota(jnp.int32, (t, HG_DIM), 0) & 1) == 1

        sls = [slice(h * HG_DIM, (h + 1) * HG_DIM) for h in range(HG_HEADS)]
        ctx["q"] = [hq_ref[rows, sl] for sl in sls]
        ctx["v"] = [hi_ref[rows, sl] for sl in sls]
        ctx["kb"] = [ctx["kk"][:, sl].astype(BF16) for sl in sls]
        ctx["bh"] = [ctx["b"][:, sl] for sl in sls]

        ps = [[None, None] for _ in sls]
        for i in range(half.bit_length() - 1):
            m = 1 << i
            mask = (xm >= m) & (xm < 2 * m)
            for h, sl in enumerate(sls):
                if m == 1:
                    decay = jnp.where(odd_row, ctx["f"][:, sl], 1.0).astype(BF16)
                else:
                    decay = jnp.exp2(_neg_gap(ctx["bh"][h], m)).astype(BF16)
                q_l, k_l = ctx["q"][h] * decay, ctx["kb"][h] * decay
                for qi, q0 in enumerate((0, half)):
                    s_l = lax.dot_general(q_l[q0:q0 + half], k_l[q0:q0 + half], NT_DIMS,
                                          preferred_element_type=F32)
                    ps[h][qi] = jnp.where(mask, s_l, 0.0 if ps[h][qi] is None else ps[h][qi])
        ctx["diag"] = [[p.astype(BF16) for p in pair] for pair in ps]

    def head(h):
        sl = slice(h * HG_DIM, (h + 1) * HG_DIM)
        q, v, kb, b, diag = ctx["q"][h], ctx["v"][h], ctx["kb"][h], ctx["bh"][h], ctx["diag"][h]
        mid = b[half - 1:half, :]
        cross = lax.dot_general(q[half:] * jnp.exp2(b[half:] - mid).astype(BF16),
                                kb[:half] * jnp.exp2(mid - b[:half]).astype(BF16), NT_DIMS,
                                preferred_element_type=F32).astype(BF16)
        o = jnp.concatenate([_dot(diag[0], v[:half]),
                             _dot(jnp.concatenate([cross, diag[1]], axis=1), v)], axis=0)
        o = o + jnp.sum((q * kb).astype(F32), axis=-1, keepdims=True) * v.astype(F32)

        st = st_ref[h]
        b_last = b[t - 1:t, :]
        o = o + lax.dot_general(q * jnp.exp2(b).astype(BF16), st.astype(BF16), NT_DIMS,
                                preferred_element_type=F32)
        k_end = kb * jnp.exp2(b_last - b).astype(BF16)
        st_ref[h] = st * jnp.exp2(b_last) + lax.dot_general(v, k_end, TN_DIMS, preferred_element_type=F32)

        gate = hg_ref[rows, sl].astype(F32)
        o_ref[rows, sl] = (_rms(o, gn_ref[h:h + 1, :]) * (gate * jax.nn.sigmoid(gate))).astype(o_ref.dtype)

    def heads():
        for h in range(HG_HEADS):
            head(h)

    return [gates, levels, heads]


def _attn_kernel(q_ref, k_ref, v_ref, wg_ref, wu_ref, wd_ref, o_ref, wg_bf_ref, wu_bf_ref, wd_bf_ref,
                 vaug_ref, *, tq):
    wg_bf_ref[...] = wg_ref[...].astype(wg_bf_ref.dtype)
    wu_bf_ref[...] = wu_ref[...].astype(wu_bf_ref.dtype)
    wd_bf_ref[...] = wd_ref[...].astype(wd_bf_ref.dtype)

    seq = q_ref.shape[0]
    for h in range(MLA_HEADS):
        vaug_ref[h, :, 0:V_DIM] = v_ref[:, h * V_DIM:(h + 1) * V_DIM]
        vaug_ref[h, :, V_DIM:2 * V_DIM] = jnp.ones((seq, V_DIM), vaug_ref.dtype)
    row = lax.broadcasted_iota(jnp.int32, (tq, tq), 0)
    col = lax.broadcasted_iota(jnp.int32, (tq, tq), 1)

    def scores(job):
        h, i = job
        lo = i * tq
        cols = slice(h * HEAD_PAD, (h + 1) * HEAD_PAD)
        q = q_ref[lo:lo + tq, cols]
        s_d = lax.dot_general(q, k_ref[lo:lo + tq, cols], NT_DIMS, preferred_element_type=F32)
        s_d = jnp.where(row >= col, s_d, -jnp.inf)
        s_p = lax.dot_general(q, k_ref[0:lo, cols], NT_DIMS, preferred_element_type=F32) if i > 0 else None
        return s_d, s_p

    jobs = [(h, i) for i in range(seq // tq) for h in range(MLA_HEADS)]
    pending = [scores(job) for job in jobs[:ATTN_LOOKAHEAD]]
    for n_done, (h, i) in enumerate(jobs):
        lo = i * tq
        s_d, s_p = pending.pop(0)
        if n_done + ATTN_LOOKAHEAD < len(jobs):
            pending.append(scores(jobs[n_done + ATTN_LOOKAHEAD]))
        m = jnp.max(s_d, axis=-1, keepdims=True)
        if i > 0:
            m = jnp.maximum(m, jnp.max(s_p, axis=-1, keepdims=True))
        acc = _dot(jnp.exp2(s_d - m).astype(BF16), vaug_ref[h, lo:lo + tq, :])
        if i > 0:
            acc += _dot(jnp.exp2(s_p - m).astype(BF16), vaug_ref[h, 0:lo, :])
        o_ref[lo:lo + tq, h * V_DIM:(h + 1) * V_DIM] = (
            acc[:, 0:V_DIM] / acc[:, V_DIM:2 * V_DIM]).astype(o_ref.dtype)


def _postmix_kernel(h1_ref, ohg_ref, omla_ref, p_ref, wo_ref, g2_ref, wg_ref, wu_ref, wd_ref,
                    gple_ref, wpg_ref, wpp_ref, gfin_ref, y_ref):
    tp = h1_ref.shape[0]
    parts = [slice(r, r + TOKEN_TILE) for r in range(0, tp, TOKEN_TILE)] if tp % TOKEN_TILE == 0 else [slice(0, tp)]

    def tail_pieces(rows, h3):
        ctx = {}

        def gate():
            ctx["gate"] = jax.nn.sigmoid(_dot(_rms(h3, gple_ref[...]).astype(BF16), wpg_ref[...]))

        def finish():
            h4 = h3 + ctx["gate"] * _dot(p_ref[rows, :].astype(BF16), wpp_ref[...])
            y_ref[rows, :] = _rms(h4, gfin_ref[...])

        return [gate, finish]

    pending = []
    for rows in parts:
        h2 = (h1_ref[rows, :] + _dot(ohg_ref[rows, :], wo_ref[0:HG_WIDTH, :])
              + _dot(omla_ref[rows, :], wo_ref[HG_WIDTH:HG_WIDTH + MLA_HEADS * V_DIM, :]))
        slots = [c for c in POSTMIX_TAIL_AFTER_FFN_CHUNKS if c < D_FF // FFN_CHUNK][:len(pending)]
        between = {c: [piece] for c, piece in zip(slots, pending)}
        h3 = h2 + 0.5 * _swiglu(_rms(h2, g2_ref[...]).astype(BF16), wg_ref, wu_ref, wd_ref, between)
        for piece in pending[len(slots):]:
            piece()
        pending = tail_pieces(rows, h3)
    for piece in pending:
        piece()


def _resident(shape):
    return pl.BlockSpec(shape, lambda *_: (0,) * len(shape), pipeline_mode=pl.Buffered(1))


def _rows(tile, width):
    return pl.BlockSpec((tile, width), lambda i: (i, 0))


def kernel(x, p, positions, ln_ffn1, w1_gate, w1_up, w1_down, ln_mix, w_in, hg_lb_logits, hg_out_norm,
           q_a_norm, w_q_up, kv_a_norm, w_kv_up, w_out, ln_ffn2, w2_gate, w2_up, w2_down, ln_ple,
           w_ple_gate, w_ple_proj, ln_final):
    bsz, seq, _ = x.shape
    assert p.shape[0] == 1 and hg_lb_logits.shape[0] == 2, "single-layer trunk"
    n = bsz * seq
    tm = min(TOKEN_TILE, seq)
    tc = min(HGRN_CHUNK, seq)
    tq = min(ATTN_BLOCK, seq)
    tp = min(POSTMIX_TILE, n)
    assert seq % tm == 0 and tm % tc == 0 and seq % tq == 0 and n % tp == 0

    x2 = x.reshape(n, D_MODEL)
    pos2 = positions.astype(F32).reshape(n // tm, 1, tm)
    half = ROPE_DIM // 2
    inv_freq = ROPE_THETA ** (-jnp.arange(half, dtype=F32) / half)
    invf = jnp.concatenate([inv_freq, inv_freq, jnp.zeros((LANES - ROPE_DIM,), F32)]).reshape(1, LANES)
    wg1, wu1, wd1 = w1_gate[0].astype(BF16), w1_up[0].astype(BF16), w1_down[0].astype(BF16)
    win = w_in[0].astype(BF16)
    wq = jnp.pad(w_q_up[0].astype(BF16).reshape(Q_LORA, MLA_HEADS, QK_DIM),
                 ((0, 0), (0, 0), (0, HEAD_PAD - QK_DIM))).reshape(Q_LORA, MLA_HEADS * HEAD_PAD)
    wkv = w_kv_up[0].astype(BF16).reshape(KV_LORA, MLA_HEADS, NOPE_DIM + V_DIM)
    wk = wkv[:, :, :NOPE_DIM].reshape(KV_LORA, MLA_HEADS * NOPE_DIM)
    wv = wkv[:, :, NOPE_DIM:].reshape(KV_LORA, MLA_HEADS * V_DIM)
    wo = w_out[0].astype(BF16)
    row = lambda a: a.reshape(1, -1)
    layer0 = lambda w: w.reshape(w.shape[1:])

    steps = n // tm
    tile = lambda width: pl.BlockSpec((tm, width), lambda i: (jnp.minimum(i, steps - 1), 0))
    prev_tile = pl.BlockSpec((tm, HG_WIDTH), lambda i: (jnp.maximum(i - 1, 0), 0))
    h1, qp, kp, vv, o_hg = pl.pallas_call(
        functools.partial(_premix_kernel, tiles_per_row=seq // tm, chunk=tc),
        name="premix",
        grid=(steps + 1,),
        in_specs=[
            tile(D_MODEL), pl.BlockSpec((None, 1, tm), lambda i: (jnp.minimum(i, steps - 1), 0, 0)),
            _resident((1, LANES)), _resident((1, D_MODEL)),
            _resident((D_MODEL, D_FF)), _resident((D_MODEL, D_FF)), _resident((D_FF, D_MODEL)),
            _resident((1, D_MODEL)), _resident((D_MODEL, IN_WIDTH)),
            _resident((1, Q_LORA)), _resident((Q_LORA, MLA_HEADS * HEAD_PAD)), _resident((1, KV_LORA)),
            _resident((KV_LORA, MLA_HEADS * NOPE_DIM)), _resident((KV_LORA, MLA_HEADS * V_DIM)),
            _resident((2, HG_WIDTH)), _resident((HG_HEADS, HG_DIM)),
        ],
        out_specs=[tile(D_MODEL), tile(MLA_HEADS * HEAD_PAD), tile(MLA_HEADS * HEAD_PAD),
                   tile(MLA_HEADS * V_DIM), prev_tile],
        out_shape=[
            jax.ShapeDtypeStruct((n, D_MODEL), F32),
            jax.ShapeDtypeStruct((n, MLA_HEADS * HEAD_PAD), BF16),
            jax.ShapeDtypeStruct((n, MLA_HEADS * HEAD_PAD), BF16),
            jax.ShapeDtypeStruct((n, MLA_HEADS * V_DIM), BF16),
            jax.ShapeDtypeStruct((n, HG_WIDTH), BF16),
        ],
        scratch_shapes=[pltpu.VMEM((tm, HG_WIDTH), BF16), pltpu.VMEM((tm, HG_WIDTH), F32),
                        pltpu.VMEM((tm, HG_WIDTH), BF16), pltpu.VMEM((tm, HG_WIDTH), BF16),
                        pltpu.VMEM((HG_HEADS, HG_DIM, HG_DIM), F32)],
        compiler_params=pltpu.CompilerParams(dimension_semantics=("arbitrary",), vmem_limit_bytes=VMEM_LIMIT),
    )(x2, pos2, invf, row(ln_ffn1[0]), wg1, wu1, wd1, row(ln_mix[0]), win,
      row(q_a_norm[0]), wq, row(kv_a_norm[0]), wk, wv, hg_lb_logits, hg_out_norm[0])

    seq_spec = lambda width: pl.BlockSpec((seq, MLA_HEADS * width), lambda b: (b, 0))
    assert D_MODEL % (bsz * BF16_ROWS) == 0 and D_FF % (bsz * BF16_ROWS) == 0
    slab = lambda rows, cols: pl.BlockSpec((rows // bsz, cols), lambda b: (b, 0))
    o_mla, wg2, wu2, wd2 = pl.pallas_call(
        functools.partial(_attn_kernel, tq=tq),
        name="mla_attn",
        grid=(bsz,),
        in_specs=[seq_spec(HEAD_PAD), seq_spec(HEAD_PAD), seq_spec(V_DIM),
                  slab(D_MODEL, D_FF), slab(D_MODEL, D_FF), slab(D_FF, D_MODEL)],
        out_specs=[seq_spec(V_DIM), slab(D_MODEL, D_FF), slab(D_MODEL, D_FF), slab(D_FF, D_MODEL)],
        out_shape=[jax.ShapeDtypeStruct((n, MLA_HEADS * V_DIM), BF16),
                   jax.ShapeDtypeStruct((D_MODEL, D_FF), BF16), jax.ShapeDtypeStruct((D_MODEL, D_FF), BF16),
                   jax.ShapeDtypeStruct((D_FF, D_MODEL), BF16)],
        scratch_shapes=[pltpu.VMEM((MLA_HEADS, seq, 2 * V_DIM), BF16)],
        compiler_params=pltpu.CompilerParams(dimension_semantics=("parallel",), vmem_limit_bytes=VMEM_LIMIT),
    )(qp, kp, vv, layer0(w2_gate), layer0(w2_up), layer0(w2_down))

    y = pl.pallas_call(
        _postmix_kernel,
        name="postmix",
        grid=(n // tp,),
        in_specs=[
            _rows(tp, D_MODEL), _rows(tp, HG_WIDTH), _rows(tp, MLA_HEADS * V_DIM), _rows(tp, PLE_DIM),
            _resident((HG_WIDTH + MLA_HEADS * V_DIM, D_MODEL)), _resident((1, D_MODEL)),
            _resident((D_MODEL, D_FF)), _resident((D_MODEL, D_FF)), _resident((D_FF, D_MODEL)),
            _resident((1, D_MODEL)), _resident((D_MODEL, D_MODEL)), _resident((PLE_DIM, D_MODEL)),
            _resident((1, D_MODEL)),
        ],
        out_specs=_rows(tp, D_MODEL),
        out_shape=jax.ShapeDtypeStruct((n, D_MODEL), F32),
        compiler_params=pltpu.CompilerParams(dimension_semantics=("parallel",), vmem_limit_bytes=VMEM_LIMIT),
    )(h1, o_hg, o_mla, p[0].reshape(n, PLE_DIM), wo, row(ln_ffn2[0]),
      wg2, wu2, wd2, row(ln_ple[0]), w_ple_gate[0].astype(BF16), w_ple_proj[0].astype(BF16), row(ln_final))

    return y.reshape(bsz, seq, D_MODEL)
```

```python
import functools

import jax
import jax.numpy as jnp
from jax import lax
from jax.experimental import pallas as pl
from jax.experimental.pallas import tpu as pltpu

F32 = jnp.float32
BF16 = jnp.bfloat16

D_MODEL = 1024
D_FF = 2816
PLE_DIM = 256
HG_HEADS = 4
HG_DIM = 128
HG_WIDTH = HG_HEADS * HG_DIM
MLA_HEADS = 4
Q_LORA = 256
KV_LORA = 128
NOPE_DIM = 128
ROPE_DIM = 64
V_DIM = 128
QK_DIM = NOPE_DIM + ROPE_DIM
ROPE_THETA = 10000.0
EPS = 1e-6
LOG2_E = 1.4426950408889634

LANES = 128
SUBLANES = 8
BF16_ROWS = 2 * SUBLANES
HEAD_PAD = 2 * LANES
HG_GROUPS = 4
IN_WIDTH = HG_GROUPS * HG_WIDTH + Q_LORA + KV_LORA + ROPE_DIM

FFN_CHUNK = 256
TOKEN_TILE = 512
POSTMIX_TILE = 1024
HGRN_CHUNK = 256
ATTN_BLOCK = 256
ATTN_LOOKAHEAD = 3
POSTMIX_TAIL_AFTER_FFN_CHUNKS = (1, 4)
HGRN_AFTER_FFN_CHUNKS = (0, 1, 5, 6)
VMEM_LIMIT = 56 * 1024 * 1024

NT_DIMS = (((1,), (1,)), ((), ()))
TN_DIMS = (((0,), (0,)), ((), ()))


def _dot(a, b):
    return jnp.dot(a, b, preferred_element_type=F32)


def _rms(x, g):
    return x * lax.rsqrt(jnp.mean(x * x, axis=-1, keepdims=True) + EPS) * g


def _swiglu(xn, wg_ref, wu_ref, wd_ref, between=None):
    acc = None
    for idx, c in enumerate(range(0, D_FF, FFN_CHUNK)):
        g = _dot(xn, wg_ref[:, c:c + FFN_CHUNK])
        u = _dot(xn, wu_ref[:, c:c + FFN_CHUNK])
        a = (g * jax.nn.sigmoid(g) * u).astype(BF16)
        d = _dot(a, wd_ref[c:c + FFN_CHUNK, :])
        acc = d if acc is None else acc + d
        for piece in (between or {}).get(idx, ()):
            piece()
    return acc


def _rope(x, cos_t, sin_lo, sin_hi):
    return (x * cos_t + pltpu.roll(x, LANES - ROPE_DIM // 2, 1) * sin_lo
            + pltpu.roll(x, ROPE_DIM // 2, 1) * sin_hi)


def _premix_kernel(x_ref, pos_ref, invf_ref, g1_ref, wg_ref, wu_ref, wd_ref, gmix_ref, win_ref,
                   qan_ref, wq_ref, kvan_ref, wk_ref, wv_ref, lbl_ref, gn_ref,
                   h1_ref, qp_ref, kp_ref, v_ref, ohg_ref,
                   hq_s, hf_s, hi_s, hg_s, st_ref, *, tiles_per_row, chunk):
    i = pl.program_id(0)
    tm = x_ref.shape[0]

    @pl.when(i == 0)
    def _():
        hq_s[...] = jnp.zeros_like(hq_s)
        hf_s[...] = jnp.zeros_like(hf_s)
        hi_s[...] = jnp.zeros_like(hi_s)
        hg_s[...] = jnp.zeros_like(hg_s)
        st_ref[...] = jnp.zeros_like(st_ref)

    @pl.when(lax.rem(i + tiles_per_row - 1, tiles_per_row) == 0)
    def _():
        st_ref[...] = jnp.zeros_like(st_ref)

    pieces = []
    for r0 in range(0, tm, chunk):
        pieces += _hgrn_pieces(lbl_ref, hq_s, hf_s, hi_s, hg_s, gn_ref, ohg_ref, st_ref, r0, chunk)
    early_chunks = [c for c in HGRN_AFTER_FFN_CHUNKS if c < D_FF // FFN_CHUNK][:len(pieces)]
    early = {}
    for c, piece in zip(early_chunks, pieces):
        early.setdefault(c, []).append(piece)
    late = iter(pieces[len(early_chunks):])

    t_row = lax.broadcasted_iota(jnp.int32, (tm, tm), 0)
    t_col = lax.broadcasted_iota(jnp.int32, (tm, tm), 1)
    pos_col = jnp.sum(jnp.where(t_row == t_col, pos_ref[...], 0.0), axis=1, keepdims=True)
    ang = pos_col * invf_ref[...]
    lane = lax.broadcasted_iota(jnp.int32, ang.shape, 1)
    cos_a, sin_a = jnp.cos(ang), jnp.sin(ang)
    half = ROPE_DIM // 2
    cos_t = jnp.where(lane < ROPE_DIM, cos_a, 0.0)
    sin_lo = jnp.where(lane < half, -sin_a, 0.0)
    sin_hi = jnp.where((lane >= half) & (lane < ROPE_DIM), sin_a, 0.0)

    x = x_ref[...]
    xn = _rms(x, g1_ref[...]).astype(BF16)
    h1 = x + 0.5 * _swiglu(xn, wg_ref, wu_ref, wd_ref, early)
    h1_ref[...] = h1

    hn = _rms(h1, gmix_ref[...]).astype(BF16)
    w = HG_WIDTH
    new = []
    for j in range(HG_GROUPS):
        new.append(_dot(hn, win_ref[:, j * w:(j + 1) * w]))
        piece = next(late, None)
        if piece is not None:
            piece()
    for piece in late:
        piece()
    c0 = HG_GROUPS * w
    cq = _dot(hn, win_ref[:, c0:c0 + Q_LORA])
    ckv_kr = _dot(hn, win_ref[:, c0 + Q_LORA:IN_WIDTH])
    ckv = ckv_kr[:, 0:KV_LORA]
    kr = jnp.concatenate([ckv_kr[:, KV_LORA:KV_LORA + ROPE_DIM],
                          jnp.zeros((tm, LANES - ROPE_DIM), F32)], axis=1)

    q = _dot(_rms(cq, qan_ref[...]).astype(BF16), wq_ref[...]) * (QK_DIM ** -0.5 * LOG2_E)
    ckvn = _rms(ckv, kvan_ref[...]).astype(BF16)
    k_nope = _dot(ckvn, wk_ref[...])
    v_ref[...] = _dot(ckvn, wv_ref[...]).astype(v_ref.dtype)
    k_rope = _rope(kr, cos_t, sin_lo, sin_hi).astype(kp_ref.dtype)
    for h in range(MLA_HEADS):
        a = h * HEAD_PAD
        qp_ref[:, a:a + NOPE_DIM] = q[:, a:a + NOPE_DIM].astype(qp_ref.dtype)
        qp_ref[:, a + NOPE_DIM:a + HEAD_PAD] = _rope(
            q[:, a + NOPE_DIM:a + HEAD_PAD], cos_t, sin_lo, sin_hi).astype(qp_ref.dtype)
        kp_ref[:, a:a + NOPE_DIM] = k_nope[:, h * NOPE_DIM:(h + 1) * NOPE_DIM].astype(kp_ref.dtype)
        kp_ref[:, a + NOPE_DIM:a + HEAD_PAD] = k_rope

    hq_s[...] = new[0].astype(hq_s.dtype)
    hf_s[...] = new[1]
    hi_s[...] = new[2].astype(hi_s.dtype)
    hg_s[...] = new[3].astype(hg_s.dtype)


def _neg_gap(b, m):
    t = b.shape[0]
    if m >= SUBLANES:
        pieces = []
        for s in range(0, t, 2 * m):
            mid = b[s + m - 1:s + m, :]
            pieces += [mid - b[s:s + m], b[s + m:s + 2 * m] - mid]
        return jnp.concatenate(pieces, axis=0)
    b3 = b.reshape(t // SUBLANES, SUBLANES, LANES)
    sub = lax.broadcasted_iota(jnp.int32, b3.shape, 1)

    def row(i):
        return jnp.broadcast_to(b3[:, i:i + 1, :], b3.shape)

    mid = row(m - 1)
    for s in range(2 * m, SUBLANES, 2 * m):
        mid = jnp.where(sub >= s, row(s + m - 1), mid)
    return -jnp.abs(b3 - mid).reshape(t, LANES)


def _hgrn_pieces(lbl_ref, hq_ref, hf_ref, hi_ref, hg_ref, gn_ref, o_ref, st_ref, r0, t):
    rows = slice(r0, r0 + t)
    half = t // 2
    ctx = {}

    def gates():
        lg = lbl_ref[...]
        e = jnp.exp(lg - jnp.max(lg, axis=0, keepdims=True))
        lb = e[0:1, :] / jnp.sum(e, axis=0, keepdims=True)
        f_raw = hf_ref[rows, :]
        sig = jax.nn.sigmoid(f_raw)
        f = lb + (1.0 - lb) * sig
        g = jnp.log(f)
        ctx["f"] = f
        ctx["kk"] = (1.0 - lb) * (1.0 - sig)

        row = lax.broadcasted_iota(jnp.int32, (t, t), 0)
        col = lax.broadcasted_iota(jnp.int32, (t, t), 1)
        tri = (row >= col).astype(BF16)
        g1 = g.astype(BF16)
        r1 = g - g1.astype(F32)
        g2 = r1.astype(BF16)
        g3 = (r1 - g2.astype(F32)).astype(BF16)
        ctx["b"] = (_dot(tri, g1) + _dot(tri, g2) + _dot(tri, g3)) * LOG2_E

    def levels():
        ri = lax.broadcasted_iota(jnp.int32, (half, half), 0)
        ci = lax.broadcasted_iota(jnp.int32, (half, half), 1)
        xm = jnp.where(ri > ci, ri ^ ci, 0)
        odd_row = (lax.broadcasted_iota(jnp.int32, (t, HG_DIM), 0) & 1) == 1

        sls = [slice(h * HG_DIM, (h + 1) * HG_DIM) for h in range(HG_HEADS)]
        ctx["q"] = [hq_ref[rows, sl] for sl in sls]
        ctx["v"] = [hi_ref[rows, sl] for sl in sls]
        ctx["kb"] = [ctx["kk"][:, sl].astype(BF16) for sl in sls]
        ctx["bh"] = [ctx["b"][:, sl] for sl in sls]

        ps = [[None, None] for _ in sls]
        for i in range(half.bit_length() - 1):
            m = 1 << i
            mask = (xm >= m) & (xm < 2 * m)
            for h, sl in enumerate(sls):
                if m == 1:
                    decay = jnp.where(odd_row, ctx["f"][:, sl], 1.0).astype(BF16)
                else:
                    decay = jnp.exp2(_neg_gap(ctx["bh"][h], m)).astype(BF16)
                q_l, k_l = ctx["q"][h] * decay, ctx["kb"][h] * decay
                for qi, q0 in enumerate((0, half)):
                    s_l = lax.dot_general(q_l[q0:q0 + half], k_l[q0:q0 + half], NT_DIMS,
                                          preferred_element_type=F32)
                    ps[h][qi] = jnp.where(mask, s_l, 0.0 if ps[h][qi] is None else ps[h][qi])
        ctx["diag"] = [[p.astype(BF16) for p in pair] for pair in ps]

    def head(h):
        sl = slice(h * HG_DIM, (h + 1) * HG_DIM)
        q, v, kb, b, diag = ctx["q"][h], ctx["v"][h], ctx["kb"][h], ctx["bh"][h], ctx["diag"][h]
        mid = b[half - 1:half, :]
        cross = lax.dot_general(q[half:] * jnp.exp2(b[half:] - mid).astype(BF16),
                                kb[:half] * jnp.exp2(mid - b[:half]).astype(BF16), NT_DIMS,
                                preferred_element_type=F32).astype(BF16)
        o = jnp.concatenate([_dot(diag[0], v[:half]),
                             _dot(jnp.concatenate([cross, diag[1]], axis=1), v)], axis=0)
        o = o + jnp.sum((q * kb).astype(F32), axis=-1, keepdims=True) * v.astype(F32)

        st = st_ref[h]
        b_last = b[t - 1:t, :]
        o = o + lax.dot_general(q * jnp.exp2(b).astype(BF16), st.astype(BF16), NT_DIMS,
                                preferred_element_type=F32)
        k_end = kb * jnp.exp2(b_last - b).astype(BF16)
        st_ref[h] = st * jnp.exp2(b_last) + lax.dot_general(v, k_end, TN_DIMS, preferred_element_type=F32)

        gate = hg_ref[rows, sl].astype(F32)
        o_ref[rows, sl] = (_rms(o, gn_ref[h:h + 1, :]) * (gate * jax.nn.sigmoid(gate))).astype(o_ref.dtype)

    def heads():
        levels()
        for h in range(HG_HEADS):
            head(h)

    return [gates, heads]


def _attn_kernel(q_ref, k_ref, v_ref, wg_ref, wu_ref, wd_ref, o_ref, wg_bf_ref, wu_bf_ref, wd_bf_ref,
                 vaug_ref, *, tq):
    wg_bf_ref[...] = wg_ref[...].astype(wg_bf_ref.dtype)
    wu_bf_ref[...] = wu_ref[...].astype(wu_bf_ref.dtype)
    wd_bf_ref[...] = wd_ref[...].astype(wd_bf_ref.dtype)

    seq = q_ref.shape[0]
    for h in range(MLA_HEADS):
        vaug_ref[h, :, 0:V_DIM] = v_ref[:, h * V_DIM:(h + 1) * V_DIM]
        vaug_ref[h, :, V_DIM:2 * V_DIM] = jnp.ones((seq, V_DIM), vaug_ref.dtype)
    row = lax.broadcasted_iota(jnp.int32, (tq, tq), 0)
    col = lax.broadcasted_iota(jnp.int32, (tq, tq), 1)

    def scores(job):
        h, i = job
        lo = i * tq
        cols = slice(h * HEAD_PAD, (h + 1) * HEAD_PAD)
        q = q_ref[lo:lo + tq, cols]
        s_d = lax.dot_general(q, k_ref[lo:lo + tq, cols], NT_DIMS, preferred_element_type=F32)
        s_d = jnp.where(row >= col, s_d, -jnp.inf)
        s_p = lax.dot_general(q, k_ref[0:lo, cols], NT_DIMS, preferred_element_type=F32) if i > 0 else None
        return s_d, s_p

    jobs = [(h, i) for i in range(seq // tq) for h in range(MLA_HEADS)]
    pending = [scores(job) for job in jobs[:ATTN_LOOKAHEAD]]
    for n_done, (h, i) in enumerate(jobs):
        lo = i * tq
        s_d, s_p = pending.pop(0)
        if n_done + ATTN_LOOKAHEAD < len(jobs):
            pending.append(scores(jobs[n_done + ATTN_LOOKAHEAD]))
        m = jnp.max(s_d, axis=-1, keepdims=True)
        if i > 0:
            m = jnp.maximum(m, jnp.max(s_p, axis=-1, keepdims=True))
        acc = _dot(jnp.exp2(s_d - m).astype(BF16), vaug_ref[h, lo:lo + tq, :])
        if i > 0:
            acc += _dot(jnp.exp2(s_p - m).astype(BF16), vaug_ref[h, 0:lo, :])
        o_ref[lo:lo + tq, h * V_DIM:(h + 1) * V_DIM] = (
            acc[:, 0:V_DIM] / acc[:, V_DIM:2 * V_DIM]).astype(o_ref.dtype)


def _postmix_kernel(h1_ref, ohg_ref, omla_ref, p_ref, wo_ref, g2_ref, wg_ref, wu_ref, wd_ref,
                    gple_ref, wpg_ref, wpp_ref, gfin_ref, y_ref):
    tp = h1_ref.shape[0]
    parts = [slice(r, r + TOKEN_TILE) for r in range(0, tp, TOKEN_TILE)] if tp % TOKEN_TILE == 0 else [slice(0, tp)]

    def tail_pieces(rows, h3):
        ctx = {}

        def gate():
            ctx["gate"] = jax.nn.sigmoid(_dot(_rms(h3, gple_ref[...]).astype(BF16), wpg_ref[...]))

        def finish():
            h4 = h3 + ctx["gate"] * _dot(p_ref[rows, :].astype(BF16), wpp_ref[...])
            y_ref[rows, :] = _rms(h4, gfin_ref[...])

        return [gate, finish]

    pending = []
    for rows in parts:
        h2 = (h1_ref[rows, :] + _dot(ohg_ref[rows, :], wo_ref[0:HG_WIDTH, :])
              + _dot(omla_ref[rows, :], wo_ref[HG_WIDTH:HG_WIDTH + MLA_HEADS * V_DIM, :]))
        slots = [c for c in POSTMIX_TAIL_AFTER_FFN_CHUNKS if c < D_FF // FFN_CHUNK][:len(pending)]
        between = {c: [piece] for c, piece in zip(slots, pending)}
        h3 = h2 + 0.5 * _swiglu(_rms(h2, g2_ref[...]).astype(BF16), wg_ref, wu_ref, wd_ref, between)
        for piece in pending[len(slots):]:
            piece()
        pending = tail_pieces(rows, h3)
    for piece in pending:
        piece()


def _resident(shape):
    return pl.BlockSpec(shape, lambda *_: (0,) * len(shape), pipeline_mode=pl.Buffered(1))


def _rows(tile, width):
    return pl.BlockSpec((tile, width), lambda i: (i, 0))


def kernel(x, p, positions, ln_ffn1, w1_gate, w1_up, w1_down, ln_mix, w_in, hg_lb_logits, hg_out_norm,
           q_a_norm, w_q_up, kv_a_norm, w_kv_up, w_out, ln_ffn2, w2_gate, w2_up, w2_down, ln_ple,
           w_ple_gate, w_ple_proj, ln_final):
    bsz, seq, _ = x.shape
    assert p.shape[0] == 1 and hg_lb_logits.shape[0] == 2, "single-layer trunk"
    n = bsz * seq
    tm = min(TOKEN_TILE, seq)
    tc = min(HGRN_CHUNK, seq)
    tq = min(ATTN_BLOCK, seq)
    tp = min(POSTMIX_TILE, n)
    assert seq % tm == 0 and tm % tc == 0 and seq % tq == 0 and n % tp == 0

    x2 = x.reshape(n, D_MODEL)
    pos2 = positions.astype(F32).reshape(n // tm, 1, tm)
    half = ROPE_DIM // 2
    inv_freq = ROPE_THETA ** (-jnp.arange(half, dtype=F32) / half)
    invf = jnp.concatenate([inv_freq, inv_freq, jnp.zeros((LANES - ROPE_DIM,), F32)]).reshape(1, LANES)
    wg1, wu1, wd1 = w1_gate[0].astype(BF16), w1_up[0].astype(BF16), w1_down[0].astype(BF16)
    win = w_in[0].astype(BF16)
    wq = jnp.pad(w_q_up[0].astype(BF16).reshape(Q_LORA, MLA_HEADS, QK_DIM),
                 ((0, 0), (0, 0), (0, HEAD_PAD - QK_DIM))).reshape(Q_LORA, MLA_HEADS * HEAD_PAD)
    wkv = w_kv_up[0].astype(BF16).reshape(KV_LORA, MLA_HEADS, NOPE_DIM + V_DIM)
    wk = wkv[:, :, :NOPE_DIM].reshape(KV_LORA, MLA_HEADS * NOPE_DIM)
    wv = wkv[:, :, NOPE_DIM:].reshape(KV_LORA, MLA_HEADS * V_DIM)
    wo = w_out[0].astype(BF16)
    row = lambda a: a.reshape(1, -1)
    layer0 = lambda w: w.reshape(w.shape[1:])

    steps = n // tm
    tile = lambda width: pl.BlockSpec((tm, width), lambda i: (jnp.minimum(i, steps - 1), 0))
    prev_tile = pl.BlockSpec((tm, HG_WIDTH), lambda i: (jnp.maximum(i - 1, 0), 0))
    h1, qp, kp, vv, o_hg = pl.pallas_call(
        functools.partial(_premix_kernel, tiles_per_row=seq // tm, chunk=tc),
        name="premix",
        grid=(steps + 1,),
        in_specs=[
            tile(D_MODEL), pl.BlockSpec((None, 1, tm), lambda i: (jnp.minimum(i, steps - 1), 0, 0)),
            _resident((1, LANES)), _resident((1, D_MODEL)),
            _resident((D_MODEL, D_FF)), _resident((D_MODEL, D_FF)), _resident((D_FF, D_MODEL)),
            _resident((1, D_MODEL)), _resident((D_MODEL, IN_WIDTH)),
            _resident((1, Q_LORA)), _resident((Q_LORA, MLA_HEADS * HEAD_PAD)), _resident((1, KV_LORA)),
            _resident((KV_LORA, MLA_HEADS * NOPE_DIM)), _resident((KV_LORA, MLA_HEADS * V_DIM)),
            _resident((2, HG_WIDTH)), _resident((HG_HEADS, HG_DIM)),
        ],
        out_specs=[tile(D_MODEL), tile(MLA_HEADS * HEAD_PAD), tile(MLA_HEADS * HEAD_PAD),
                   tile(MLA_HEADS * V_DIM), prev_tile],
        out_shape=[
            jax.ShapeDtypeStruct((n, D_MODEL), F32),
            jax.ShapeDtypeStruct((n, MLA_HEADS * HEAD_PAD), BF16),
            jax.ShapeDtypeStruct((n, MLA_HEADS * HEAD_PAD), BF16),
            jax.ShapeDtypeStruct((n, MLA_HEADS * V_DIM), BF16),
            jax.ShapeDtypeStruct((n, HG_WIDTH), BF16),
        ],
        scratch_shapes=[pltpu.VMEM((tm, HG_WIDTH), BF16), pltpu.VMEM((tm, HG_WIDTH), F32),
                        pltpu.VMEM((tm, HG_WIDTH), BF16), pltpu.VMEM((tm, HG_WIDTH), BF16),
                        pltpu.VMEM((HG_HEADS, HG_DIM, HG_DIM), F32)],
        compiler_params=pltpu.CompilerParams(dimension_semantics=("arbitrary",), vmem_limit_bytes=VMEM_LIMIT),
    )(x2, pos2, invf, row(ln_ffn1[0]), wg1, wu1, wd1, row(ln_mix[0]), win,
      row(q_a_norm[0]), wq, row(kv_a_norm[0]), wk, wv, hg_lb_logits, hg_out_norm[0])

    seq_spec = lambda width: pl.BlockSpec((seq, MLA_HEADS * width), lambda b: (b, 0))
    assert D_MODEL % (bsz * BF16_ROWS) == 0 and D_FF % (bsz * BF16_ROWS) == 0
    slab = lambda rows, cols: pl.BlockSpec((rows // bsz, cols), lambda b: (b, 0))
    o_mla, wg2, wu2, wd2 = pl.pallas_call(
        functools.partial(_attn_kernel, tq=tq),
        name="mla_attn",
        grid=(bsz,),
        in_specs=[seq_spec(HEAD_PAD), seq_spec(HEAD_PAD), seq_spec(V_DIM),
                  slab(D_MODEL, D_FF), slab(D_MODEL, D_FF), slab(D_FF, D_MODEL)],
        out_specs=[seq_spec(V_DIM), slab(D_MODEL, D_FF), slab(D_MODEL, D_FF), slab(D_FF, D_MODEL)],
        out_shape=[jax.ShapeDtypeStruct((n, MLA_HEADS * V_DIM), BF16),
                   jax.ShapeDtypeStruct((D_MODEL, D_FF), BF16), jax.ShapeDtypeStruct((D_MODEL, D_FF), BF16),
                   jax.ShapeDtypeStruct((D_FF, D_MODEL), BF16)],
        scratch_shapes=[pltpu.VMEM((MLA_HEADS, seq, 2 * V_DIM), BF16)],
        compiler_params=pltpu.CompilerParams(dimension_semantics=("parallel",), vmem_limit_bytes=VMEM_LIMIT),
    )(qp, kp, vv, layer0(w2_gate), layer0(w2_up), layer0(w2_down))

    y = pl.pallas_call(
        _postmix_kernel,
        name="postmix",
        grid=(n // tp,),
        in_specs=[
            _rows(tp, D_MODEL), _rows(tp, HG_WIDTH), _rows(tp, MLA_HEADS * V_DIM), _rows(tp, PLE_DIM),
            _resident((HG_WIDTH + MLA_HEADS * V_DIM, D_MODEL)), _resident((1, D_MODEL)),
            _resident((D_MODEL, D_FF)), _resident((D_MODEL, D_FF)), _resident((D_FF, D_MODEL)),
            _resident((1, D_MODEL)), _resident((D_MODEL, D_MODEL)), _resident((PLE_DIM, D_MODEL)),
            _resident((1, D_MODEL)),
        ],
        out_specs=_rows(tp, D_MODEL),
        out_shape=jax.ShapeDtypeStruct((n, D_MODEL), F32),
        compiler_params=pltpu.CompilerParams(dimension_semantics=("parallel",), vmem_limit_bytes=VMEM_LIMIT),
    )(h1, o_hg, o_mla, p[0].reshape(n, PLE_DIM), wo, row(ln_ffn2[0]),
      wg2, wu2, wd2, row(ln_ple[0]), w_ple_gate[0].astype(BF16), w_ple_proj[0].astype(BF16), row(ln_final))

    return y.reshape(bsz, seq, D_MODEL)
```
